```python
import math
import jax
import jax.numpy as jnp
from jax import lax
import numpy as np

D_MODEL = 2048
BATCH = 32
SEQ = 256
DEPTH = 4
DEC_BATCH = 8
DEC_SEQ = 2048
PAST_LEN = 512

GRID_W = 64
N_MIXERS = 3
N_HEADS = 16
KV_HEADS = 4
HEAD_DIM = 128
GROUP = N_HEADS // KV_HEADS
NQ = N_HEADS * HEAD_DIM
NKV = KV_HEADS * HEAD_DIM
WINDOW = 128
Q_BLOCK = 128
BAND = Q_BLOCK + 2 * WINDOW
DIFF_HEADS = 8
DIFF_KV_HEADS = 2
DIFF_GROUP = DIFF_HEADS // DIFF_KV_HEADS
DIFF_DIM = 128
NQD = DIFF_HEADS * 2 * DIFF_DIM
NKD = DIFF_KV_HEADS * 2 * DIFF_DIM
D_FF = -(-8 * D_MODEL // (3 * 256)) * 256
ROPE_THETA = 10000.0
EPS = 1e-6
NEG_INF = -1e30
N_FULL = (DEPTH - 0 + N_MIXERS - 1) // N_MIXERS
N_WIN = (DEPTH - 1 + N_MIXERS - 1) // N_MIXERS
N_DIFF = (DEPTH - 2 + N_MIXERS - 1) // N_MIXERS

kernel_name = 'hybrid_diffusion_prefix_trunk_step'


def rms_norm(x, g):
    xf = x.astype(jnp.float32)
    y = xf * lax.rsqrt(jnp.mean(xf * xf, axis=-1, keepdims=True) + EPS)
    return (y * g.astype(jnp.float32)).astype(x.dtype)


def grid_positions(n_tokens):
    n_rows = n_tokens // GRID_W
    rows = jnp.repeat(jnp.arange(n_rows, dtype=jnp.int32), GRID_W)
    cols = jnp.tile(jnp.arange(GRID_W, dtype=jnp.int32), n_rows)
    return rows, cols


def rope_2d(x, rows, cols):
    hd = x.shape[-1]
    half = hd // 2
    nf = half // 2
    inv = 1.0 / (ROPE_THETA ** (jnp.arange(nf, dtype=jnp.float32) / nf))
    bshape = (1, x.shape[1]) + (1,) * (x.ndim - 3) + (nf,)
    xf = x.astype(jnp.float32)

    def rot(xh, pos):
        ang = (pos.astype(jnp.float32)[:, None] * inv).reshape(bshape)
        cos, sin = jnp.cos(ang), jnp.sin(ang)
        x1, x2 = xh[..., :nf], xh[..., nf:]
        return jnp.concatenate([x1 * cos - x2 * sin, x1 * sin + x2 * cos], axis=-1)

    out = jnp.concatenate([rot(xf[..., :half], rows), rot(xf[..., half:], cols)], axis=-1)
    return out.astype(x.dtype)


def sweep_query_blocks(fn, q):
    b, s = q.shape[:2]
    nb = s // Q_BLOCK
    qb = jnp.moveaxis(q.reshape((b, nb, Q_BLOCK) + q.shape[2:]), 1, 0)
    out = lax.map(lambda a: fn(a[0], a[1]), (jnp.arange(nb), qb))
    return jnp.moveaxis(out, 0, 1).reshape((b, s) + out.shape[3:])


def gqa_softmax(qb, keys, vals, mask=None, sink=None):
    s = jnp.einsum('bqhgd,bkhd->bhgqk', qb, keys).astype(jnp.float32) * (qb.shape[-1] ** -0.5)
    if mask is not None:
        s = jnp.where(mask, s, NEG_INF)
    if sink is not None:
        sk = jnp.broadcast_to(sink.astype(jnp.float32).reshape(1, s.shape[1], s.shape[2], 1, 1), s.shape[:-1] + (1,))
        p = jax.nn.softmax(jnp.concatenate([s, sk], axis=-1), axis=-1)[..., :-1]
    else:
        p = jax.nn.softmax(s, axis=-1)
    return jnp.einsum('bhgqk,bkhd->bqhgd', p.astype(vals.dtype), vals)


def full_mixer(h, ctx_kv, pos, w_qkv, w_o, g_q, g_k):
    b, s, _ = h.shape
    q, k, v = jnp.split(h @ w_qkv, [NQ, NQ + NKV], axis=-1)
    q = rms_norm(q.reshape(b, s, N_HEADS, HEAD_DIM), g_q)
    k = rms_norm(k.reshape(b, s, KV_HEADS, HEAD_DIM), g_k)
    v = v.reshape(b, s, KV_HEADS, HEAD_DIM)
    if pos is None:
        keys, vals = k, v
    else:
        q = rope_2d(q, *pos)
        k = rope_2d(k, *pos)
        keys = jnp.concatenate([k, ctx_kv[0]], axis=1)
        vals = jnp.concatenate([v, ctx_kv[1]], axis=1)
    q5 = q.reshape(b, s, KV_HEADS, GROUP, HEAD_DIM)
    o = sweep_query_blocks(lambda i, qb: gqa_softmax(qb, keys, vals), q5)
    return o.reshape(b, s, NQ) @ w_o, k, v


def window_mixer(h, ctx_kv, pos, w_qkv, w_o, sink):
    b, s, _ = h.shape
    q, k, v = jnp.split(h @ w_qkv, [NQ, NQ + NKV], axis=-1)
    q = q.reshape(b, s, N_HEADS, HEAD_DIM)
    k = k.reshape(b, s, KV_HEADS, HEAD_DIM)
    v = v.reshape(b, s, KV_HEADS, HEAD_DIM)
    if pos is None:
        q5 = q.reshape(b, s, KV_HEADS, GROUP, HEAD_DIM)
        o = sweep_query_blocks(lambda i, qb: gqa_softmax(qb, k, v, sink=sink), q5)
    else:
        q = rope_2d(q, *pos)
        k = rope_2d(k, *pos)
        q5 = q.reshape(b, s, KV_HEADS, GROUP, HEAD_DIM)
        kc, vc = ctx_kv
        n_ctx = kc.shape[1]
        padw = ((0, 0), (WINDOW, WINDOW), (0, 0), (0, 0))
        kp, vp = jnp.pad(k, padw), jnp.pad(v, padw)

        def blk(i, qb):
            start = i * Q_BLOCK
            kb = lax.dynamic_slice_in_dim(kp, start, BAND, axis=1)
            vb = lax.dynamic_slice_in_dim(vp, start, BAND, axis=1)
            qpos = start + jnp.arange(Q_BLOCK)
            kpos = start - WINDOW + jnp.arange(BAND)
            band = (jnp.abs(qpos[:, None] - kpos[None, :]) <= WINDOW) & (kpos >= 0)[None, :] & (kpos < s)[None, :]
            mask = jnp.concatenate([band, jnp.ones((Q_BLOCK, n_ctx), dtype=bool)], axis=1)
            return gqa_softmax(qb, jnp.concatenate([kb, kc], axis=1), jnp.concatenate([vb, vc], axis=1), mask, sink)

        o = sweep_query_blocks(blk, q5)
    return o.reshape(b, s, NQ) @ w_o, k, v


def diff_lambda_init(layer):
    return 0.8 - 0.6 * math.exp(-0.3 * layer)


def diff_mixer(h, ctx_kv, pos, w_qkv, w_o, lam_q1, lam_k1, lam_q2, lam_k2, g_subln, lam_init):
    b, s, _ = h.shape
    q, k, v = jnp.split(h @ w_qkv, [NQD, NQD + NKD], axis=-1)
    q = q.reshape(b, s, DIFF_HEADS, 2, DIFF_DIM)
    k = k.reshape(b, s, DIFF_KV_HEADS, 2, DIFF_DIM)
    v = v.reshape(b, s, DIFF_KV_HEADS, 2 * DIFF_DIM)
    if pos is None:
        keys, vals = k, v
    else:
        q = rope_2d(q, *pos)
        k = rope_2d(k, *pos)
        keys = jnp.concatenate([k, ctx_kv[0]], axis=1)
        vals = jnp.concatenate([v, ctx_kv[1]], axis=1)
    lam = (jnp.exp(jnp.sum(lam_q1.astype(jnp.float32) * lam_k1.astype(jnp.float32)))
           - jnp.exp(jnp.sum(lam_q2.astype(jnp.float32) * lam_k2.astype(jnp.float32))) + lam_init)
    q6 = q.reshape(b, s, DIFF_KV_HEADS, DIFF_GROUP, 2, DIFF_DIM)

    def blk(i, qb):
        sc = jnp.einsum('bqhgmd,bkhmd->bhgmqk', qb, keys).astype(jnp.float32) * (DIFF_DIM ** -0.5)
        p = jax.nn.softmax(sc, axis=-1)
        w = p[:, :, :, 0] - lam * p[:, :, :, 1]
        return jnp.einsum('bhgqk,bkhe->bqhge', w.astype(vals.dtype), vals)

    o = sweep_query_blocks(blk, q6)
    o = rms_norm(o, g_subln) * (1.0 - lam_init)
    return o.reshape(b, s, NQD) @ w_o, k, v


def swiglu(h, w_gate, w_up, w_down):
    return (jax.nn.silu(h @ w_gate) * (h @ w_up)) @ w_down


def run_trunk(x, cond, caches, pos, p):
    ctx_mode = caches is None
    kept = ([], [], [], [], [], [])
    for layer in range(DEPTH):
        kind, slot = layer % N_MIXERS, layer // N_MIXERS
        mod = jax.nn.silu(cond) @ p['w_ada'][layer] + p['b_ada'][layer]
        sh1, sc1, gt1, sh2, sc2, gt2 = [m[:, None, :] for m in jnp.split(mod, 6, axis=-1)]
        g = p['g_norm'][layer]
        h = rms_norm(x, g[0]) * (1.0 + sc1) + sh1
        ctx_kv = None if ctx_mode else (caches[2 * kind][:, slot], caches[2 * kind + 1][:, slot])
        if kind == 0:
            m, k, v = full_mixer(h, ctx_kv, pos, p['w_qkv_full'][slot], p['w_o_full'][slot],
                                 p['g_q_full'][slot], p['g_k_full'][slot])
        elif kind == 1:
            m, k, v = window_mixer(h, ctx_kv, pos, p['w_qkv_win'][slot], p['w_o_win'][slot], p['sink_win'][slot])
        else:
            m, k, v = diff_mixer(h, ctx_kv, pos, p['w_qkv_diff'][slot], p['w_o_diff'][slot],
                                 p['lam_q1'][slot], p['lam_k1'][slot], p['lam_q2'][slot], p['lam_k2'][slot],
                                 p['g_subln_diff'][slot], diff_lambda_init(layer))
        if ctx_mode:
            kept[2 * kind].append(k)
            kept[2 * kind + 1].append(v)
        x = x + gt1 * rms_norm(m, g[1])
        h = rms_norm(x, g[2]) * (1.0 + sc2) + sh2
        x = x + gt2 * rms_norm(swiglu(h, p['w_gate'][layer], p['w_up'][layer], p['w_down'][layer]), g[3])
    new_state = [jnp.stack(a, axis=1) for a in kept] if ctx_mode else None
    return x, new_state


def setup_inputs(seed: int = 0) -> dict:
    key = jax.random.key(seed)
    ks = iter(jax.random.split(key, 40))

    def nrm(shape, scale):
        return scale * jax.random.normal(next(ks), shape, jnp.float32)

    d = D_MODEL
    return {
        'x_prompt': nrm((BATCH, SEQ, d), 1.0),
        'x_sample': nrm((DEC_BATCH, DEC_SEQ, d), 1.0),
        'c': nrm((DEC_BATCH, d), 1.0),
        'cache_k_full': nrm((DEC_BATCH, N_FULL, PAST_LEN, KV_HEADS, HEAD_DIM), 1.0),
        'cache_v_full': nrm((DEC_BATCH, N_FULL, PAST_LEN, KV_HEADS, HEAD_DIM), 1.0),
        'cache_k_win': nrm((DEC_BATCH, N_WIN, PAST_LEN, KV_HEADS, HEAD_DIM), 1.0),
        'cache_v_win': nrm((DEC_BATCH, N_WIN, PAST_LEN, KV_HEADS, HEAD_DIM), 1.0),
        'cache_k_diff': nrm((DEC_BATCH, N_DIFF, PAST_LEN, DIFF_KV_HEADS, 2, DIFF_DIM), 1.0),
        'cache_v_diff': nrm((DEC_BATCH, N_DIFF, PAST_LEN, DIFF_KV_HEADS, 2 * DIFF_DIM), 1.0),
        'c_ctx': nrm((d,), 1.0),
        'w_ada': nrm((DEPTH, d, 6 * d), 0.5 * d ** -0.5),
        'b_ada': nrm((DEPTH, 6 * d), 0.01),
        'g_norm': 1.0 + nrm((DEPTH, 4, d), 0.02),
        'w_qkv_full': nrm((N_FULL, d, NQ + 2 * NKV), d ** -0.5),
        'w_o_full': nrm((N_FULL, NQ, d), NQ ** -0.5),
        'g_q_full': 1.0 + nrm((N_FULL, HEAD_DIM), 0.02),
        'g_k_full': 1.0 + nrm((N_FULL, HEAD_DIM), 0.02),
        'w_qkv_win': nrm((N_WIN, d, NQ + 2 * NKV), d ** -0.5),
        'w_o_win': nrm((N_WIN, NQ, d), NQ ** -0.5),
        'sink_win': nrm((N_WIN, N_HEADS), 0.5),
        'w_qkv_diff': nrm((N_DIFF, d, NQD + 2 * NKD), d ** -0.5),
        'w_o_diff': nrm((N_DIFF, NQD, d), NQD ** -0.5),
        'lam_q1': nrm((N_DIFF, DIFF_DIM), 0.1),
        'lam_k1': nrm((N_DIFF, DIFF_DIM), 0.1),
        'lam_q2': nrm((N_DIFF, DIFF_DIM), 0.1),
        'lam_k2': nrm((N_DIFF, DIFF_DIM), 0.1),
        'g_subln_diff': 1.0 + nrm((N_DIFF, 2 * DIFF_DIM), 0.02),
        'w_gate': nrm((DEPTH, d, D_FF), d ** -0.5),
        'w_up': nrm((DEPTH, d, D_FF), d ** -0.5),
        'w_down': nrm((DEPTH, D_FF, d), D_FF ** -0.5),
    }


def reference(x_prompt, x_sample, c, cache_k_full, cache_v_full, cache_k_win, cache_v_win,
              cache_k_diff, cache_v_diff, c_ctx, w_ada, b_ada, g_norm, w_qkv_full, w_o_full,
              g_q_full, g_k_full, w_qkv_win, w_o_win, sink_win, w_qkv_diff, w_o_diff,
              lam_q1, lam_k1, lam_q2, lam_k2, g_subln_diff, w_gate, w_up, w_down):
    p = {'w_ada': w_ada, 'b_ada': b_ada, 'g_norm': g_norm,
         'w_qkv_full': w_qkv_full, 'w_o_full': w_o_full, 'g_q_full': g_q_full, 'g_k_full': g_k_full,
         'w_qkv_win': w_qkv_win, 'w_o_win': w_o_win, 'sink_win': sink_win,
         'w_qkv_diff': w_qkv_diff, 'w_o_diff': w_o_diff, 'lam_q1': lam_q1, 'lam_k1': lam_k1,
         'lam_q2': lam_q2, 'lam_k2': lam_k2, 'g_subln_diff': g_subln_diff,
         'w_gate': w_gate, 'w_up': w_up, 'w_down': w_down}
    y_prompt, ns = run_trunk(x_prompt, c_ctx[None, :], None, None, p)
    caches = (cache_k_full, cache_v_full, cache_k_win, cache_v_win, cache_k_diff, cache_v_diff)
    pos = grid_positions(x_sample.shape[1])
    y_sample, _ = run_trunk(x_sample, c, caches, pos, p)
    return (y_prompt, y_sample, ns[0], ns[1], ns[2], ns[3], ns[4], ns[5])
```

```python
import functools
import math

import jax
import jax.numpy as jnp
from jax import lax
from jax.experimental import pallas as pl
from jax.experimental.pallas import tpu as pltpu

F32 = jnp.float32
BF16 = jnp.bfloat16

N_MIXERS = 3
N_HEADS = 16
KV_HEADS = 4
HEAD_DIM = 128
GROUP = N_HEADS // KV_HEADS
WINDOW = 128
GRID_W = 64
DIFF_HEADS = 8
DIFF_KV_HEADS = 2
DIFF_GROUP = DIFF_HEADS // DIFF_KV_HEADS
DIFF_DIM = 128
ROPE_THETA = 10000.0
EPS = 1e-6
NEG_INF = -1e30

LANES = 128
SUBLANES = 8
V7X_VMEM_BYTES = 64 * 1024 * 1024
VMEM_LIMIT = V7X_VMEM_BYTES - 8 * 1024 * 1024

NT_DIMS = (((1,), (1,)), ((), ()))


def _params(*semantics):
    return pltpu.CompilerParams(dimension_semantics=semantics, vmem_limit_bytes=VMEM_LIMIT)


def _rms(x):
    return x * lax.rsqrt(jnp.mean(x * x, axis=-1, keepdims=True) + EPS)


def _mod_kernel(c_ref, w_ref, b_ref, o_ref):
    c = c_ref[...]
    a = (c * jax.nn.sigmoid(c)).astype(BF16)
    o_ref[...] = jnp.dot(a, w_ref[...].astype(BF16), preferred_element_type=F32) + b_ref[...]


def _modulation(cond, w_ada, b_ada):
    n_layers, d, n = w_ada.shape
    rows = cond.shape[0]
    tn = 1024
    return pl.pallas_call(
        _mod_kernel,
        out_shape=jax.ShapeDtypeStruct((n_layers, rows, n), F32),
        grid=(n_layers, n // tn),
        in_specs=[
            pl.BlockSpec((rows, d), lambda l, j: (0, 0)),
            pl.BlockSpec((None, d, tn), lambda l, j: (l, 0, j)),
            pl.BlockSpec((None, 1, tn), lambda l, j: (l, 0, j)),
        ],
        out_specs=pl.BlockSpec((None, rows, tn), lambda l, j: (l, 0, j)),
        compiler_params=_params("parallel", "parallel"),
        name="adaln_mod",
    )(cond, w_ada, b_ada.reshape(n_layers, 1, n))


def _qkv_kernel(*refs, d, nq, nk, qk_norm, rope):
    it = iter(refs)
    x_ref, mod_ref, g_ref, w_ref = next(it), next(it), next(it), next(it)
    gq_ref = gk_ref = cos_ref = sin_ref = None
    if qk_norm:
        gq_ref, gk_ref = next(it), next(it)
    if rope:
        cos_ref, sin_ref = next(it), next(it)
    q_ref, k_ref, v_ref = next(it), next(it), next(it)

    mod = mod_ref[...]
    y = _rms(x_ref[...]) * g_ref[0:1, :]
    h = (y * (1.0 + mod[:, d:2 * d]) + mod[:, 0:d]).astype(BF16)
    tm = h.shape[0]

    if rope:
        cos = cos_ref[...]
        sin = sin_ref[...]
        lane = lax.broadcasted_iota(jnp.int32, (tm, LANES), 1)
        upper = (lane & 32) != 0

    def finish(z, gain_ref):
        if gain_ref is not None:
            z = _rms(z) * gain_ref[...]
        if rope:
            partner = jnp.where(upper, pltpu.roll(z, 32, 1), pltpu.roll(z, LANES - 32, 1))
            z = z * cos + partner * sin
        return z

    chunk = 512
    for c0 in range(0, nq + nk, chunk):
        acc = jnp.dot(h, w_ref[:, c0:c0 + chunk], preferred_element_type=F32)
        for j in range(chunk // LANES):
            col = c0 + j * LANES
            z = acc[:, j * LANES:(j + 1) * LANES]
            if col < nq:
                q_ref[:, col:col + LANES] = finish(z, gq_ref).astype(q_ref.dtype)
            else:
                k_ref[:, col - nq:col - nq + LANES] = finish(z, gk_ref).astype(k_ref.dtype)
    v_ref[...] = jnp.dot(h, w_ref[:, nq + nk:], preferred_element_type=F32).astype(v_ref.dtype)


def _qkv(x, mod, g_norm, w_qkv, layer, slot, row_of_tile, *, tm, nq, nk, kv_dtype,
         gains=None, rope_tables=None):
    t, d = x.shape
    n_all = w_qkv.shape[-1]
    nv = n_all - nq - nk
    operands = [x, mod, g_norm, w_qkv]
    in_specs = [
        pl.BlockSpec((tm, d), lambda i: (i, 0)),
        pl.BlockSpec((None, None, 1, mod.shape[-1]), lambda i: (layer, row_of_tile(i), 0, 0)),
        pl.BlockSpec((None, 4, d), lambda i: (layer, 0, 0)),
        pl.BlockSpec((None, d, n_all), lambda i: (slot, 0, 0), pipeline_mode=pl.Buffered(1)),
    ]
    if gains is not None:
        for g in gains:
            operands.append(g.reshape(g.shape[0], 1, LANES))
            in_specs.append(pl.BlockSpec((None, 1, LANES), lambda i: (slot, 0, 0)))
    if rope_tables is not None:
        tiles_per_seq = rope_tables[0].shape[0] // tm
        for tab in rope_tables:
            operands.append(tab)
            in_specs.append(pl.BlockSpec((tm, LANES), lambda i: (i % tiles_per_seq, 0)))
    kern = functools.partial(_qkv_kernel, d=d, nq=nq, nk=nk, qk_norm=gains is not None,
                             rope=rope_tables is not None)
    return pl.pallas_call(
        kern,
        out_shape=(jax.ShapeDtypeStruct((t, nq), BF16),
                   jax.ShapeDtypeStruct((t, nk), kv_dtype),
                   jax.ShapeDtypeStruct((t, nv), kv_dtype)),
        grid=(t // tm,),
        in_specs=in_specs,
        out_specs=(pl.BlockSpec((tm, nq), lambda i: (i, 0)),
                   pl.BlockSpec((tm, nk), lambda i: (i, 0)),
                   pl.BlockSpec((tm, nv), lambda i: (i, 0))),
        compiler_params=_params("parallel"),
        name="qkv_proj",
    )(*operands)


def _stack_heads(q, n, stride, offset):
    return jnp.concatenate(
        [q[:, (g * stride + offset) * LANES:(g * stride + offset + 1) * LANES] for g in range(n)], axis=0)


def _gqa_attn_kernel(*refs, tq, n_new, n_ctx, band, has_sink):
    it = iter(refs)
    sink_ref = next(it) if has_sink else None
    q_ref, k_ref, v_ref = next(it), next(it), next(it)
    kc_ref = vc_ref = None
    if n_ctx:
        kc_ref, vc_ref = next(it), next(it)
    o_ref = next(it)

    scale = HEAD_DIM ** -0.5
    qs = _stack_heads(q_ref[...], GROUP, 1, 0)
    rows = GROUP * tq

    band_width = tq + 2 * WINDOW
    if band and n_new > band_width:
        q0 = pl.program_id(2) * tq
        start = pl.multiple_of(jnp.clip(q0 - WINDOW, 0, n_new - band_width), LANES)
        k = k_ref[pl.ds(start, band_width), :]
        v = v_ref[pl.ds(start, band_width), :]
    else:
        q0 = pl.program_id(2) * tq
        start = 0
        k = k_ref[...]
        v = v_ref[...]
    k = k.astype(BF16)
    v = v.astype(BF16)

    s = lax.dot_general(qs, k, NT_DIMS, preferred_element_type=F32) * scale
    if band:
        nk = k.shape[0]
        qpos = q0 + lax.broadcasted_iota(jnp.int32, (tq, nk), 0)
        kpos = start + lax.broadcasted_iota(jnp.int32, (tq, nk), 1)
        near = jnp.abs(qpos - kpos) <= WINDOW
        s = jnp.where(jnp.concatenate([near] * GROUP, axis=0), s, NEG_INF)
    m = jnp.max(s, axis=-1, keepdims=True)
    if n_ctx:
        kc = kc_ref[...].astype(BF16)
        vc = vc_ref[...].astype(BF16)
        sc = lax.dot_general(qs, kc, NT_DIMS, preferred_element_type=F32) * scale
        m = jnp.maximum(m, jnp.max(sc, axis=-1, keepdims=True))
    if has_sink:
        head0 = pl.program_id(1) * GROUP
        sk = jnp.concatenate(
            [jnp.full((tq, 1), sink_ref[head0 + g], dtype=F32) for g in range(GROUP)], axis=0)
        m = jnp.maximum(m, sk)
    p = jnp.exp(s - m)
    denom = jnp.sum(p, axis=-1, keepdims=True)
    acc = jnp.dot(p.astype(BF16), v, preferred_element_type=F32)
    if n_ctx:
        pc = jnp.exp(sc - m)
        denom = denom + jnp.sum(pc, axis=-1, keepdims=True)
        acc = acc + jnp.dot(pc.astype(BF16), vc, preferred_element_type=F32)
    if has_sink:
        denom = denom + jnp.exp(sk - m)
    o = acc * (1.0 / denom)
    for g in range(GROUP):
        o_ref[:, g * LANES:(g + 1) * LANES] = o[g * tq:(g + 1) * tq, :].astype(o_ref.dtype)
    del rows


def _gqa_attention(q, k, v, cache, slot, sink, *, batch, seq, tq, band):
    nq_tiles = seq // tq
    n_ctx = 0 if cache is None else cache[0].shape[2]
    operands, in_specs = [], []
    if sink is not None:
        operands.append(sink)
        in_specs.append(pl.BlockSpec(memory_space=pltpu.SMEM))
    operands += [q, k, v]
    in_specs += [
        pl.BlockSpec((tq, GROUP * HEAD_DIM), lambda b, h, i: (b * nq_tiles + i, h)),
        pl.BlockSpec((seq, HEAD_DIM), lambda b, h, i: (b, h)),
        pl.BlockSpec((seq, HEAD_DIM), lambda b, h, i: (b, h)),
    ]
    if cache is not None:
        operands += list(cache)
        in_specs += [pl.BlockSpec((None, None, n_ctx, HEAD_DIM), lambda b, h, i: (b, slot, 0, h))] * 2
    kern = functools.partial(_gqa_attn_kernel, tq=tq, n_new=seq, n_ctx=n_ctx, band=band,
                             has_sink=sink is not None)
    return pl.pallas_call(
        kern,
        out_shape=jax.ShapeDtypeStruct(q.shape, BF16),
        grid=(batch, KV_HEADS, nq_tiles),
        in_specs=in_specs,
        out_specs=pl.BlockSpec((tq, GROUP * HEAD_DIM), lambda b, h, i: (b * nq_tiles + i, h)),
        compiler_params=_params("parallel", "parallel", "parallel"),
        name="gqa_attn",
    )(*operands)


def _diff_attn_kernel(*refs, tq, n_ctx, lam_init):
    it = iter(refs)
    lq1_ref, lk1_ref, lq2_ref, lk2_ref, gs_ref = next(it), next(it), next(it), next(it), next(it)
    q_ref, k_ref, v_ref = next(it), next(it), next(it)
    kc_ref = vc_ref = None
    if n_ctx:
        kc_ref, vc_ref = next(it), next(it)
    o_ref = next(it)

    lam = (jnp.exp(jnp.sum(lq1_ref[...] * lk1_ref[...], axis=-1, keepdims=True))
           - jnp.exp(jnp.sum(lq2_ref[...] * lk2_ref[...], axis=-1, keepdims=True)) + lam_init)
    scale = DIFF_DIM ** -0.5
    q = q_ref[...]
    k = k_ref[...].astype(BF16)
    v = v_ref[...].astype(BF16)
    if n_ctx:
        kc = kc_ref[...].astype(BF16)
        vc = vc_ref[...].astype(BF16)

    e_new, e_ctx, recip = [], [], []
    for half in range(2):
        qs = _stack_heads(q, DIFF_GROUP, 2, half)
        s = lax.dot_general(qs, k[:, half * DIFF_DIM:(half + 1) * DIFF_DIM], NT_DIMS,
                            preferred_element_type=F32) * scale
        m = jnp.max(s, axis=-1, keepdims=True)
        if n_ctx:
            sc = lax.dot_general(qs, kc[:, half * DIFF_DIM:(half + 1) * DIFF_DIM], NT_DIMS,
                                 preferred_element_type=F32) * scale
            m = jnp.maximum(m, jnp.max(sc, axis=-1, keepdims=True))
        e = jnp.exp(s - m)
        denom = jnp.sum(e, axis=-1, keepdims=True)
        e_new.append(e)
        if n_ctx:
            ec = jnp.exp(sc - m)
            denom = denom + jnp.sum(ec, axis=-1, keepdims=True)
            e_ctx.append(ec)
        recip.append(1.0 / denom)
    r0 = recip[0]
    r1 = lam * recip[1]
    w = e_new[0] * r0 - e_new[1] * r1
    acc = jnp.dot(w.astype(BF16), v, preferred_element_type=F32)
    if n_ctx:
        wc = e_ctx[0] * r0 - e_ctx[1] * r1
        acc = acc + jnp.dot(wc.astype(BF16), vc, preferred_element_type=F32)
    y = (_rms(acc) * gs_ref[...]) * (1.0 - lam_init)
    width = 2 * DIFF_DIM
    for g in range(DIFF_GROUP):
        o_ref[:, g * width:(g + 1) * width] = y[g * tq:(g + 1) * tq, :].astype(o_ref.dtype)


def _diff_attention(q, k, v, cache, slot, lam_vecs, g_subln, *, batch, seq, tq, lam_init):
    nq_tiles = seq // tq
    n_ctx = 0 if cache is None else cache[0].shape[2]
    qw = DIFF_GROUP * 2 * DIFF_DIM
    kw = 2 * DIFF_DIM
    operands = [a.reshape(a.shape[0], 1, DIFF_DIM) for a in lam_vecs]
    in_specs = [pl.BlockSpec((None, 1, DIFF_DIM), lambda b, h, i: (slot, 0, 0))] * 4
    operands.append(g_subln.reshape(g_subln.shape[0], 1, kw))
    in_specs.append(pl.BlockSpec((None, 1, kw), lambda b, h, i: (slot, 0, 0)))
    operands += [q, k, v]
    in_specs += [
        pl.BlockSpec((tq, qw), lambda b, h, i: (b * nq_tiles + i, h)),
        pl.BlockSpec((seq, kw), lambda b, h, i: (b, h)),
        pl.BlockSpec((seq, kw), lambda b, h, i: (b, h)),
    ]
    if cache is not None:
        operands += list(cache)
        in_specs += [pl.BlockSpec((None, None, n_ctx, kw), lambda b, h, i: (b, slot, 0, h))] * 2
    kern = functools.partial(_diff_attn_kernel, tq=tq, n_ctx=n_ctx, lam_init=lam_init)
    return pl.pallas_call(
        kern,
        out_shape=jax.ShapeDtypeStruct(q.shape, BF16),
        grid=(batch, DIFF_KV_HEADS, nq_tiles),
        in_specs=in_specs,
        out_specs=pl.BlockSpec((tq, qw), lambda b, h, i: (b * nq_tiles + i, h)),
        compiler_params=_params("parallel", "parallel", "parallel"),
        name="diff_attn",
    )(*operands)


def _oproj_kernel(o_ref, w_ref, x_ref, mod_ref, g_ref, out_ref, *, d):
    m = jnp.dot(o_ref[...], w_ref[...], preferred_element_type=F32)
    gate = mod_ref[...][:, 2 * d:3 * d]
    out_ref[...] = x_ref[...] + gate * (_rms(m) * g_ref[1:2, :])


def _oproj(o, w_o, x, mod, g_norm, layer, slot, row_of_tile, *, tm):
    t, d = x.shape
    nin = o.shape[1]
    return pl.pallas_call(
        functools.partial(_oproj_kernel, d=d),
        out_shape=jax.ShapeDtypeStruct((t, d), F32),
        grid=(t // tm,),
        in_specs=[
            pl.BlockSpec((tm, nin), lambda i: (i, 0)),
            pl.BlockSpec((None, nin, d), lambda i: (slot, 0, 0), pipeline_mode=pl.Buffered(1)),
            pl.BlockSpec((tm, d), lambda i: (i, 0)),
            pl.BlockSpec((None, None, 1, mod.shape[-1]), lambda i: (layer, row_of_tile(i), 0, 0)),
            pl.BlockSpec((None, 4, d), lambda i: (layer, 0, 0)),
        ],
        out_specs=pl.BlockSpec((tm, d), lambda i: (i, 0)),
        compiler_params=_params("parallel"),
        name="out_proj",
    )(o, w_o, x, mod, g_norm)


def _ffn_kernel(x_ref, mod_ref, g_ref, wg_ref, wu_ref, wd_ref, out_ref, h_scr, acc_scr, *, d):
    j = pl.program_id(1)

    @pl.when(j == 0)
    def _():
        mod = mod_ref[...]
        y = _rms(x_ref[...]) * g_ref[2:3, :]
        h_scr[...] = (y * (1.0 + mod[:, 4 * d:5 * d]) + mod[:, 3 * d:4 * d]).astype(BF16)

    h = h_scr[...]
    a = jnp.dot(h, wg_ref[...], preferred_element_type=F32)
    b = jnp.dot(h, wu_ref[...], preferred_element_type=F32)
    u = ((a * jax.nn.sigmoid(a)) * b).astype(BF16)
    part = jnp.dot(u, wd_ref[...], preferred_element_type=F32)

    @pl.when(j == 0)
    def _():
        acc_scr[...] = part

    @pl.when(j > 0)
    def _():
        acc_scr[...] += part

    @pl.when(j == pl.num_programs(1) - 1)
    def _():
        gate = mod_ref[...][:, 5 * d:6 * d]
        out_ref[...] = x_ref[...] + gate * (_rms(acc_scr[...]) * g_ref[3:4, :])


def _ffn(x, mod, g_norm, w_gate, w_up, w_down, layer, row_of_tile, *, tm, fc):
    t, d = x.shape
    d_ff = w_gate.shape[-1]
    return pl.pallas_call(
        functools.partial(_ffn_kernel, d=d),
        out_shape=jax.ShapeDtypeStruct((t, d), F32),
        grid=(t // tm, d_ff // fc),
        in_specs=[
            pl.BlockSpec((tm, d), lambda i, j: (i, 0)),
            pl.BlockSpec((None, None, 1, mod.shape[-1]), lambda i, j: (layer, row_of_tile(i), 0, 0)),
            pl.BlockSpec((None, 4, d), lambda i, j: (layer, 0, 0)),
            pl.BlockSpec((None, d, fc), lambda i, j: (layer, 0, j)),
            pl.BlockSpec((None, d, fc), lambda i, j: (layer, 0, j)),
            pl.BlockSpec((None, fc, d), lambda i, j: (layer, j, 0)),
        ],
        out_specs=pl.BlockSpec((tm, d), lambda i, j: (i, 0)),
        scratch_shapes=[pltpu.VMEM((tm, d), BF16), pltpu.VMEM((tm, d), F32)],
        compiler_params=_params("parallel", "arbitrary"),
        name="ffn",
    )(x, mod, g_norm, w_gate, w_up, w_down)


def _rope_tables(n_tokens):
    nf = HEAD_DIM // 4
    t = jnp.arange(n_tokens, dtype=jnp.int32)
    rows = (t // GRID_W).astype(F32)
    cols = (t % GRID_W).astype(F32)
    inv = 1.0 / (ROPE_THETA ** (jnp.arange(nf, dtype=F32) / nf))
    ar = rows[:, None] * inv
    ac = cols[:, None] * inv
    cos = jnp.concatenate([jnp.cos(ar), jnp.cos(ar), jnp.cos(ac), jnp.cos(ac)], axis=-1)
    sin = jnp.concatenate([-jnp.sin(ar), jnp.sin(ar), -jnp.sin(ac), jnp.sin(ac)], axis=-1)
    return cos, sin


def _diff_lambda_init(layer):
    return 0.8 - 0.6 * math.exp(-0.3 * layer)


def _largest_tile(n, cap):
    t = cap
    while n % t:
        t //= 2
    return t


def _run_path(x3, first_row, rows_per_batch, caches, mod, p, depth):
    batch, seq, d = x3.shape
    latent = caches is not None
    x = x3.reshape(batch * seq, d)
    tm = _largest_tile(seq if latent else batch * seq, 512)
    tiles_per_batch = seq // tm if latent else None

    if latent:
        def row_of_tile(i):
            return first_row + (i // tiles_per_batch) * rows_per_batch
    else:
        def row_of_tile(i):
            return first_row

    rope_tables = _rope_tables(seq) if latent else None
    kv_dtype = BF16 if latent else F32
    kept = ([], [], [], [], [], [])
    for layer in range(depth):
        kind, slot = layer % N_MIXERS, layer // N_MIXERS
        cache = None if not latent else (caches[2 * kind], caches[2 * kind + 1])
        common = dict(tm=tm, kv_dtype=kv_dtype, rope_tables=rope_tables)
        if kind == 0:
            q, k, v = _qkv(x, mod, p['g_norm'], p['w_qkv_full'], layer, slot, row_of_tile,
                           nq=N_HEADS * HEAD_DIM, nk=KV_HEADS * HEAD_DIM,
                           gains=(p['g_q_full'], p['g_k_full']), **common)
            o = _gqa_attention(q, k, v, cache, slot, None, batch=batch, seq=seq,
                               tq=_largest_tile(seq, 256), band=False)
            w_o = p['w_o_full']
        elif kind == 1:
            q, k, v = _qkv(x, mod, p['g_norm'], p['w_qkv_win'], layer, slot, row_of_tile,
                           nq=N_HEADS * HEAD_DIM, nk=KV_HEADS * HEAD_DIM, **common)
            o = _gqa_attention(q, k, v, cache, slot, p['sink_win'][slot], batch=batch, seq=seq,
                               tq=_largest_tile(seq, 256), band=latent)
            w_o = p['w_o_win']
        else:
            q, k, v = _qkv(x, mod, p['g_norm'], p['w_qkv_diff'], layer, slot, row_of_tile,
                           nq=DIFF_HEADS * 2 * DIFF_DIM, nk=DIFF_KV_HEADS * 2 * DIFF_DIM, **common)
            o = _diff_attention(q, k, v, cache, slot,
                                (p['lam_q1'], p['lam_k1'], p['lam_q2'], p['lam_k2']), p['g_subln_diff'],
                                batch=batch, seq=seq, tq=_largest_tile(seq, 128),
                                lam_init=_diff_lambda_init(layer))
            w_o = p['w_o_diff']
        if not latent:
            kept[2 * kind].append(k)
            kept[2 * kind + 1].append(v)
        x = _oproj(o, w_o, x, mod, p['g_norm'], layer, slot, row_of_tile, tm=tm)
        x = _ffn(x, mod, p['g_norm'], p['w_gate'], p['w_up'], p['w_down'], layer, row_of_tile,
                 tm=tm, fc=512)
    return x.reshape(batch, seq, d), kept


def kernel(x_prompt, x_sample, c, cache_k_full, cache_v_full, cache_k_win, cache_v_win, cache_k_diff,
           cache_v_diff, c_ctx, w_ada, b_ada, g_norm, w_qkv_full, w_o_full, g_q_full, g_k_full, w_qkv_win,
           w_o_win, sink_win, w_qkv_diff, w_o_diff, lam_q1, lam_k1, lam_q2, lam_k2, g_subln_diff, w_gate,
           w_up, w_down):
    depth, d = w_ada.shape[0], w_ada.shape[1]
    batch, seq, _ = x_prompt.shape
    dec_batch, dec_seq, _ = x_sample.shape

    p = {
        'g_norm': g_norm,
        'w_qkv_full': w_qkv_full.astype(BF16), 'w_o_full': w_o_full.astype(BF16),
        'g_q_full': g_q_full, 'g_k_full': g_k_full,
        'w_qkv_win': w_qkv_win.astype(BF16), 'w_o_win': w_o_win.astype(BF16), 'sink_win': sink_win,
        'w_qkv_diff': w_qkv_diff.astype(BF16), 'w_o_diff': w_o_diff.astype(BF16),
        'lam_q1': lam_q1, 'lam_k1': lam_k1, 'lam_q2': lam_q2, 'lam_k2': lam_k2,
        'g_subln_diff': g_subln_diff,
        'w_gate': w_gate.astype(BF16), 'w_up': w_up.astype(BF16), 'w_down': w_down.astype(BF16),
    }

    n_rows = -(-(1 + dec_batch) // SUBLANES) * SUBLANES
    cond = jnp.zeros((n_rows, d), F32).at[0].set(c_ctx).at[1:1 + dec_batch].set(c)
    mod = _modulation(cond, w_ada, b_ada)
    mod = mod.reshape(depth, n_rows, 1, 6 * d)

    y_prompt, kept = _run_path(x_prompt, 0, 0, None, mod, p, depth)

    def flat_cache(a):
        return a.reshape(a.shape[0], a.shape[1], a.shape[2], -1)

    caches = tuple(flat_cache(a) for a in (cache_k_full, cache_v_full, cache_k_win, cache_v_win,
                                           cache_k_diff, cache_v_diff))
    y_sample, _ = _run_path(x_sample, 1, 1, caches, mod, p, depth)

    def stack(parts, tail):
        return jnp.stack([a.reshape((batch, seq) + tail) for a in parts], axis=1)

    return (y_prompt, y_sample,
            stack(kept[0], (KV_HEADS, HEAD_DIM)), stack(kept[1], (KV_HEADS, HEAD_DIM)),
            stack(kept[2], (KV_HEADS, HEAD_DIM)), stack(kept[3], (KV_HEADS, HEAD_DIM)),
            stack(kept[4], (DIFF_KV_HEADS, 2, DIFF_DIM)), stack(kept[5], (DIFF_KV_HEADS, 2 * DIFF_DIM)))
```

```python
import functools
import math

import jax
import jax.numpy as jnp
from jax import lax
from jax.experimental import pallas as pl
from jax.experimental.pallas import tpu as pltpu

F32 = jnp.float32
BF16 = jnp.bfloat16

N_MIXERS = 3
N_HEADS = 16
KV_HEADS = 4
HEAD_DIM = 128
GROUP = N_HEADS // KV_HEADS
WINDOW = 128
GRID_W = 64
DIFF_HEADS = 8
DIFF_KV_HEADS = 2
DIFF_GROUP = DIFF_HEADS // DIFF_KV_HEADS
DIFF_DIM = 128
ROPE_THETA = 10000.0
EPS = 1e-6
NEG_INF = -1e30

LANES = 128
SUBLANES = 8
V7X_VMEM_BYTES = 64 * 1024 * 1024
VMEM_LIMIT = V7X_VMEM_BYTES - 8 * 1024 * 1024

NT_DIMS = (((1,), (1,)), ((), ()))
LOG2E = math.log2(math.e)
ROW_CHUNK = 2 * SUBLANES
HEADS_PER_BLOCK = 2


def _params(*semantics):
    return pltpu.CompilerParams(dimension_semantics=semantics, vmem_limit_bytes=VMEM_LIMIT)


def _rms(x):
    return x * lax.rsqrt(jnp.mean(x * x, axis=-1, keepdims=True) + EPS)


def _mod_kernel(c_ref, w_ref, b_ref, o_ref):
    c = c_ref[...]
    a = (c * jax.nn.sigmoid(c)).astype(BF16)
    o_ref[...] = jnp.dot(a, w_ref[...].astype(BF16), preferred_element_type=F32) + b_ref[...]


def _modulation(cond, w_ada, b_ada):
    n_layers, d, n = w_ada.shape
    rows = cond.shape[0]
    tn = 1024
    return pl.pallas_call(
        _mod_kernel,
        out_shape=jax.ShapeDtypeStruct((n_layers, rows, n), F32),
        grid=(n_layers, n // tn),
        in_specs=[
            pl.BlockSpec((rows, d), lambda l, j: (0, 0)),
            pl.BlockSpec((None, d, tn), lambda l, j: (l, 0, j)),
            pl.BlockSpec((None, 1, tn), lambda l, j: (l, 0, j)),
        ],
        out_specs=pl.BlockSpec((None, rows, tn), lambda l, j: (l, 0, j)),
        compiler_params=_params("parallel", "parallel"),
        name="adaln_mod",
    )(cond, w_ada, b_ada.reshape(n_layers, 1, n))


def _qkv_kernel(*refs, d, nq, nk, qk_norm, rope, q_scale):
    it = iter(refs)
    x_ref, mod_ref, g_ref, w_ref = next(it), next(it), next(it), next(it)
    gq_ref = gk_ref = cos_ref = sin_ref = None
    if qk_norm:
        gq_ref, gk_ref = next(it), next(it)
    if rope:
        cos_ref, sin_ref = next(it), next(it)
    q_ref, k_ref, v_ref = next(it), next(it), next(it)

    mod = mod_ref[...]
    y = _rms(x_ref[...]) * g_ref[0:1, :]
    h = (y * (1.0 + mod[:, d:2 * d]) + mod[:, 0:d]).astype(BF16)
    tm = h.shape[0]

    if rope:
        cos = cos_ref[...]
        sin = sin_ref[...]
        lane = lax.broadcasted_iota(jnp.int32, (tm, LANES), 1)
        upper = (lane & 32) != 0

    def finish(z, gain_ref):
        if gain_ref is not None:
            z = _rms(z) * gain_ref[...]
        if rope:
            partner = jnp.where(upper, pltpu.roll(z, 32, 1), pltpu.roll(z, LANES - 32, 1))
            z = z * cos + partner * sin
        return z

    chunk = 512
    for c0 in range(0, nq + nk, chunk):
        acc = jnp.dot(h, w_ref[:, c0:c0 + chunk], preferred_element_type=F32)
        for j in range(chunk // LANES):
            col = c0 + j * LANES
            z = acc[:, j * LANES:(j + 1) * LANES]
            if col < nq:
                q_ref[:, col:col + LANES] = (finish(z, gq_ref) * q_scale).astype(q_ref.dtype)
            else:
                k_ref[:, col - nq:col - nq + LANES] = finish(z, gk_ref).astype(k_ref.dtype)
    v_ref[...] = jnp.dot(h, w_ref[:, nq + nk:], preferred_element_type=F32).astype(v_ref.dtype)


def _qkv(x, mod, g_norm, w_qkv, layer, slot, row_of_tile, *, tm, nq, nk, kv_dtype,
         gains=None, rope_tables=None):
    t, d = x.shape
    n_all = w_qkv.shape[-1]
    nv = n_all - nq - nk
    operands = [x, mod, g_norm, w_qkv]
    in_specs = [
        pl.BlockSpec((tm, d), lambda i: (i, 0)),
        pl.BlockSpec((None, None, 1, mod.shape[-1]), lambda i: (layer, row_of_tile(i), 0, 0)),
        pl.BlockSpec((None, 4, d), lambda i: (layer, 0, 0)),
        pl.BlockSpec((None, d, n_all), lambda i: (slot, 0, 0), pipeline_mode=pl.Buffered(1)),
    ]
    if gains is not None:
        for g in gains:
            operands.append(g.reshape(g.shape[0], 1, LANES))
            in_specs.append(pl.BlockSpec((None, 1, LANES), lambda i: (slot, 0, 0)))
    if rope_tables is not None:
        tiles_per_seq = rope_tables[0].shape[0] // tm
        for tab in rope_tables:
            operands.append(tab)
            in_specs.append(pl.BlockSpec((tm, LANES), lambda i: (i % tiles_per_seq, 0)))
    kern = functools.partial(_qkv_kernel, d=d, nq=nq, nk=nk, qk_norm=gains is not None,
                             rope=rope_tables is not None, q_scale=HEAD_DIM ** -0.5 * LOG2E)
    return pl.pallas_call(
        kern,
        out_shape=(jax.ShapeDtypeStruct((t, nq), BF16),
                   jax.ShapeDtypeStruct((t, nk), kv_dtype),
                   jax.ShapeDtypeStruct((t, nv), kv_dtype)),
        grid=(t // tm,),
        in_specs=in_specs,
        out_specs=(pl.BlockSpec((tm, nq), lambda i: (i, 0)),
                   pl.BlockSpec((tm, nk), lambda i: (i, 0)),
                   pl.BlockSpec((tm, nv), lambda i: (i, 0))),
        compiler_params=_params("parallel"),
        name="qkv_proj",
    )(*operands)


def _gqa_attn_kernel(*refs, tq, n_new, n_ctx, band, has_sink):
    it = iter(refs)
    sink_ref = next(it) if has_sink else None
    q_ref, k_ref, v_ref = next(it), next(it), next(it)
    kc_ref = vc_ref = None
    if n_ctx:
        kc_ref, vc_ref = next(it), next(it)
    o_ref = next(it)
    s_scr, p_scr = next(it), next(it)
    bias_scr = next(it) if band else None

    band_width = tq + 2 * WINDOW
    q0 = pl.program_id(2) * tq
    if band and n_new > band_width:
        start = pl.multiple_of(jnp.clip(q0 - WINDOW, 0, n_new - band_width), LANES)
        k = k_ref[pl.ds(start, band_width), :]
        v = v_ref[pl.ds(start, band_width), :]
    else:
        start = 0
        k = k_ref[...]
        v = v_ref[...]
    k = k.astype(BF16)
    v = v.astype(BF16)
    nk = k.shape[0]
    if n_ctx:
        k = jnp.concatenate([k, kc_ref[...].astype(BF16)], axis=0)
        v = jnp.concatenate([v, vc_ref[...].astype(BF16)], axis=0)
    if band:
        qpos = q0 + lax.broadcasted_iota(jnp.int32, (tq, nk), 0)
        kpos = start + lax.broadcasted_iota(jnp.int32, (tq, nk), 1)
        bias_scr[...] = jnp.where(jnp.abs(qpos - kpos) <= WINDOW, 0.0, NEG_INF).astype(F32)

    rows = HEADS_PER_BLOCK * tq
    for blk in range(GROUP // HEADS_PER_BLOCK):
        base = blk * rows
        heads = range(blk * HEADS_PER_BLOCK, (blk + 1) * HEADS_PER_BLOCK)
        qb = jnp.concatenate([q_ref[:, h * LANES:(h + 1) * LANES] for h in heads], axis=0)
        s_scr[base:base + rows, :] = lax.dot_general(qb, k, NT_DIMS, preferred_element_type=F32)
        recips = []
        for r in range(0, rows, ROW_CHUNK):
            rs = slice(base + r, base + r + ROW_CHUNK)
            if band:
                qr = r % tq
                parts = [s_scr[rs, 0:nk] + bias_scr[qr:qr + ROW_CHUNK, :]]
                if n_ctx:
                    parts.append(s_scr[rs, nk:])
            else:
                parts = [s_scr[rs, :]]
            m = functools.reduce(jnp.maximum, [jnp.max(a, axis=-1, keepdims=True) for a in parts])
            if has_sink:
                sk = sink_ref[pl.program_id(1) * GROUP + heads[r // tq]] * LOG2E
                m = jnp.maximum(m, sk)
            es = [jnp.exp2(a - m) for a in parts]
            denom = functools.reduce(jnp.add, [jnp.sum(e, axis=-1, keepdims=True) for e in es])
            if has_sink:
                denom = denom + jnp.exp2(sk - m)
            col = 0
            for e in es:
                p_scr[rs, col:col + e.shape[1]] = e.astype(BF16)
                col += e.shape[1]
            recips.append(1.0 / denom)
        acc = jnp.dot(p_scr[base:base + rows, :], v, preferred_element_type=F32)
        o = acc * jnp.concatenate(recips, axis=0)
        for j, h in enumerate(heads):
            o_ref[:, h * LANES:(h + 1) * LANES] = o[j * tq:(j + 1) * tq, :].astype(o_ref.dtype)


def _gqa_attention(q, k, v, cache, slot, sink, *, batch, seq, tq, band):
    nq_tiles = seq // tq
    n_ctx = 0 if cache is None else cache[0].shape[2]
    operands, in_specs = [], []
    if sink is not None:
        operands.append(sink)
        in_specs.append(pl.BlockSpec(memory_space=pltpu.SMEM))
    operands += [q, k, v]
    in_specs += [
        pl.BlockSpec((tq, GROUP * HEAD_DIM), lambda b, h, i: (b * nq_tiles + i, h)),
        pl.BlockSpec((seq, HEAD_DIM), lambda b, h, i: (b, h)),
        pl.BlockSpec((seq, HEAD_DIM), lambda b, h, i: (b, h)),
    ]
    if cache is not None:
        operands += list(cache)
        in_specs += [pl.BlockSpec((None, None, n_ctx, HEAD_DIM), lambda b, h, i: (b, slot, 0, h))] * 2
    kern = functools.partial(_gqa_attn_kernel, tq=tq, n_new=seq, n_ctx=n_ctx, band=band,
                             has_sink=sink is not None)
    n_new_keys = min(seq, tq + 2 * WINDOW) if band else seq
    n_keys = n_new_keys + n_ctx
    scratch = [pltpu.VMEM((GROUP * tq, n_keys), F32), pltpu.VMEM((GROUP * tq, n_keys), BF16)]
    if band:
        scratch.append(pltpu.VMEM((tq, n_new_keys), F32))
    return pl.pallas_call(
        kern,
        out_shape=jax.ShapeDtypeStruct(q.shape, BF16),
        grid=(batch, KV_HEADS, nq_tiles),
        in_specs=in_specs,
        out_specs=pl.BlockSpec((tq, GROUP * HEAD_DIM), lambda b, h, i: (b * nq_tiles + i, h)),
        scratch_shapes=scratch,
        compiler_params=_params("parallel", "parallel", "parallel"),
        name="gqa_attn",
    )(*operands)


def _diff_attn_kernel(*refs, tq, n_ctx, lam_init):
    it = iter(refs)
    lq1_ref, lk1_ref, lq2_ref, lk2_ref, gs_ref = next(it), next(it), next(it), next(it), next(it)
    q_ref, k_ref, v_ref = next(it), next(it), next(it)
    kc_ref = vc_ref = None
    if n_ctx:
        kc_ref, vc_ref = next(it), next(it)
    o_ref = next(it)
    s_scrs = (next(it), next(it))
    p_scr = next(it)

    lam = (jnp.exp(jnp.sum(lq1_ref[...] * lk1_ref[...], axis=-1, keepdims=True))
           - jnp.exp(jnp.sum(lq2_ref[...] * lk2_ref[...], axis=-1, keepdims=True)) + lam_init)
    k = k_ref[...].astype(BF16)
    v = v_ref[...].astype(BF16)
    if n_ctx:
        k = jnp.concatenate([k, kc_ref[...].astype(BF16)], axis=0)
        v = jnp.concatenate([v, vc_ref[...].astype(BF16)], axis=0)

    width = 2 * DIFF_DIM
    rows = HEADS_PER_BLOCK * tq
    for blk in range(DIFF_GROUP // HEADS_PER_BLOCK):
        base = blk * rows
        heads = range(blk * HEADS_PER_BLOCK, (blk + 1) * HEADS_PER_BLOCK)
        for half in range(2):
            qb = jnp.concatenate(
                [q_ref[:, (2 * g + half) * DIFF_DIM:(2 * g + half + 1) * DIFF_DIM] for g in heads], axis=0)
            s_scrs[half][base:base + rows, :] = lax.dot_general(
                qb, k[:, half * DIFF_DIM:(half + 1) * DIFF_DIM], NT_DIMS, preferred_element_type=F32)
        for r in range(0, rows, ROW_CHUNK):
            rs = slice(base + r, base + r + ROW_CHUNK)
            s0 = s_scrs[0][rs, :]
            e0 = jnp.exp2(s0 - jnp.max(s0, axis=-1, keepdims=True))
            r0 = 1.0 / jnp.sum(e0, axis=-1, keepdims=True)
            s1 = s_scrs[1][rs, :]
            e1 = jnp.exp2(s1 - jnp.max(s1, axis=-1, keepdims=True))
            r1 = lam / jnp.sum(e1, axis=-1, keepdims=True)
            p_scr[rs, :] = (e0 * r0 - e1 * r1).astype(BF16)
        acc = jnp.dot(p_scr[base:base + rows, :], v, preferred_element_type=F32)
        y = (_rms(acc) * gs_ref[...]) * (1.0 - lam_init)
        for j, g in enumerate(heads):
            o_ref[:, g * width:(g + 1) * width] = y[j * tq:(j + 1) * tq, :].astype(o_ref.dtype)


def _diff_attention(q, k, v, cache, slot, lam_vecs, g_subln, *, batch, seq, tq, lam_init):
    nq_tiles = seq // tq
    n_ctx = 0 if cache is None else cache[0].shape[2]
    qw = DIFF_GROUP * 2 * DIFF_DIM
    kw = 2 * DIFF_DIM
    operands = [a.reshape(a.shape[0], 1, DIFF_DIM) for a in lam_vecs]
    in_specs = [pl.BlockSpec((None, 1, DIFF_DIM), lambda b, h, i: (slot, 0, 0))] * 4
    operands.append(g_subln.reshape(g_subln.shape[0], 1, kw))
    in_specs.append(pl.BlockSpec((None, 1, kw), lambda b, h, i: (slot, 0, 0)))
    operands += [q, k, v]
    in_specs += [
        pl.BlockSpec((tq, qw), lambda b, h, i: (b * nq_tiles + i, h)),
        pl.BlockSpec((seq, kw), lambda b, h, i: (b, h)),
        pl.BlockSpec((seq, kw), lambda b, h, i: (b, h)),
    ]
    if cache is not None:
        operands += list(cache)
        in_specs += [pl.BlockSpec((None, None, n_ctx, kw), lambda b, h, i: (b, slot, 0, h))] * 2
    kern = functools.partial(_diff_attn_kernel, tq=tq, n_ctx=n_ctx, lam_init=lam_init)
    return pl.pallas_call(
        kern,
        out_shape=jax.ShapeDtypeStruct(q.shape, BF16),
        grid=(batch, DIFF_KV_HEADS, nq_tiles),
        in_specs=in_specs,
        out_specs=pl.BlockSpec((tq, qw), lambda b, h, i: (b * nq_tiles + i, h)),
        scratch_shapes=[pltpu.VMEM((DIFF_GROUP * tq, seq + n_ctx), F32),
                        pltpu.VMEM((DIFF_GROUP * tq, seq + n_ctx), F32),
                        pltpu.VMEM((DIFF_GROUP * tq, seq + n_ctx), BF16)],
        compiler_params=_params("parallel", "parallel", "parallel"),
        name="diff_attn",
    )(*operands)


def _oproj_kernel(o_ref, w_ref, x_ref, mod_ref, g_ref, out_ref, *, d):
    m = jnp.dot(o_ref[...], w_ref[...], preferred_element_type=F32)
    gate = mod_ref[...][:, 2 * d:3 * d]
    out_ref[...] = x_ref[...] + gate * (_rms(m) * g_ref[1:2, :])


def _oproj(o, w_o, x, mod, g_norm, layer, slot, row_of_tile, *, tm):
    t, d = x.shape
    nin = o.shape[1]
    return pl.pallas_call(
        functools.partial(_oproj_kernel, d=d),
        out_shape=jax.ShapeDtypeStruct((t, d), F32),
        grid=(t // tm,),
        in_specs=[
            pl.BlockSpec((tm, nin), lambda i: (i, 0)),
            pl.BlockSpec((None, nin, d), lambda i: (slot, 0, 0), pipeline_mode=pl.Buffered(1)),
            pl.BlockSpec((tm, d), lambda i: (i, 0)),
            pl.BlockSpec((None, None, 1, mod.shape[-1]), lambda i: (layer, row_of_tile(i), 0, 0)),
            pl.BlockSpec((None, 4, d), lambda i: (layer, 0, 0)),
        ],
        out_specs=pl.BlockSpec((tm, d), lambda i: (i, 0)),
        compiler_params=_params("parallel"),
        name="out_proj",
    )(o, w_o, x, mod, g_norm)


def _ffn_kernel(x_ref, mod_ref, g_ref, wg_ref, wu_ref, wd_ref, out_ref, h_scr, acc_scr, *, d):
    j = pl.program_id(1)

    @pl.when(j == 0)
    def _():
        mod = mod_ref[...]
        y = _rms(x_ref[...]) * g_ref[2:3, :]
        h_scr[...] = (y * (1.0 + mod[:, 4 * d:5 * d]) + mod[:, 3 * d:4 * d]).astype(BF16)
        acc_scr[...] = jnp.zeros_like(acc_scr)

    h = h_scr[...]
    a = jnp.dot(h, wg_ref[...], preferred_element_type=F32)
    b = jnp.dot(h, wu_ref[...], preferred_element_type=F32)
    u = ((a * jax.nn.sigmoid(a)) * b).astype(BF16)
    acc_scr[...] += jnp.dot(u, wd_ref[...], preferred_element_type=F32)

    @pl.when(j == pl.num_programs(1) - 1)
    def _():
        gate = mod_ref[...][:, 5 * d:6 * d]
        out_ref[...] = x_ref[...] + gate * (_rms(acc_scr[...]) * g_ref[3:4, :])


def _ffn(x, mod, g_norm, w_gate, w_up, w_down, layer, row_of_tile, *, tm, fc):
    t, d = x.shape
    d_ff = w_gate.shape[-1]
    return pl.pallas_call(
        functools.partial(_ffn_kernel, d=d),
        out_shape=jax.ShapeDtypeStruct((t, d), F32),
        grid=(t // tm, d_ff // fc),
        in_specs=[
            pl.BlockSpec((tm, d), lambda i, j: (i, 0)),
            pl.BlockSpec((None, None, 1, mod.shape[-1]), lambda i, j: (layer, row_of_tile(i), 0, 0)),
            pl.BlockSpec((None, 4, d), lambda i, j: (layer, 0, 0)),
            pl.BlockSpec((None, d, fc), lambda i, j: (layer, 0, j)),
            pl.BlockSpec((None, d, fc), lambda i, j: (layer, 0, j)),
            pl.BlockSpec((None, fc, d), lambda i, j: (layer, j, 0)),
        ],
        out_specs=pl.BlockSpec((tm, d), lambda i, j: (i, 0)),
        scratch_shapes=[pltpu.VMEM((tm, d), BF16), pltpu.VMEM((tm, d), F32)],
        compiler_params=_params("parallel", "arbitrary"),
        name="ffn",
    )(x, mod, g_norm, w_gate, w_up, w_down)


def _rope_tables(n_tokens):
    nf = HEAD_DIM // 4
    t = jnp.arange(n_tokens, dtype=jnp.int32)
    rows = (t // GRID_W).astype(F32)
    cols = (t % GRID_W).astype(F32)
    inv = 1.0 / (ROPE_THETA ** (jnp.arange(nf, dtype=F32) / nf))
    ar = rows[:, None] * inv
    ac = cols[:, None] * inv
    cos = jnp.concatenate([jnp.cos(ar), jnp.cos(ar), jnp.cos(ac), jnp.cos(ac)], axis=-1)
    sin = jnp.concatenate([-jnp.sin(ar), jnp.sin(ar), -jnp.sin(ac), jnp.sin(ac)], axis=-1)
    return cos, sin


def _diff_lambda_init(layer):
    return 0.8 - 0.6 * math.exp(-0.3 * layer)


def _largest_tile(n, cap):
    t = cap
    while n % t:
        t //= 2
    return t


def _run_path(x3, first_row, rows_per_batch, caches, mod, p, depth):
    batch, seq, d = x3.shape
    latent = caches is not None
    x = x3.reshape(batch * seq, d)
    tm = _largest_tile(seq if latent else batch * seq, 512)
    tiles_per_batch = seq // tm if latent else None

    if latent:
        def row_of_tile(i):
            return first_row + (i // tiles_per_batch) * rows_per_batch
    else:
        def row_of_tile(i):
            return first_row

    rope_tables = _rope_tables(seq) if latent else None
    kv_dtype = BF16 if latent else F32
    kept = ([], [], [], [], [], [])
    for layer in range(depth):
        kind, slot = layer % N_MIXERS, layer // N_MIXERS
        cache = None if not latent else (caches[2 * kind], caches[2 * kind + 1])
        common = dict(tm=tm, kv_dtype=kv_dtype, rope_tables=rope_tables)
        if kind == 0:
            q, k, v = _qkv(x, mod, p['g_norm'], p['w_qkv_full'], layer, slot, row_of_tile,
                           nq=N_HEADS * HEAD_DIM, nk=KV_HEADS * HEAD_DIM,
                           gains=(p['g_q_full'], p['g_k_full']), **common)
            o = _gqa_attention(q, k, v, cache, slot, None, batch=batch, seq=seq,
                               tq=_largest_tile(seq, 256), band=False)
            w_o = p['w_o_full']
        elif kind == 1:
            q, k, v = _qkv(x, mod, p['g_norm'], p['w_qkv_win'], layer, slot, row_of_tile,
                           nq=N_HEADS * HEAD_DIM, nk=KV_HEADS * HEAD_DIM, **common)
            o = _gqa_attention(q, k, v, cache, slot, p['sink_win'][slot], batch=batch, seq=seq,
                               tq=_largest_tile(seq, 256), band=latent)
            w_o = p['w_o_win']
        else:
            q, k, v = _qkv(x, mod, p['g_norm'], p['w_qkv_diff'], layer, slot, row_of_tile,
                           nq=DIFF_HEADS * 2 * DIFF_DIM, nk=DIFF_KV_HEADS * 2 * DIFF_DIM, **common)
            o = _diff_attention(q, k, v, cache, slot,
                                (p['lam_q1'], p['lam_k1'], p['lam_q2'], p['lam_k2']), p['g_subln_diff'],
                                batch=batch, seq=seq, tq=_largest_tile(seq, 256),
                                lam_init=_diff_lambda_init(layer))
            w_o = p['w_o_diff']
        if not latent:
            kept[2 * kind].append(k)
            kept[2 * kind + 1].append(v)
        x = _oproj(o, w_o, x, mod, p['g_norm'], layer, slot, row_of_tile, tm=tm)
        x = _ffn(x, mod, p['g_norm'], p['w_gate'], p['w_up'], p['w_down'], layer, row_of_tile,
                 tm=tm, fc=512)
    return x.reshape(batch, seq, d), kept


def kernel(x_prompt, x_sample, c, cache_k_full, cache_v_full, cache_k_win, cache_v_win, cache_k_diff,
           cache_v_diff, c_ctx, w_ada, b_ada, g_norm, w_qkv_full, w_o_full, g_q_full, g_k_full, w_qkv_win,
           w_o_win, sink_win, w_qkv_diff, w_o_diff, lam_q1, lam_k1, lam_q2, lam_k2, g_subln_diff, w_gate,
           w_up, w_down):
    depth, d = w_ada.shape[0], w_ada.shape[1]
    batch, seq, _ = x_prompt.shape
    dec_batch, dec_seq, _ = x_sample.shape

    p = {
        'g_norm': g_norm,
        'w_qkv_full': w_qkv_full.astype(BF16), 'w_o_full': w_o_full.astype(BF16),
        'g_q_full': g_q_full, 'g_k_full': g_k_full,
        'w_qkv_win': w_qkv_win.astype(BF16), 'w_o_win': w_o_win.astype(BF16), 'sink_win': sink_win,
        'w_qkv_diff': w_qkv_diff.astype(BF16), 'w_o_diff': w_o_diff.astype(BF16),
        'lam_q1': lam_q1, 'lam_k1': lam_k1, 'lam_q2': lam_q2, 'lam_k2': lam_k2,
        'g_subln_diff': g_subln_diff,
        'w_gate': w_gate.astype(BF16), 'w_up': w_up.astype(BF16), 'w_down': w_down.astype(BF16),
    }

    n_rows = -(-(1 + dec_batch) // SUBLANES) * SUBLANES
    cond = jnp.zeros((n_rows, d), F32).at[0].set(c_ctx).at[1:1 + dec_batch].set(c)
    mod = _modulation(cond, w_ada, b_ada)
    mod = mod.reshape(depth, n_rows, 1, 6 * d)

    y_prompt, kept = _run_path(x_prompt, 0, 0, None, mod, p, depth)

    def flat_cache(a):
        return a.reshape(a.shape[0], a.shape[1], a.shape[2], -1)

    caches = tuple(flat_cache(a) for a in (cache_k_full, cache_v_full, cache_k_win, cache_v_win,
                                           cache_k_diff, cache_v_diff))
    y_sample, _ = _run_path(x_sample, 1, 1, caches, mod, p, depth)

    def stack(parts, tail):
        return jnp.stack([a.reshape((batch, seq) + tail) for a in parts], axis=1)

    return (y_prompt, y_sample,
            stack(kept[0], (KV_HEADS, HEAD_DIM)), stack(kept[1], (KV_HEADS, HEAD_DIM)),
            stack(kept[2], (KV_HEADS, HEAD_DIM)), stack(kept[3], (KV_HEADS, HEAD_DIM)),
            stack(kept[4], (DIFF_KV_HEADS, 2, DIFF_DIM)), stack(kept[5], (DIFF_KV_HEADS, 2 * DIFF_DIM)))
```

```python
import functools
import math

import jax
import jax.numpy as jnp
from jax import lax
from jax.experimental import pallas as pl
from jax.experimental.pallas import tpu as pltpu

F32 = jnp.float32
BF16 = jnp.bfloat16

N_MIXERS = 3
N_HEADS = 16
KV_HEADS = 4
HEAD_DIM = 128
GROUP = N_HEADS // KV_HEADS
WINDOW = 128
GRID_W = 64
DIFF_HEADS = 8
DIFF_KV_HEADS = 2
DIFF_GROUP = DIFF_HEADS // DIFF_KV_HEADS
DIFF_DIM = 128
ROPE_THETA = 10000.0
EPS = 1e-6
NEG_INF = -1e30

LANES = 128
SUBLANES = 8
V7X_VMEM_BYTES = 64 * 1024 * 1024
VMEM_LIMIT = V7X_VMEM_BYTES - 8 * 1024 * 1024

NT_DIMS = (((1,), (1,)), ((), ()))
LOG2E = math.log2(math.e)
ROW_CHUNK = 2 * SUBLANES
ATTN_BLOCK_ROWS = 256


def _params(*semantics):
    return pltpu.CompilerParams(dimension_semantics=semantics, vmem_limit_bytes=VMEM_LIMIT)


def _rms(x):
    return x * lax.rsqrt(jnp.mean(x * x, axis=-1, keepdims=True) + EPS)


def _mod_kernel(c_ref, w_ref, b_ref, o_ref):
    c = c_ref[...]
    a = (c * jax.nn.sigmoid(c)).astype(BF16)
    o_ref[...] = jnp.dot(a, w_ref[...].astype(BF16), preferred_element_type=F32) + b_ref[...]


def _modulation(cond, w_ada, b_ada):
    n_layers, d, n = w_ada.shape
    rows = cond.shape[0]
    tn = 1024
    return pl.pallas_call(
        _mod_kernel,
        out_shape=jax.ShapeDtypeStruct((n_layers, rows, n), F32),
        grid=(n_layers, n // tn),
        in_specs=[
            pl.BlockSpec((rows, d), lambda l, j: (0, 0)),
            pl.BlockSpec((None, d, tn), lambda l, j: (l, 0, j)),
            pl.BlockSpec((None, 1, tn), lambda l, j: (l, 0, j)),
        ],
        out_specs=pl.BlockSpec((None, rows, tn), lambda l, j: (l, 0, j)),
        compiler_params=_params("parallel", "parallel"),
        name="adaln_mod",
    )(cond, w_ada, b_ada.reshape(n_layers, 1, n))


def _qkv_kernel(*refs, d, nq, nk, qk_norm, rope, q_scale):
    it = iter(refs)
    x_ref, mod_ref, g_ref, w_ref = next(it), next(it), next(it), next(it)
    gq_ref = gk_ref = cos_ref = sin_ref = None
    if qk_norm:
        gq_ref, gk_ref = next(it), next(it)
    if rope:
        cos_ref, sin_ref = next(it), next(it)
    q_ref, k_ref, v_ref = next(it), next(it), next(it)

    mod = mod_ref[...]
    y = _rms(x_ref[...]) * g_ref[0:1, :]
    h = (y * (1.0 + mod[:, d:2 * d]) + mod[:, 0:d]).astype(BF16)
    tm = h.shape[0]

    if rope:
        cos = cos_ref[...]
        sin = sin_ref[...]
        lane = lax.broadcasted_iota(jnp.int32, (tm, LANES), 1)
        upper = (lane & 32) != 0

    def finish(z, gain_ref):
        if gain_ref is not None:
            z = _rms(z) * gain_ref[...]
        if rope:
            partner = jnp.where(upper, pltpu.roll(z, 32, 1), pltpu.roll(z, LANES - 32, 1))
            z = z * cos + partner * sin
        return z

    chunk = 512
    for c0 in range(0, nq + nk, chunk):
        acc = jnp.dot(h, w_ref[:, c0:c0 + chunk], preferred_element_type=F32)
        for j in range(chunk // LANES):
            col = c0 + j * LANES
            z = acc[:, j * LANES:(j + 1) * LANES]
            if col < nq:
                q_ref[:, col:col + LANES] = (finish(z, gq_ref) * q_scale).astype(q_ref.dtype)
            else:
                k_ref[:, col - nq:col - nq + LANES] = finish(z, gk_ref).astype(k_ref.dtype)
    v_ref[...] = jnp.dot(h, w_ref[:, nq + nk:], preferred_element_type=F32).astype(v_ref.dtype)


def _qkv(x, mod, g_norm, w_qkv, layer, slot, row_of_tile, *, tm, nq, nk, kv_dtype,
         gains=None, rope_tables=None):
    t, d = x.shape
    n_all = w_qkv.shape[-1]
    nv = n_all - nq - nk
    operands = [x, mod, g_norm, w_qkv]
    in_specs = [
        pl.BlockSpec((tm, d), lambda i: (i, 0)),
        pl.BlockSpec((None, None, 1, mod.shape[-1]), lambda i: (layer, row_of_tile(i), 0, 0)),
        pl.BlockSpec((None, 4, d), lambda i: (layer, 0, 0)),
        pl.BlockSpec((None, d, n_all), lambda i: (slot, 0, 0), pipeline_mode=pl.Buffered(1)),
    ]
    if gains is not None:
        for g in gains:
            operands.append(g.reshape(g.shape[0], 1, LANES))
            in_specs.append(pl.BlockSpec((None, 1, LANES), lambda i: (slot, 0, 0)))
    if rope_tables is not None:
        tiles_per_seq = rope_tables[0].shape[0] // tm
        for tab in rope_tables:
            operands.append(tab)
            in_specs.append(pl.BlockSpec((tm, LANES), lambda i: (i % tiles_per_seq, 0)))
    kern = functools.partial(_qkv_kernel, d=d, nq=nq, nk=nk, qk_norm=gains is not None,
                             rope=rope_tables is not None, q_scale=HEAD_DIM ** -0.5 * LOG2E)
    return pl.pallas_call(
        kern,
        out_shape=(jax.ShapeDtypeStruct((t, nq), BF16),
                   jax.ShapeDtypeStruct((t, nk), kv_dtype),
                   jax.ShapeDtypeStruct((t, nv), kv_dtype)),
        grid=(t // tm,),
        in_specs=in_specs,
        out_specs=(pl.BlockSpec((tm, nq), lambda i: (i, 0)),
                   pl.BlockSpec((tm, nk), lambda i: (i, 0)),
                   pl.BlockSpec((tm, nv), lambda i: (i, 0))),
        compiler_params=_params("parallel"),
        name="qkv_proj",
    )(*operands)


def _gqa_attn_kernel(*refs, tq, block_rows, n_new, n_ctx, band, has_sink):
    it = iter(refs)
    sink_ref = next(it) if has_sink else None
    q_ref, k_ref, v_ref = next(it), next(it), next(it)
    kc_ref = vc_ref = None
    if n_ctx:
        kc_ref, vc_ref = next(it), next(it)
    o_ref = next(it)
    s_scr, p_scr = next(it), next(it)
    bias_scr = next(it) if band else None

    band_width = tq + 2 * WINDOW
    q0 = pl.program_id(2) * tq
    if band and n_new > band_width:
        start = pl.multiple_of(jnp.clip(q0 - WINDOW, 0, n_new - band_width), LANES)
        k = k_ref[pl.ds(start, band_width), :]
        v = v_ref[pl.ds(start, band_width), :]
    else:
        start = 0
        k = k_ref[...]
        v = v_ref[...]
    k = k.astype(BF16)
    v = v.astype(BF16)
    nk = k.shape[0]
    if n_ctx:
        k = jnp.concatenate([k, kc_ref[...].astype(BF16)], axis=0)
        v = jnp.concatenate([v, vc_ref[...].astype(BF16)], axis=0)
    if band:
        qpos = q0 + lax.broadcasted_iota(jnp.int32, (tq, nk), 0)
        kpos = start + lax.broadcasted_iota(jnp.int32, (tq, nk), 1)
        bias_scr[...] = jnp.where(jnp.abs(qpos - kpos) <= WINDOW, 0.0, NEG_INF).astype(F32)

    rows = block_rows
    for blk in range(GROUP * tq // rows):
        base = blk * rows
        head, row0 = divmod(base, tq)
        qb = q_ref[row0:row0 + rows, head * LANES:(head + 1) * LANES]
        s_scr[base:base + rows, :] = lax.dot_general(qb, k, NT_DIMS, preferred_element_type=F32)
        recips = []
        for r in range(0, rows, ROW_CHUNK):
            rs = slice(base + r, base + r + ROW_CHUNK)
            if band:
                qr = row0 + r
                parts = [s_scr[rs, 0:nk] + bias_scr[qr:qr + ROW_CHUNK, :]]
                if n_ctx:
                    parts.append(s_scr[rs, nk:])
            else:
                parts = [s_scr[rs, :]]
            m = functools.reduce(jnp.maximum, [jnp.max(a, axis=-1, keepdims=True) for a in parts])
            if has_sink:
                sk = sink_ref[pl.program_id(1) * GROUP + head] * LOG2E
                m = jnp.maximum(m, sk)
            es = [jnp.exp2(a - m) for a in parts]
            denom = functools.reduce(jnp.add, [jnp.sum(e, axis=-1, keepdims=True) for e in es])
            if has_sink:
                denom = denom + jnp.exp2(sk - m)
            col = 0
            for e in es:
                p_scr[rs, col:col + e.shape[1]] = e.astype(BF16)
                col += e.shape[1]
            recips.append(1.0 / denom)
        acc = jnp.dot(p_scr[base:base + rows, :], v, preferred_element_type=F32)
        o = acc * jnp.concatenate(recips, axis=0)
        o_ref[row0:row0 + rows, head * LANES:(head + 1) * LANES] = o.astype(o_ref.dtype)


def _gqa_attention(q, k, v, cache, slot, sink, *, batch, seq, tq, band):
    nq_tiles = seq // tq
    n_ctx = 0 if cache is None else cache[0].shape[2]
    operands, in_specs = [], []
    if sink is not None:
        operands.append(sink)
        in_specs.append(pl.BlockSpec(memory_space=pltpu.SMEM))
    operands += [q, k, v]
    in_specs += [
        pl.BlockSpec((tq, GROUP * HEAD_DIM), lambda b, h, i: (b * nq_tiles + i, h)),
        pl.BlockSpec((seq, HEAD_DIM), lambda b, h, i: (b, h)),
        pl.BlockSpec((seq, HEAD_DIM), lambda b, h, i: (b, h)),
    ]
    if cache is not None:
        operands += list(cache)
        in_specs += [pl.BlockSpec((None, None, n_ctx, HEAD_DIM), lambda b, h, i: (b, slot, 0, h))] * 2
    kern = functools.partial(_gqa_attn_kernel, tq=tq, block_rows=min(tq, ATTN_BLOCK_ROWS), n_new=seq,
                             n_ctx=n_ctx, band=band, has_sink=sink is not None)
    n_new_keys = min(seq, tq + 2 * WINDOW) if band else seq
    n_keys = n_new_keys + n_ctx
    scratch = [pltpu.VMEM((GROUP * tq, n_keys), F32), pltpu.VMEM((GROUP * tq, n_keys), BF16)]
    if band:
        scratch.append(pltpu.VMEM((tq, n_new_keys), F32))
    return pl.pallas_call(
        kern,
        out_shape=jax.ShapeDtypeStruct(q.shape, BF16),
        grid=(batch, KV_HEADS, nq_tiles),
        in_specs=in_specs,
        out_specs=pl.BlockSpec((tq, GROUP * HEAD_DIM), lambda b, h, i: (b * nq_tiles + i, h)),
        scratch_shapes=scratch,
        compiler_params=_params("parallel", "parallel", "parallel"),
        name="gqa_attn",
    )(*operands)


def _diff_attn_kernel(*refs, tq, block_heads, n_ctx, lam_init):
    it = iter(refs)
    lq1_ref, lk1_ref, lq2_ref, lk2_ref, gs_ref = next(it), next(it), next(it), next(it), next(it)
    q_ref, k_ref, v_ref = next(it), next(it), next(it)
    kc_ref = vc_ref = None
    if n_ctx:
        kc_ref, vc_ref = next(it), next(it)
    o_ref = next(it)
    s_scrs = (next(it), next(it))
    p_scr = next(it)

    lam = (jnp.exp(jnp.sum(lq1_ref[...] * lk1_ref[...], axis=-1, keepdims=True))
           - jnp.exp(jnp.sum(lq2_ref[...] * lk2_ref[...], axis=-1, keepdims=True)) + lam_init)
    k = k_ref[...].astype(BF16)
    v = v_ref[...].astype(BF16)
    if n_ctx:
        k = jnp.concatenate([k, kc_ref[...].astype(BF16)], axis=0)
        v = jnp.concatenate([v, vc_ref[...].astype(BF16)], axis=0)

    width = 2 * DIFF_DIM
    rows = block_heads * tq
    for blk in range(DIFF_GROUP // block_heads):
        base = blk * rows
        heads = range(blk * block_heads, (blk + 1) * block_heads)
        for half in range(2):
            qb = jnp.concatenate(
                [q_ref[:, (2 * g + half) * DIFF_DIM:(2 * g + half + 1) * DIFF_DIM] for g in heads], axis=0)
            s_scrs[half][base:base + rows, :] = lax.dot_general(
                qb, k[:, half * DIFF_DIM:(half + 1) * DIFF_DIM], NT_DIMS, preferred_element_type=F32)
        recips = []
        for r in range(0, rows, ROW_CHUNK):
            rs = slice(base + r, base + r + ROW_CHUNK)
            s0 = s_scrs[0][rs, :]
            e0 = jnp.exp2(s0 - jnp.max(s0, axis=-1, keepdims=True))
            l0 = jnp.sum(e0, axis=-1, keepdims=True)
            s1 = s_scrs[1][rs, :]
            e1 = jnp.exp2(s1 - jnp.max(s1, axis=-1, keepdims=True))
            ratio = lam * l0 / jnp.sum(e1, axis=-1, keepdims=True)
            p_scr[rs, :] = (e0 - e1 * ratio).astype(BF16)
            recips.append(1.0 / l0)
        acc = jnp.dot(p_scr[base:base + rows, :], v, preferred_element_type=F32)
        acc = acc * jnp.concatenate(recips, axis=0)
        y = (_rms(acc) * gs_ref[...]) * (1.0 - lam_init)
        for j, g in enumerate(heads):
            o_ref[:, g * width:(g + 1) * width] = y[j * tq:(j + 1) * tq, :].astype(o_ref.dtype)


def _diff_attention(q, k, v, cache, slot, lam_vecs, g_subln, *, batch, seq, tq, lam_init):
    nq_tiles = seq // tq
    n_ctx = 0 if cache is None else cache[0].shape[2]
    qw = DIFF_GROUP * 2 * DIFF_DIM
    kw = 2 * DIFF_DIM
    operands = [a.reshape(a.shape[0], 1, DIFF_DIM) for a in lam_vecs]
    in_specs = [pl.BlockSpec((None, 1, DIFF_DIM), lambda b, h, i: (slot, 0, 0))] * 4
    operands.append(g_subln.reshape(g_subln.shape[0], 1, kw))
    in_specs.append(pl.BlockSpec((None, 1, kw), lambda b, h, i: (slot, 0, 0)))
    operands += [q, k, v]
    in_specs += [
        pl.BlockSpec((tq, qw), lambda b, h, i: (b * nq_tiles + i, h)),
        pl.BlockSpec((seq, kw), lambda b, h, i: (b, h)),
        pl.BlockSpec((seq, kw), lambda b, h, i: (b, h)),
    ]
    if cache is not None:
        operands += list(cache)
        in_specs += [pl.BlockSpec((None, None, n_ctx, kw), lambda b, h, i: (b, slot, 0, h))] * 2
    kern = functools.partial(_diff_attn_kernel, tq=tq, block_heads=2, n_ctx=n_ctx, lam_init=lam_init)
    return pl.pallas_call(
        kern,
        out_shape=jax.ShapeDtypeStruct(q.shape, BF16),
        grid=(batch, DIFF_KV_HEADS, nq_tiles),
        in_specs=in_specs,
        out_specs=pl.BlockSpec((tq, qw), lambda b, h, i: (b * nq_tiles + i, h)),
        scratch_shapes=[pltpu.VMEM((DIFF_GROUP * tq, seq + n_ctx), F32),
                        pltpu.VMEM((DIFF_GROUP * tq, seq + n_ctx), F32),
                        pltpu.VMEM((DIFF_GROUP * tq, seq + n_ctx), BF16)],
        compiler_params=_params("parallel", "parallel", "parallel"),
        name="diff_attn",
    )(*operands)


def _oproj_kernel(o_ref, w_ref, x_ref, mod_ref, g_ref, out_ref, h_ref, *, d, row_blocks):
    mod = mod_ref[...]
    gate1, shift2, scale2 = mod[:, 2 * d:3 * d], mod[:, 3 * d:4 * d], mod[:, 4 * d:5 * d]
    rows = o_ref.shape[0] // row_blocks
    for r in range(row_blocks):
        rs = slice(r * rows, (r + 1) * rows)
        m = jnp.dot(o_ref[rs, :], w_ref[...], preferred_element_type=F32)
        x_new = x_ref[rs, :] + gate1 * (_rms(m) * g_ref[1:2, :])
        out_ref[rs, :] = x_new
        h_ref[rs, :] = ((_rms(x_new) * g_ref[2:3, :]) * (1.0 + scale2) + shift2).astype(BF16)


def _oproj(o, w_o, x, mod, g_norm, layer, slot, row_of_tile, *, tm):
    t, d = x.shape
    nin = o.shape[1]
    return pl.pallas_call(
        functools.partial(_oproj_kernel, d=d, row_blocks=4 if tm % (4 * ROW_CHUNK) == 0 else 1),
        out_shape=(jax.ShapeDtypeStruct((t, d), F32), jax.ShapeDtypeStruct((t, d), BF16)),
        grid=(t // tm,),
        in_specs=[
            pl.BlockSpec((tm, nin), lambda i: (i, 0)),
            pl.BlockSpec((None, nin, d), lambda i: (slot, 0, 0), pipeline_mode=pl.Buffered(1)),
            pl.BlockSpec((tm, d), lambda i: (i, 0)),
            pl.BlockSpec((None, None, 1, mod.shape[-1]), lambda i: (layer, row_of_tile(i), 0, 0)),
            pl.BlockSpec((None, 4, d), lambda i: (layer, 0, 0)),
        ],
        out_specs=(pl.BlockSpec((tm, d), lambda i: (i, 0)), pl.BlockSpec((tm, d), lambda i: (i, 0))),
        compiler_params=_params("parallel"),
        name="out_proj",
    )(o, w_o, x, mod, g_norm)


def _ffn_kernel(h_ref, x_ref, mod_ref, g_ref, wg_ref, wu_ref, wd_ref, out_ref, *, d):
    j = pl.program_id(1)

    @pl.when(j == 0)
    def _():
        out_ref[...] = jnp.zeros_like(out_ref)

    h = h_ref[...]
    a = jnp.dot(h, wg_ref[...], preferred_element_type=F32)
    b = jnp.dot(h, wu_ref[...], preferred_element_type=F32)
    u = ((a * jax.nn.sigmoid(a)) * b).astype(BF16)
    out_ref[...] += jnp.dot(u, wd_ref[...], preferred_element_type=F32)

    @pl.when(j == pl.num_programs(1) - 1)
    def _():
        gate = mod_ref[...][:, 5 * d:6 * d]
        out_ref[...] = x_ref[...] + gate * (_rms(out_ref[...]) * g_ref[3:4, :])


def _ffn(h, x, mod, g_norm, w_gate, w_up, w_down, layer, row_of_tile, *, tm, fc):
    t, d = x.shape
    d_ff = w_gate.shape[-1]
    return pl.pallas_call(
        functools.partial(_ffn_kernel, d=d),
        out_shape=jax.ShapeDtypeStruct((t, d), F32),
        grid=(t // tm, d_ff // fc),
        in_specs=[
            pl.BlockSpec((tm, d), lambda i, j: (i, 0)),
            pl.BlockSpec((tm, d), lambda i, j: (i, 0)),
            pl.BlockSpec((None, None, 1, mod.shape[-1]), lambda i, j: (layer, row_of_tile(i), 0, 0)),
            pl.BlockSpec((None, 4, d), lambda i, j: (layer, 0, 0)),
            pl.BlockSpec((None, d, fc), lambda i, j: (layer, 0, j)),
            pl.BlockSpec((None, d, fc), lambda i, j: (layer, 0, j)),
            pl.BlockSpec((None, fc, d), lambda i, j: (layer, j, 0)),
        ],
        out_specs=pl.BlockSpec((tm, d), lambda i, j: (i, 0)),
        compiler_params=_params("parallel", "arbitrary"),
        name="ffn",
    )(h, x, mod, g_norm, w_gate, w_up, w_down)


def _rope_tables(n_tokens):
    nf = HEAD_DIM // 4
    t = jnp.arange(n_tokens, dtype=jnp.int32)
    rows = (t // GRID_W).astype(F32)
    cols = (t % GRID_W).astype(F32)
    inv = 1.0 / (ROPE_THETA ** (jnp.arange(nf, dtype=F32) / nf))
    ar = rows[:, None] * inv
    ac = cols[:, None] * inv
    cos = jnp.concatenate([jnp.cos(ar), jnp.cos(ar), jnp.cos(ac), jnp.cos(ac)], axis=-1)
    sin = jnp.concatenate([-jnp.sin(ar), jnp.sin(ar), -jnp.sin(ac), jnp.sin(ac)], axis=-1)
    return cos, sin


def _diff_lambda_init(layer):
    return 0.8 - 0.6 * math.exp(-0.3 * layer)


def _largest_tile(n, cap):
    t = cap
    while n % t:
        t //= 2
    return t


def _run_path(x3, first_row, rows_per_batch, caches, mod, p, depth):
    batch, seq, d = x3.shape
    latent = caches is not None
    x = x3.reshape(batch * seq, d)
    tm = _largest_tile(seq if latent else batch * seq, 512)
    tiles_per_batch = seq // tm if latent else None

    if latent:
        def row_of_tile(i):
            return first_row + (i // tiles_per_batch) * rows_per_batch
    else:
        def row_of_tile(i):
            return first_row

    rope_tables = _rope_tables(seq) if latent else None
    kv_dtype = BF16 if latent else F32
    kept = ([], [], [], [], [], [])
    for layer in range(depth):
        kind, slot = layer % N_MIXERS, layer // N_MIXERS
        cache = None if not latent else (caches[2 * kind], caches[2 * kind + 1])
        common = dict(tm=tm, kv_dtype=kv_dtype, rope_tables=rope_tables)
        if kind == 0:
            q, k, v = _qkv(x, mod, p['g_norm'], p['w_qkv_full'], layer, slot, row_of_tile,
                           nq=N_HEADS * HEAD_DIM, nk=KV_HEADS * HEAD_DIM,
                           gains=(p['g_q_full'], p['g_k_full']), **common)
            o = _gqa_attention(q, k, v, cache, slot, None, batch=batch, seq=seq,
                               tq=_largest_tile(seq, 512), band=False)
            w_o = p['w_o_full']
        elif kind == 1:
            q, k, v = _qkv(x, mod, p['g_norm'], p['w_qkv_win'], layer, slot, row_of_tile,
                           nq=N_HEADS * HEAD_DIM, nk=KV_HEADS * HEAD_DIM, **common)
            o = _gqa_attention(q, k, v, cache, slot, p['sink_win'][slot], batch=batch, seq=seq,
                               tq=_largest_tile(seq, 256), band=latent)
            w_o = p['w_o_win']
        else:
            q, k, v = _qkv(x, mod, p['g_norm'], p['w_qkv_diff'], layer, slot, row_of_tile,
                           nq=DIFF_HEADS * 2 * DIFF_DIM, nk=DIFF_KV_HEADS * 2 * DIFF_DIM, **common)
            o = _diff_attention(q, k, v, cache, slot,
                                (p['lam_q1'], p['lam_k1'], p['lam_q2'], p['lam_k2']), p['g_subln_diff'],
                                batch=batch, seq=seq, tq=_largest_tile(seq, 256),
                                lam_init=_diff_lambda_init(layer))
            w_o = p['w_o_diff']
        if not latent:
            kept[2 * kind].append(k)
            kept[2 * kind + 1].append(v)
        x, h = _oproj(o, w_o, x, mod, p['g_norm'], layer, slot, row_of_tile, tm=tm)
        x = _ffn(h, x, mod, p['g_norm'], p['w_gate'], p['w_up'], p['w_down'], layer, row_of_tile,
                 tm=tm, fc=512)
    return x.reshape(batch, seq, d), kept


def kernel(x_prompt, x_sample, c, cache_k_full, cache_v_full, cache_k_win, cache_v_win, cache_k_diff,
           cache_v_diff, c_ctx, w_ada, b_ada, g_norm, w_qkv_full, w_o_full, g_q_full, g_k_full, w_qkv_win,
           w_o_win, sink_win, w_qkv_diff, w_o_diff, lam_q1, lam_k1, lam_q2, lam_k2, g_subln_diff, w_gate,
           w_up, w_down):
    depth, d = w_ada.shape[0], w_ada.shape[1]
    batch, seq, _ = x_prompt.shape
    dec_batch, dec_seq, _ = x_sample.shape

    p = {
        'g_norm': g_norm,
        'w_qkv_full': w_qkv_full.astype(BF16), 'w_o_full': w_o_full.astype(BF16),
        'g_q_full': g_q_full, 'g_k_full': g_k_full,
        'w_qkv_win': w_qkv_win.astype(BF16), 'w_o_win': w_o_win.astype(BF16), 'sink_win': sink_win,
        'w_qkv_diff': w_qkv_diff.astype(BF16), 'w_o_diff': w_o_diff.astype(BF16),
        'lam_q1': lam_q1, 'lam_k1': lam_k1, 'lam_q2': lam_q2, 'lam_k2': lam_k2,
        'g_subln_diff': g_subln_diff,
        'w_gate': w_gate.astype(BF16), 'w_up': w_up.astype(BF16), 'w_down': w_down.astype(BF16),
    }

    n_rows = -(-(1 + dec_batch) // SUBLANES) * SUBLANES
    cond = jnp.zeros((n_rows, d), F32).at[0].set(c_ctx).at[1:1 + dec_batch].set(c)
    mod = _modulation(cond, w_ada, b_ada)
    mod = mod.reshape(depth, n_rows, 1, 6 * d)

    y_prompt, kept = _run_path(x_prompt, 0, 0, None, mod, p, depth)

    def flat_cache(a):
        return a.reshape(a.shape[0], a.shape[1], a.shape[2], -1)

    caches = tuple(flat_cache(a) for a in (cache_k_full, cache_v_full, cache_k_win, cache_v_win,
                                           cache_k_diff, cache_v_diff))
    y_sample, _ = _run_path(x_sample, 1, 1, caches, mod, p, depth)

    def stack(parts, tail):
        return jnp.stack([a.reshape((batch, seq) + tail) for a in parts], axis=1)

    return (y_prompt, y_sample,
            stack(kept[0], (KV_HEADS, HEAD_DIM)), stack(kept[1], (KV_HEADS, HEAD_DIM)),
            stack(kept[2], (KV_HEADS, HEAD_DIM)), stack(kept[3], (KV_HEADS, HEAD_DIM)),
            stack(kept[4], (DIFF_KV_HEADS, 2, DIFF_DIM)), stack(kept[5], (DIFF_KV_HEADS, 2 * DIFF_DIM)))
```

```python
import functools
import math

import jax
import jax.numpy as jnp
from jax import lax
from jax.experimental import pallas as pl
from jax.experimental.pallas import tpu as pltpu

F32 = jnp.float32
BF16 = jnp.bfloat16

N_MIXERS = 3
N_HEADS = 16
KV_HEADS = 4
HEAD_DIM = 128
GROUP = N_HEADS // KV_HEADS
WINDOW = 128
GRID_W = 64
DIFF_HEADS = 8
DIFF_KV_HEADS = 2
DIFF_GROUP = DIFF_HEADS // DIFF_KV_HEADS
DIFF_DIM = 128
ROPE_THETA = 10000.0
EPS = 1e-6
NEG_INF = -1e30

LANES = 128
SUBLANES = 8
V7X_VMEM_BYTES = 64 * 1024 * 1024
VMEM_LIMIT = V7X_VMEM_BYTES - 8 * 1024 * 1024

NT_DIMS = (((1,), (1,)), ((), ()))
LOG2E = math.log2(math.e)
ROW_CHUNK = 2 * SUBLANES
ATTN_BLOCK_ROWS = 256


def _params(*semantics):
    return pltpu.CompilerParams(dimension_semantics=semantics, vmem_limit_bytes=VMEM_LIMIT)


def _rms(x):
    return x * lax.rsqrt(jnp.mean(x * x, axis=-1, keepdims=True) + EPS)


def _mod_kernel(c_ref, w_ref, b_ref, o_ref):
    c = c_ref[...]
    a = (c * jax.nn.sigmoid(c)).astype(BF16)
    o_ref[...] = jnp.dot(a, w_ref[...].astype(BF16), preferred_element_type=F32) + b_ref[...]


def _modulation(cond, w_ada, b_ada):
    n_layers, d, n = w_ada.shape
    rows = cond.shape[0]
    tn = 1024
    return pl.pallas_call(
        _mod_kernel,
        out_shape=jax.ShapeDtypeStruct((n_layers, rows, n), F32),
        grid=(n_layers, n // tn),
        in_specs=[
            pl.BlockSpec((rows, d), lambda l, j: (0, 0)),
            pl.BlockSpec((None, d, tn), lambda l, j: (l, 0, j)),
            pl.BlockSpec((None, 1, tn), lambda l, j: (l, 0, j)),
        ],
        out_specs=pl.BlockSpec((None, rows, tn), lambda l, j: (l, 0, j)),
        compiler_params=_params("parallel", "parallel"),
        name="adaln_mod",
    )(cond, w_ada, b_ada.reshape(n_layers, 1, n))


def _qkv_kernel(*refs, d, nq, nk, qk_norm, rope, q_scale):
    it = iter(refs)
    x_ref, mod_ref, g_ref, w_ref = next(it), next(it), next(it), next(it)
    gq_ref = gk_ref = cos_ref = sin_ref = None
    if qk_norm:
        gq_ref, gk_ref = next(it), next(it)
    if rope:
        cos_ref, sin_ref = next(it), next(it)
    q_ref, k_ref, v_ref = next(it), next(it), next(it)

    mod = mod_ref[...]
    y = _rms(x_ref[...]) * g_ref[0:1, :]
    h = (y * (1.0 + mod[:, d:2 * d]) + mod[:, 0:d]).astype(BF16)
    tm = h.shape[0]

    if rope:
        cos = cos_ref[...]
        sin = sin_ref[...]
        lane = lax.broadcasted_iota(jnp.int32, (tm, LANES), 1)
        upper = (lane & 32) != 0

    def finish(z, gain_ref):
        if gain_ref is not None:
            z = _rms(z) * gain_ref[...]
        if rope:
            partner = jnp.where(upper, pltpu.roll(z, 32, 1), pltpu.roll(z, LANES - 32, 1))
            z = z * cos + partner * sin
        return z

    chunk = 512
    for c0 in range(0, nq + nk, chunk):
        acc = jnp.dot(h, w_ref[:, c0:c0 + chunk], preferred_element_type=F32)
        for j in range(chunk // LANES):
            col = c0 + j * LANES
            z = acc[:, j * LANES:(j + 1) * LANES]
            if col < nq:
                q_ref[:, col:col + LANES] = (finish(z, gq_ref) * q_scale).astype(q_ref.dtype)
            else:
                k_ref[:, col - nq:col - nq + LANES] = finish(z, gk_ref).astype(k_ref.dtype)
    v_ref[...] = jnp.dot(h, w_ref[:, nq + nk:], preferred_element_type=F32).astype(v_ref.dtype)


def _qkv(x, mod, g_norm, w_qkv, layer, slot, row_of_tile, *, tm, nq, nk, kv_dtype,
         gains=None, rope_tables=None):
    t, d = x.shape
    n_all = w_qkv.shape[-1]
    nv = n_all - nq - nk
    operands = [x, mod, g_norm, w_qkv]
    in_specs = [
        pl.BlockSpec((tm, d), lambda i: (i, 0)),
        pl.BlockSpec((None, None, 1, mod.shape[-1]), lambda i: (layer, row_of_tile(i), 0, 0)),
        pl.BlockSpec((None, 4, d), lambda i: (layer, 0, 0)),
        pl.BlockSpec((None, d, n_all), lambda i: (slot, 0, 0), pipeline_mode=pl.Buffered(1)),
    ]
    if gains is not None:
        for g in gains:
            operands.append(g.reshape(g.shape[0], 1, LANES))
            in_specs.append(pl.BlockSpec((None, 1, LANES), lambda i: (slot, 0, 0)))
    if rope_tables is not None:
        tiles_per_seq = rope_tables[0].shape[0] // tm
        for tab in rope_tables:
            operands.append(tab)
            in_specs.append(pl.BlockSpec((tm, LANES), lambda i: (i % tiles_per_seq, 0)))
    kern = functools.partial(_qkv_kernel, d=d, nq=nq, nk=nk, qk_norm=gains is not None,
                             rope=rope_tables is not None, q_scale=HEAD_DIM ** -0.5 * LOG2E)
    return pl.pallas_call(
        kern,
        out_shape=(jax.ShapeDtypeStruct((t, nq), BF16),
                   jax.ShapeDtypeStruct((t, nk), kv_dtype),
                   jax.ShapeDtypeStruct((t, nv), kv_dtype)),
        grid=(t // tm,),
        in_specs=in_specs,
        out_specs=(pl.BlockSpec((tm, nq), lambda i: (i, 0)),
                   pl.BlockSpec((tm, nk), lambda i: (i, 0)),
                   pl.BlockSpec((tm, nv), lambda i: (i, 0))),
        compiler_params=_params("parallel"),
        name="qkv_proj",
    )(*operands)


def _gqa_attn_kernel(*refs, tq, kv_heads, block_rows, n_new, n_ctx, band, has_sink):
    it = iter(refs)
    sink_ref = next(it) if has_sink else None
    q_ref, k_ref, v_ref = next(it), next(it), next(it)
    kc_ref = vc_ref = None
    if n_ctx:
        kc_ref, vc_ref = next(it), next(it)
    o_ref = next(it)
    s_scr, p_scr = next(it), next(it)
    bias_scr = next(it) if band else None

    band_width = tq + 2 * WINDOW
    q0 = pl.program_id(2) * tq
    if band and n_new > band_width:
        start = pl.multiple_of(jnp.clip(q0 - WINDOW, 0, n_new - band_width), LANES)
        k = k_ref[pl.ds(start, band_width), :]
        v = v_ref[pl.ds(start, band_width), :]
    else:
        start = 0
        k = k_ref[...]
        v = v_ref[...]
    k = k.astype(BF16)
    v = v.astype(BF16)
    nk = k.shape[0]
    if n_ctx:
        k = jnp.concatenate([k, kc_ref[...].astype(BF16)], axis=0)
        v = jnp.concatenate([v, vc_ref[...].astype(BF16)], axis=0)
    if band:
        qpos = q0 + lax.broadcasted_iota(jnp.int32, (tq, nk), 0)
        kpos = start + lax.broadcasted_iota(jnp.int32, (tq, nk), 1)
        bias_scr[...] = jnp.where(jnp.abs(qpos - kpos) <= WINDOW, 0.0, NEG_INF).astype(F32)
    ones = jnp.ones((k.shape[0], HEAD_DIM), BF16)

    if block_rows <= tq:
        blocks = [((h,), r0, block_rows) for h in range(GROUP) for r0 in range(0, tq, block_rows)]
    else:
        stacked = block_rows // tq
        blocks = [(tuple(range(h0, h0 + stacked)), 0, tq) for h0 in range(0, GROUP, stacked)]

    base = 0
    for kvh in range(kv_heads):
        k_h = k[:, kvh * HEAD_DIM:(kvh + 1) * HEAD_DIM]
        v_ext = jnp.concatenate([v[:, kvh * HEAD_DIM:(kvh + 1) * HEAD_DIM], ones], axis=1)
        for heads, row0, n in blocks:
            rows = len(heads) * n
            cols = [(kvh * GROUP + h) * HEAD_DIM for h in heads]
            qb = jnp.concatenate([q_ref[row0:row0 + n, c:c + HEAD_DIM] for c in cols], axis=0)
            s_scr[base:base + rows, :] = lax.dot_general(qb, k_h, NT_DIMS, preferred_element_type=F32)
            sink_terms = []
            for r in range(0, rows, ROW_CHUNK):
                rs = slice(base + r, base + r + ROW_CHUNK)
                if band:
                    qr = row0 + r % n
                    parts = [s_scr[rs, 0:nk] + bias_scr[qr:qr + ROW_CHUNK, :]]
                    if n_ctx:
                        parts.append(s_scr[rs, nk:])
                else:
                    parts = [s_scr[rs, :]]
                m = functools.reduce(jnp.maximum, [jnp.max(a, axis=-1, keepdims=True) for a in parts])
                if has_sink:
                    head = (pl.program_id(1) * kv_heads + kvh) * GROUP + heads[r // n]
                    sk = sink_ref[head] * LOG2E
                    m = jnp.maximum(m, sk)
                    sink_terms.append(jnp.exp2(sk - m))
                col = 0
                for a in parts:
                    p_scr[rs, col:col + a.shape[1]] = jnp.exp2(a - m).astype(BF16)
                    col += a.shape[1]
            acc = jnp.dot(p_scr[base:base + rows, :], v_ext, preferred_element_type=F32)
            denom = acc[:, HEAD_DIM:]
            if has_sink:
                denom = denom + jnp.concatenate(sink_terms, axis=0)
            o = acc[:, :HEAD_DIM] / denom
            for j, c in enumerate(cols):
                o_ref[row0:row0 + n, c:c + HEAD_DIM] = o[j * n:(j + 1) * n, :].astype(o_ref.dtype)
            base += rows


def _gqa_attention(q, k, v, cache, slot, sink, *, batch, seq, tq, kv_heads, block_rows, band):
    nq_tiles = seq // tq
    n_ctx = 0 if cache is None else cache[0].shape[2]
    qw, kw = kv_heads * GROUP * HEAD_DIM, kv_heads * HEAD_DIM
    operands, in_specs = [], []
    if sink is not None:
        operands.append(sink)
        in_specs.append(pl.BlockSpec(memory_space=pltpu.SMEM))
    operands += [q, k, v]
    in_specs += [
        pl.BlockSpec((tq, qw), lambda b, h, i: (b * nq_tiles + i, h)),
        pl.BlockSpec((seq, kw), lambda b, h, i: (b, h)),
        pl.BlockSpec((seq, kw), lambda b, h, i: (b, h)),
    ]
    if cache is not None:
        operands += list(cache)
        in_specs += [pl.BlockSpec((None, None, n_ctx, kw), lambda b, h, i: (b, slot, 0, h))] * 2
    kern = functools.partial(_gqa_attn_kernel, tq=tq, kv_heads=kv_heads, block_rows=block_rows, n_new=seq,
                             n_ctx=n_ctx, band=band, has_sink=sink is not None)
    n_new_keys = min(seq, tq + 2 * WINDOW) if band else seq
    n_keys = n_new_keys + n_ctx
    scratch = [pltpu.VMEM((kv_heads * GROUP * tq, n_keys), F32), pltpu.VMEM((kv_heads * GROUP * tq, n_keys), BF16)]
    if band:
        scratch.append(pltpu.VMEM((tq, n_new_keys), F32))
    return pl.pallas_call(
        kern,
        out_shape=jax.ShapeDtypeStruct(q.shape, BF16),
        grid=(batch, KV_HEADS // kv_heads, nq_tiles),
        in_specs=in_specs,
        out_specs=pl.BlockSpec((tq, qw), lambda b, h, i: (b * nq_tiles + i, h)),
        scratch_shapes=scratch,
        compiler_params=_params("parallel", "parallel", "parallel"),
        name="gqa_attn",
    )(*operands)


def _diff_attn_kernel(*refs, tq, kv_heads, block_heads, n_ctx, lam_init):
    it = iter(refs)
    lq1_ref, lk1_ref, lq2_ref, lk2_ref, gs_ref = next(it), next(it), next(it), next(it), next(it)
    q_ref, k_ref, v_ref = next(it), next(it), next(it)
    kc_ref = vc_ref = None
    if n_ctx:
        kc_ref, vc_ref = next(it), next(it)
    o_ref = next(it)
    s_scrs = (next(it), next(it))
    p_scr = next(it)

    lam = (jnp.exp(jnp.sum(lq1_ref[...] * lk1_ref[...], axis=-1, keepdims=True))
           - jnp.exp(jnp.sum(lq2_ref[...] * lk2_ref[...], axis=-1, keepdims=True)) + lam_init)
    k = k_ref[...].astype(BF16)
    v = v_ref[...].astype(BF16)
    if n_ctx:
        k = jnp.concatenate([k, kc_ref[...].astype(BF16)], axis=0)
        v = jnp.concatenate([v, vc_ref[...].astype(BF16)], axis=0)

    width = 2 * DIFF_DIM
    rows = block_heads * tq
    base = 0
    for kvh in range(kv_heads):
        k_h = k[:, kvh * width:(kvh + 1) * width]
        v_h = v[:, kvh * width:(kvh + 1) * width]
        for h0 in range(0, DIFF_GROUP, block_heads):
            heads = [kvh * DIFF_GROUP + g for g in range(h0, h0 + block_heads)]
            for half in range(2):
                qb = jnp.concatenate(
                    [q_ref[:, (2 * g + half) * DIFF_DIM:(2 * g + half + 1) * DIFF_DIM] for g in heads], axis=0)
                s_scrs[half][base:base + rows, :] = lax.dot_general(
                    qb, k_h[:, half * DIFF_DIM:(half + 1) * DIFF_DIM], NT_DIMS, preferred_element_type=F32)
            for r in range(0, rows, ROW_CHUNK):
                rs = slice(base + r, base + r + ROW_CHUNK)
                s0 = s_scrs[0][rs, :]
                e0 = jnp.exp2(s0 - jnp.max(s0, axis=-1, keepdims=True))
                r0 = 1.0 / jnp.sum(e0, axis=-1, keepdims=True)
                s1 = s_scrs[1][rs, :]
                e1 = jnp.exp2(s1 - jnp.max(s1, axis=-1, keepdims=True))
                r1 = lam / jnp.sum(e1, axis=-1, keepdims=True)
                p_scr[rs, :] = (e0 * r0 - e1 * r1).astype(BF16)
            acc = jnp.dot(p_scr[base:base + rows, :], v_h, preferred_element_type=F32)
            y = (_rms(acc) * gs_ref[...]) * (1.0 - lam_init)
            for j, g in enumerate(heads):
                o_ref[:, g * width:(g + 1) * width] = y[j * tq:(j + 1) * tq, :].astype(o_ref.dtype)
            base += rows


def _diff_attention(q, k, v, cache, slot, lam_vecs, g_subln, *, batch, seq, tq, kv_heads, lam_init):
    nq_tiles = seq // tq
    n_ctx = 0 if cache is None else cache[0].shape[2]
    qw = kv_heads * DIFF_GROUP * 2 * DIFF_DIM
    kw = kv_heads * 2 * DIFF_DIM
    operands = [a.reshape(a.shape[0], 1, DIFF_DIM) for a in lam_vecs]
    in_specs = [pl.BlockSpec((None, 1, DIFF_DIM), lambda b, h, i: (slot, 0, 0))] * 4
    operands.append(g_subln.reshape(g_subln.shape[0], 1, 2 * DIFF_DIM))
    in_specs.append(pl.BlockSpec((None, 1, 2 * DIFF_DIM), lambda b, h, i: (slot, 0, 0)))
    operands += [q, k, v]
    in_specs += [
        pl.BlockSpec((tq, qw), lambda b, h, i: (b * nq_tiles + i, h)),
        pl.BlockSpec((seq, kw), lambda b, h, i: (b, h)),
        pl.BlockSpec((seq, kw), lambda b, h, i: (b, h)),
    ]
    if cache is not None:
        operands += list(cache)
        in_specs += [pl.BlockSpec((None, None, n_ctx, kw), lambda b, h, i: (b, slot, 0, h))] * 2
    kern = functools.partial(_diff_attn_kernel, tq=tq, kv_heads=kv_heads, block_heads=2, n_ctx=n_ctx,
                             lam_init=lam_init)
    score_shape = (kv_heads * DIFF_GROUP * tq, seq + n_ctx)
    return pl.pallas_call(
        kern,
        out_shape=jax.ShapeDtypeStruct(q.shape, BF16),
        grid=(batch, DIFF_KV_HEADS // kv_heads, nq_tiles),
        in_specs=in_specs,
        out_specs=pl.BlockSpec((tq, qw), lambda b, h, i: (b * nq_tiles + i, h)),
        scratch_shapes=[pltpu.VMEM(score_shape, F32), pltpu.VMEM(score_shape, F32),
                        pltpu.VMEM(score_shape, BF16)],
        compiler_params=_params("parallel", "parallel", "parallel"),
        name="diff_attn",
    )(*operands)


def _oproj_kernel(o_ref, w_ref, x_ref, mod_ref, g_ref, out_ref, h_ref, *, d, row_blocks):
    mod = mod_ref[...]
    gate1, shift2, scale2 = mod[:, 2 * d:3 * d], mod[:, 3 * d:4 * d], mod[:, 4 * d:5 * d]
    rows = o_ref.shape[0] // row_blocks
    for r in range(row_blocks):
        rs = slice(r * rows, (r + 1) * rows)
        m = jnp.dot(o_ref[rs, :], w_ref[...], preferred_element_type=F32)
        x_new = x_ref[rs, :] + gate1 * (_rms(m) * g_ref[1:2, :])
        out_ref[rs, :] = x_new
        h_ref[rs, :] = ((_rms(x_new) * g_ref[2:3, :]) * (1.0 + scale2) + shift2).astype(BF16)


def _oproj(o, w_o, x, mod, g_norm, layer, slot, row_of_tile, *, tm):
    t, d = x.shape
    nin = o.shape[1]
    return pl.pallas_call(
        functools.partial(_oproj_kernel, d=d, row_blocks=4 if tm % (4 * ROW_CHUNK) == 0 else 1),
        out_shape=(jax.ShapeDtypeStruct((t, d), F32), jax.ShapeDtypeStruct((t, d), BF16)),
        grid=(t // tm,),
        in_specs=[
            pl.BlockSpec((tm, nin), lambda i: (i, 0)),
            pl.BlockSpec((None, nin, d), lambda i: (slot, 0, 0), pipeline_mode=pl.Buffered(1)),
            pl.BlockSpec((tm, d), lambda i: (i, 0)),
            pl.BlockSpec((None, None, 1, mod.shape[-1]), lambda i: (layer, row_of_tile(i), 0, 0)),
            pl.BlockSpec((None, 4, d), lambda i: (layer, 0, 0)),
        ],
        out_specs=(pl.BlockSpec((tm, d), lambda i: (i, 0)), pl.BlockSpec((tm, d), lambda i: (i, 0))),
        compiler_params=_params("parallel"),
        name="out_proj",
    )(o, w_o, x, mod, g_norm)


def _ffn_kernel(h_ref, x_ref, mod_ref, g_ref, wg_ref, wu_ref, wd_ref, out_ref, *, d):
    j = pl.program_id(1)

    @pl.when(j == 0)
    def _():
        out_ref[...] = jnp.zeros_like(out_ref)

    h = h_ref[...]
    a = jnp.dot(h, wg_ref[...], preferred_element_type=F32)
    b = jnp.dot(h, wu_ref[...], preferred_element_type=F32)
    u = ((a * jax.nn.sigmoid(a)) * b).astype(BF16)
    out_ref[...] += jnp.dot(u, wd_ref[...], preferred_element_type=F32)

    @pl.when(j == pl.num_programs(1) - 1)
    def _():
        gate = mod_ref[...][:, 5 * d:6 * d]
        out_ref[...] = x_ref[...] + gate * (_rms(out_ref[...]) * g_ref[3:4, :])


def _ffn(h, x, mod, g_norm, w_gate, w_up, w_down, layer, row_of_tile, *, tm, fc):
    t, d = x.shape
    d_ff = w_gate.shape[-1]
    return pl.pallas_call(
        functools.partial(_ffn_kernel, d=d),
        out_shape=jax.ShapeDtypeStruct((t, d), F32),
        grid=(t // tm, d_ff // fc),
        in_specs=[
            pl.BlockSpec((tm, d), lambda i, j: (i, 0)),
            pl.BlockSpec((tm, d), lambda i, j: (i, 0)),
            pl.BlockSpec((None, None, 1, mod.shape[-1]), lambda i, j: (layer, row_of_tile(i), 0, 0)),
            pl.BlockSpec((None, 4, d), lambda i, j: (layer, 0, 0)),
            pl.BlockSpec((None, d, fc), lambda i, j: (layer, 0, j)),
            pl.BlockSpec((None, d, fc), lambda i, j: (layer, 0, j)),
            pl.BlockSpec((None, fc, d), lambda i, j: (layer, j, 0)),
        ],
        out_specs=pl.BlockSpec((tm, d), lambda i, j: (i, 0)),
        compiler_params=_params("parallel", "arbitrary"),
        name="ffn",
    )(h, x, mod, g_norm, w_gate, w_up, w_down)


def _rope_tables(n_tokens):
    nf = HEAD_DIM // 4
    t = jnp.arange(n_tokens, dtype=jnp.int32)
    rows = (t // GRID_W).astype(F32)
    cols = (t % GRID_W).astype(F32)
    inv = 1.0 / (ROPE_THETA ** (jnp.arange(nf, dtype=F32) / nf))
    ar = rows[:, None] * inv
    ac = cols[:, None] * inv
    cos = jnp.concatenate([jnp.cos(ar), jnp.cos(ar), jnp.cos(ac), jnp.cos(ac)], axis=-1)
    sin = jnp.concatenate([-jnp.sin(ar), jnp.sin(ar), -jnp.sin(ac), jnp.sin(ac)], axis=-1)
    return cos, sin


def _diff_lambda_init(layer):
    return 0.8 - 0.6 * math.exp(-0.3 * layer)


def _largest_tile(n, cap):
    t = cap
    while n % t:
        t //= 2
    return t


def _run_path(x3, first_row, rows_per_batch, caches, mod, p, depth):
    batch, seq, d = x3.shape
    latent = caches is not None
    x = x3.reshape(batch * seq, d)
    tm = _largest_tile(seq if latent else batch * seq, 512)
    tiles_per_batch = seq // tm if latent else None

    if latent:
        def row_of_tile(i):
            return first_row + (i // tiles_per_batch) * rows_per_batch
    else:
        def row_of_tile(i):
            return first_row

    rope_tables = _rope_tables(seq) if latent else None
    kv_dtype = BF16 if latent else F32
    kept = ([], [], [], [], [], [])
    for layer in range(depth):
        kind, slot = layer % N_MIXERS, layer // N_MIXERS
        cache = None if not latent else (caches[2 * kind], caches[2 * kind + 1])
        common = dict(tm=tm, kv_dtype=kv_dtype, rope_tables=rope_tables)
        attn_shape = dict(kv_heads=1 if latent else KV_HEADS)
        if kind == 0:
            q, k, v = _qkv(x, mod, p['g_norm'], p['w_qkv_full'], layer, slot, row_of_tile,
                           nq=N_HEADS * HEAD_DIM, nk=KV_HEADS * HEAD_DIM,
                           gains=(p['g_q_full'], p['g_k_full']), **common)
            o = _gqa_attention(q, k, v, cache, slot, None, batch=batch, seq=seq, band=False,
                               tq=_largest_tile(seq, 512), block_rows=ATTN_BLOCK_ROWS, **attn_shape)
            w_o = p['w_o_full']
        elif kind == 1:
            q, k, v = _qkv(x, mod, p['g_norm'], p['w_qkv_win'], layer, slot, row_of_tile,
                           nq=N_HEADS * HEAD_DIM, nk=KV_HEADS * HEAD_DIM, **common)
            o = _gqa_attention(q, k, v, cache, slot, p['sink_win'][slot], batch=batch, seq=seq, band=latent,
                               tq=_largest_tile(seq, 256),
                               block_rows=2 * ATTN_BLOCK_ROWS if latent else ATTN_BLOCK_ROWS, **attn_shape)
            w_o = p['w_o_win']
        else:
            q, k, v = _qkv(x, mod, p['g_norm'], p['w_qkv_diff'], layer, slot, row_of_tile,
                           nq=DIFF_HEADS * 2 * DIFF_DIM, nk=DIFF_KV_HEADS * 2 * DIFF_DIM, **common)
            o = _diff_attention(q, k, v, cache, slot,
                                (p['lam_q1'], p['lam_k1'], p['lam_q2'], p['lam_k2']), p['g_subln_diff'],
                                batch=batch, seq=seq, tq=_largest_tile(seq, 256),
                                kv_heads=1,
                                lam_init=_diff_lambda_init(layer))
            w_o = p['w_o_diff']
        if not latent:
            kept[2 * kind].append(k)
            kept[2 * kind + 1].append(v)
        x, h = _oproj(o, w_o, x, mod, p['g_norm'], layer, slot, row_of_tile, tm=tm)
        x = _ffn(h, x, mod, p['g_norm'], p['w_gate'], p['w_up'], p['w_down'], layer, row_of_tile,
                 tm=tm, fc=512)
    return x.reshape(batch, seq, d), kept


def kernel(x_prompt, x_sample, c, cache_k_full, cache_v_full, cache_k_win, cache_v_win, cache_k_diff,
           cache_v_diff, c_ctx, w_ada, b_ada, g_norm, w_qkv_full, w_o_full, g_q_full, g_k_full, w_qkv_win,
           w_o_win, sink_win, w_qkv_diff, w_o_diff, lam_q1, lam_k1, lam_q2, lam_k2, g_subln_diff, w_gate,
           w_up, w_down):
    depth, d = w_ada.shape[0], w_ada.shape[1]
    batch, seq, _ = x_prompt.shape
    dec_batch, dec_seq, _ = x_sample.shape

    p = {
        'g_norm': g_norm,
        'w_qkv_full': w_qkv_full.astype(BF16), 'w_o_full': w_o_full.astype(BF16),
        'g_q_full': g_q_full, 'g_k_full': g_k_full,
        'w_qkv_win': w_qkv_win.astype(BF16), 'w_o_win': w_o_win.astype(BF16), 'sink_win': sink_win,
        'w_qkv_diff': w_qkv_diff.astype(BF16), 'w_o_diff': w_o_diff.astype(BF16),
        'lam_q1': lam_q1, 'lam_k1': lam_k1, 'lam_q2': lam_q2, 'lam_k2': lam_k2,
        'g_subln_diff': g_subln_diff,
        'w_gate': w_gate.astype(BF16), 'w_up': w_up.astype(BF16), 'w_down': w_down.astype(BF16),
    }

    n_rows = -(-(1 + dec_batch) // SUBLANES) * SUBLANES
    cond = jnp.zeros((n_rows, d), F32).at[0].set(c_ctx).at[1:1 + dec_batch].set(c)
    mod = _modulation(cond, w_ada, b_ada)
    mod = mod.reshape(depth, n_rows, 1, 6 * d)

    y_prompt, kept = _run_path(x_prompt, 0, 0, None, mod, p, depth)

    def flat_cache(a):
        return a.reshape(a.shape[0], a.shape[1], a.shape[2], -1)

    caches = tuple(flat_cache(a) for a in (cache_k_full, cache_v_full, cache_k_win, cache_v_win,
                                           cache_k_diff, cache_v_diff))
    y_sample, _ = _run_path(x_sample, 1, 1, caches, mod, p, depth)

    def stack(parts, tail):
        return jnp.stack([a.reshape((batch, seq) + tail) for a in parts], axis=1)

    return (y_prompt, y_sample,
            stack(kept[0], (KV_HEADS, HEAD_DIM)), stack(kept[1], (KV_HEADS, HEAD_DIM)),
            stack(kept[2], (KV_HEADS, HEAD_DIM)), stack(kept[3], (KV_HEADS, HEAD_DIM)),
            stack(kept[4], (DIFF_KV_HEADS, 2, DIFF_DIM)), stack(kept[5], (DIFF_KV_HEADS, 2 * DIFF_DIM)))
```

```python
import functools
import math

import jax
import jax.numpy as jnp
from jax import lax
from jax.experimental import pallas as pl
from jax.experimental.pallas import tpu as pltpu

F32 = jnp.float32
BF16 = jnp.bfloat16

N_MIXERS = 3
N_HEADS = 16
KV_HEADS = 4
HEAD_DIM = 128
GROUP = N_HEADS // KV_HEADS
WINDOW = 128
GRID_W = 64
DIFF_HEADS = 8
DIFF_KV_HEADS = 2
DIFF_GROUP = DIFF_HEADS // DIFF_KV_HEADS
DIFF_DIM = 128
ROPE_THETA = 10000.0
EPS = 1e-6
NEG_INF = -1e30

LANES = 128
SUBLANES = 8
V7X_VMEM_BYTES = 64 * 1024 * 1024
VMEM_LIMIT = V7X_VMEM_BYTES - 8 * 1024 * 1024

NT_DIMS = (((1,), (1,)), ((), ()))
LOG2E = math.log2(math.e)
ROW_CHUNK = 2 * SUBLANES
ATTN_BLOCK_ROWS = 256
FFN_CHUNK = 512


def _params(*semantics):
    return pltpu.CompilerParams(dimension_semantics=semantics, vmem_limit_bytes=VMEM_LIMIT)


def _rms(x):
    return x * lax.rsqrt(jnp.mean(x * x, axis=-1, keepdims=True) + EPS)


def _mod_kernel(c_ref, w_ref, b_ref, o_ref):
    c = c_ref[...]
    a = (c * jax.nn.sigmoid(c)).astype(BF16)
    o_ref[...] = jnp.dot(a, w_ref[...].astype(BF16), preferred_element_type=F32) + b_ref[...]


def _modulation(cond, w_ada, b_ada):
    n_layers, d, n = w_ada.shape
    rows = cond.shape[0]
    tn = 1024
    return pl.pallas_call(
        _mod_kernel,
        out_shape=jax.ShapeDtypeStruct((n_layers, rows, n), F32),
        grid=(n_layers, n // tn),
        in_specs=[
            pl.BlockSpec((rows, d), lambda l, j: (0, 0)),
            pl.BlockSpec((None, d, tn), lambda l, j: (l, 0, j)),
            pl.BlockSpec((None, 1, tn), lambda l, j: (l, 0, j)),
        ],
        out_specs=pl.BlockSpec((None, rows, tn), lambda l, j: (l, 0, j)),
        compiler_params=_params("parallel", "parallel"),
        name="adaln_mod",
    )(cond, w_ada, b_ada.reshape(n_layers, 1, n))


def _qkv_kernel(*refs, d, nq, nk, qk_norm, rope, q_scale):
    it = iter(refs)
    x_ref, mod_ref, g_ref, w_ref = next(it), next(it), next(it), next(it)
    gq_ref = gk_ref = cos_ref = sin_ref = None
    if qk_norm:
        gq_ref, gk_ref = next(it), next(it)
    if rope:
        cos_ref, sin_ref = next(it), next(it)
    q_ref, k_ref, v_ref = next(it), next(it), next(it)

    mod = mod_ref[...]
    y = _rms(x_ref[...]) * g_ref[0:1, :]
    h = (y * (1.0 + mod[:, d:2 * d]) + mod[:, 0:d]).astype(BF16)
    tm = h.shape[0]

    if rope:
        cos = cos_ref[...]
        sin = sin_ref[...]
        lane = lax.broadcasted_iota(jnp.int32, (tm, LANES), 1)
        upper = (lane & 32) != 0

    def finish(z, gain_ref):
        if gain_ref is not None:
            z = _rms(z) * gain_ref[...]
        if rope:
            partner = jnp.where(upper, pltpu.roll(z, 32, 1), pltpu.roll(z, LANES - 32, 1))
            z = z * cos + partner * sin
        return z

    chunk = 512
    for c0 in range(0, nq + nk, chunk):
        acc = jnp.dot(h, w_ref[:, c0:c0 + chunk], preferred_element_type=F32)
        for j in range(chunk // LANES):
            col = c0 + j * LANES
            z = acc[:, j * LANES:(j + 1) * LANES]
            if col < nq:
                q_ref[:, col:col + LANES] = (finish(z, gq_ref) * q_scale).astype(q_ref.dtype)
            else:
                k_ref[:, col - nq:col - nq + LANES] = finish(z, gk_ref).astype(k_ref.dtype)
    v_ref[...] = jnp.dot(h, w_ref[:, nq + nk:], preferred_element_type=F32).astype(v_ref.dtype)


def _qkv(x, mod, g_norm, w_qkv, layer, slot, row_of_tile, *, tm, nq, nk, kv_dtype,
         gains=None, rope_tables=None):
    t, d = x.shape
    n_all = w_qkv.shape[-1]
    nv = n_all - nq - nk
    operands = [x, mod, g_norm, w_qkv]
    in_specs = [
        pl.BlockSpec((tm, d), lambda i: (i, 0)),
        pl.BlockSpec((None, None, 1, mod.shape[-1]), lambda i: (layer, row_of_tile(i), 0, 0)),
        pl.BlockSpec((None, 4, d), lambda i: (layer, 0, 0)),
        pl.BlockSpec((None, d, n_all), lambda i: (slot, 0, 0), pipeline_mode=pl.Buffered(1)),
    ]
    if gains is not None:
        for g in gains:
            operands.append(g.reshape(g.shape[0], 1, LANES))
            in_specs.append(pl.BlockSpec((None, 1, LANES), lambda i: (slot, 0, 0)))
    if rope_tables is not None:
        tiles_per_seq = rope_tables[0].shape[0] // tm
        for tab in rope_tables:
            operands.append(tab)
            in_specs.append(pl.BlockSpec((tm, LANES), lambda i: (i % tiles_per_seq, 0)))
    kern = functools.partial(_qkv_kernel, d=d, nq=nq, nk=nk, qk_norm=gains is not None,
                             rope=rope_tables is not None, q_scale=HEAD_DIM ** -0.5 * LOG2E)
    return pl.pallas_call(
        kern,
        out_shape=(jax.ShapeDtypeStruct((t, nq), BF16),
                   jax.ShapeDtypeStruct((t, nk), kv_dtype),
                   jax.ShapeDtypeStruct((t, nv), kv_dtype)),
        grid=(t // tm,),
        in_specs=in_specs,
        out_specs=(pl.BlockSpec((tm, nq), lambda i: (i, 0)),
                   pl.BlockSpec((tm, nk), lambda i: (i, 0)),
                   pl.BlockSpec((tm, nv), lambda i: (i, 0))),
        compiler_params=_params("parallel"),
        name="qkv_proj",
    )(*operands)


def _gqa_attn_kernel(*refs, tq, kv_heads, block_rows, n_new, n_ctx, band, has_sink):
    it = iter(refs)
    sink_ref = next(it) if has_sink else None
    q_ref, k_ref, v_ref = next(it), next(it), next(it)
    kc_ref = vc_ref = None
    if n_ctx:
        kc_ref, vc_ref = next(it), next(it)
    o_ref = next(it)
    s_scr, p_scr = next(it), next(it)
    bias_scr = next(it) if band else None

    band_width = tq + 2 * WINDOW
    q0 = pl.program_id(2) * tq
    if band and n_new > band_width:
        start = pl.multiple_of(jnp.clip(q0 - WINDOW, 0, n_new - band_width), LANES)
        k = k_ref[pl.ds(start, band_width), :]
        v = v_ref[pl.ds(start, band_width), :]
    else:
        start = 0
        k = k_ref[...]
        v = v_ref[...]
    k = k.astype(BF16)
    v = v.astype(BF16)
    nk = k.shape[0]
    if n_ctx:
        k = jnp.concatenate([k, kc_ref[...].astype(BF16)], axis=0)
        v = jnp.concatenate([v, vc_ref[...].astype(BF16)], axis=0)
    if band:
        qpos = q0 + lax.broadcasted_iota(jnp.int32, (tq, nk), 0)
        kpos = start + lax.broadcasted_iota(jnp.int32, (tq, nk), 1)
        bias_scr[...] = jnp.where(jnp.abs(qpos - kpos) <= WINDOW, 0.0, NEG_INF).astype(F32)
    ones = jnp.ones((k.shape[0], HEAD_DIM), BF16)

    if block_rows <= tq:
        blocks = [((h,), r0, block_rows) for h in range(GROUP) for r0 in range(0, tq, block_rows)]
    else:
        stacked = block_rows // tq
        blocks = [(tuple(range(h0, h0 + stacked)), 0, tq) for h0 in range(0, GROUP, stacked)]

    base = 0
    for kvh in range(kv_heads):
        k_h = k[:, kvh * HEAD_DIM:(kvh + 1) * HEAD_DIM]
        v_ext = jnp.concatenate([v[:, kvh * HEAD_DIM:(kvh + 1) * HEAD_DIM], ones], axis=1)
        for heads, row0, n in blocks:
            rows = len(heads) * n
            cols = [(kvh * GROUP + h) * HEAD_DIM for h in heads]
            qb = jnp.concatenate([q_ref[row0:row0 + n, c:c + HEAD_DIM] for c in cols], axis=0)
            s_scr[base:base + rows, :] = lax.dot_general(qb, k_h, NT_DIMS, preferred_element_type=F32)
            sink_terms = []
            for r in range(0, rows, ROW_CHUNK):
                rs = slice(base + r, base + r + ROW_CHUNK)
                if band:
                    qr = row0 + r % n
                    parts = [s_scr[rs, 0:nk] + bias_scr[qr:qr + ROW_CHUNK, :]]
                    if n_ctx:
                        parts.append(s_scr[rs, nk:])
                else:
                    parts = [s_scr[rs, :]]
                m = functools.reduce(jnp.maximum, [jnp.max(a, axis=-1, keepdims=True) for a in parts])
                if has_sink:
                    head = (pl.program_id(1) * kv_heads + kvh) * GROUP + heads[r // n]
                    sk = sink_ref[head] * LOG2E
                    m = jnp.maximum(m, sk)
                    sink_terms.append(jnp.exp2(sk - m))
                col = 0
                for a in parts:
                    p_scr[rs, col:col + a.shape[1]] = jnp.exp2(a - m).astype(BF16)
                    col += a.shape[1]
            acc = jnp.dot(p_scr[base:base + rows, :], v_ext, preferred_element_type=F32)
            denom = acc[:, HEAD_DIM:]
            if has_sink:
                denom = denom + jnp.concatenate(sink_terms, axis=0)
            o = acc[:, :HEAD_DIM] / denom
            for j, c in enumerate(cols):
                o_ref[row0:row0 + n, c:c + HEAD_DIM] = o[j * n:(j + 1) * n, :].astype(o_ref.dtype)
            base += rows


def _gqa_attention(q, k, v, cache, slot, sink, *, batch, seq, tq, kv_heads, block_rows, band):
    nq_tiles = seq // tq
    n_ctx = 0 if cache is None else cache[0].shape[2]
    qw, kw = kv_heads * GROUP * HEAD_DIM, kv_heads * HEAD_DIM
    operands, in_specs = [], []
    if sink is not None:
        operands.append(sink)
        in_specs.append(pl.BlockSpec(memory_space=pltpu.SMEM))
    operands += [q, k, v]
    in_specs += [
        pl.BlockSpec((tq, qw), lambda b, h, i: (b * nq_tiles + i, h)),
        pl.BlockSpec((seq, kw), lambda b, h, i: (b, h)),
        pl.BlockSpec((seq, kw), lambda b, h, i: (b, h)),
    ]
    if cache is not None:
        operands += list(cache)
        in_specs += [pl.BlockSpec((None, None, n_ctx, kw), lambda b, h, i: (b, slot, 0, h))] * 2
    kern = functools.partial(_gqa_attn_kernel, tq=tq, kv_heads=kv_heads, block_rows=block_rows, n_new=seq,
                             n_ctx=n_ctx, band=band, has_sink=sink is not None)
    n_new_keys = min(seq, tq + 2 * WINDOW) if band else seq
    n_keys = n_new_keys + n_ctx
    scratch = [pltpu.VMEM((kv_heads * GROUP * tq, n_keys), F32), pltpu.VMEM((kv_heads * GROUP * tq, n_keys), BF16)]
    if band:
        scratch.append(pltpu.VMEM((tq, n_new_keys), F32))
    return pl.pallas_call(
        kern,
        out_shape=jax.ShapeDtypeStruct(q.shape, BF16),
        grid=(batch, KV_HEADS // kv_heads, nq_tiles),
        in_specs=in_specs,
        out_specs=pl.BlockSpec((tq, qw), lambda b, h, i: (b * nq_tiles + i, h)),
        scratch_shapes=scratch,
        compiler_params=_params("parallel", "parallel", "parallel"),
        name="gqa_attn",
    )(*operands)


def _diff_attn_kernel(*refs, tq, kv_heads, block_heads, n_ctx, lam_init):
    it = iter(refs)
    lq1_ref, lk1_ref, lq2_ref, lk2_ref, gs_ref = next(it), next(it), next(it), next(it), next(it)
    q_ref, k_ref, v_ref = next(it), next(it), next(it)
    kc_ref = vc_ref = None
    if n_ctx:
        kc_ref, vc_ref = next(it), next(it)
    o_ref = next(it)
    s_scrs = (next(it), next(it))
    p_scr = next(it)

    lam = (jnp.exp(jnp.sum(lq1_ref[...] * lk1_ref[...], axis=-1, keepdims=True))
           - jnp.exp(jnp.sum(lq2_ref[...] * lk2_ref[...], axis=-1, keepdims=True)) + lam_init)
    k = k_ref[...].astype(BF16)
    v = v_ref[...].astype(BF16)
    if n_ctx:
        k = jnp.concatenate([k, kc_ref[...].astype(BF16)], axis=0)
        v = jnp.concatenate([v, vc_ref[...].astype(BF16)], axis=0)

    width = 2 * DIFF_DIM
    rows = block_heads * tq
    base = 0
    for kvh in range(kv_heads):
        k_h = k[:, kvh * width:(kvh + 1) * width]
        v_h = v[:, kvh * width:(kvh + 1) * width]
        for h0 in range(0, DIFF_GROUP, block_heads):
            heads = [kvh * DIFF_GROUP + g for g in range(h0, h0 + block_heads)]
            for half in range(2):
                qb = jnp.concatenate(
                    [q_ref[:, (2 * g + half) * DIFF_DIM:(2 * g + half + 1) * DIFF_DIM] for g in heads], axis=0)
                s_scrs[half][base:base + rows, :] = lax.dot_general(
                    qb, k_h[:, half * DIFF_DIM:(half + 1) * DIFF_DIM], NT_DIMS, preferred_element_type=F32)
            for r in range(0, rows, ROW_CHUNK):
                rs = slice(base + r, base + r + ROW_CHUNK)
                s0 = s_scrs[0][rs, :]
                e0 = jnp.exp2(s0 - jnp.max(s0, axis=-1, keepdims=True))
                r0 = 1.0 / jnp.sum(e0, axis=-1, keepdims=True)
                s1 = s_scrs[1][rs, :]
                e1 = jnp.exp2(s1 - jnp.max(s1, axis=-1, keepdims=True))
                r1 = lam / jnp.sum(e1, axis=-1, keepdims=True)
                p_scr[rs, :] = (e0 * r0 - e1 * r1).astype(BF16)
            acc = jnp.dot(p_scr[base:base + rows, :], v_h, preferred_element_type=F32)
            y = (_rms(acc) * gs_ref[...]) * (1.0 - lam_init)
            for j, g in enumerate(heads):
                o_ref[:, g * width:(g + 1) * width] = y[j * tq:(j + 1) * tq, :].astype(o_ref.dtype)
            base += rows


def _diff_attention(q, k, v, cache, slot, lam_vecs, g_subln, *, batch, seq, tq, kv_heads, lam_init):
    nq_tiles = seq // tq
    n_ctx = 0 if cache is None else cache[0].shape[2]
    qw = kv_heads * DIFF_GROUP * 2 * DIFF_DIM
    kw = kv_heads * 2 * DIFF_DIM
    operands = [a.reshape(a.shape[0], 1, DIFF_DIM) for a in lam_vecs]
    in_specs = [pl.BlockSpec((None, 1, DIFF_DIM), lambda b, h, i: (slot, 0, 0))] * 4
    operands.append(g_subln.reshape(g_subln.shape[0], 1, 2 * DIFF_DIM))
    in_specs.append(pl.BlockSpec((None, 1, 2 * DIFF_DIM), lambda b, h, i: (slot, 0, 0)))
    operands += [q, k, v]
    in_specs += [
        pl.BlockSpec((tq, qw), lambda b, h, i: (b * nq_tiles + i, h)),
        pl.BlockSpec((seq, kw), lambda b, h, i: (b, h)),
        pl.BlockSpec((seq, kw), lambda b, h, i: (b, h)),
    ]
    if cache is not None:
        operands += list(cache)
        in_specs += [pl.BlockSpec((None, None, n_ctx, kw), lambda b, h, i: (b, slot, 0, h))] * 2
    kern = functools.partial(_diff_attn_kernel, tq=tq, kv_heads=kv_heads, block_heads=2, n_ctx=n_ctx,
                             lam_init=lam_init)
    score_shape = (kv_heads * DIFF_GROUP * tq, seq + n_ctx)
    return pl.pallas_call(
        kern,
        out_shape=jax.ShapeDtypeStruct(q.shape, BF16),
        grid=(batch, DIFF_KV_HEADS // kv_heads, nq_tiles),
        in_specs=in_specs,
        out_specs=pl.BlockSpec((tq, qw), lambda b, h, i: (b * nq_tiles + i, h)),
        scratch_shapes=[pltpu.VMEM(score_shape, F32), pltpu.VMEM(score_shape, F32),
                        pltpu.VMEM(score_shape, BF16)],
        compiler_params=_params("parallel", "parallel", "parallel"),
        name="diff_attn",
    )(*operands)


def _oproj_kernel(o_ref, w_ref, x_ref, mod_ref, g_ref, out_ref, h_ref, *, d, row_blocks):
    mod = mod_ref[...]
    gate1, shift2, scale2 = mod[:, 2 * d:3 * d], mod[:, 3 * d:4 * d], mod[:, 4 * d:5 * d]
    rows = o_ref.shape[0] // row_blocks
    for r in range(row_blocks):
        rs = slice(r * rows, (r + 1) * rows)
        m = jnp.dot(o_ref[rs, :], w_ref[...], preferred_element_type=F32)
        x_new = x_ref[rs, :] + gate1 * (_rms(m) * g_ref[1:2, :])
        out_ref[rs, :] = x_new
        h_ref[rs, :] = ((_rms(x_new) * g_ref[2:3, :]) * (1.0 + scale2) + shift2).astype(BF16)


def _oproj(o, w_o, x, mod, g_norm, layer, slot, row_of_tile, *, tm):
    t, d = x.shape
    nin = o.shape[1]
    return pl.pallas_call(
        functools.partial(_oproj_kernel, d=d, row_blocks=4 if tm % (4 * ROW_CHUNK) == 0 else 1),
        out_shape=(jax.ShapeDtypeStruct((t, d), F32), jax.ShapeDtypeStruct((t, d), BF16)),
        grid=(t // tm,),
        in_specs=[
            pl.BlockSpec((tm, nin), lambda i: (i, 0)),
            pl.BlockSpec((None, nin, d), lambda i: (slot, 0, 0), pipeline_mode=pl.Buffered(1)),
            pl.BlockSpec((tm, d), lambda i: (i, 0)),
            pl.BlockSpec((None, None, 1, mod.shape[-1]), lambda i: (layer, row_of_tile(i), 0, 0)),
            pl.BlockSpec((None, 4, d), lambda i: (layer, 0, 0)),
        ],
        out_specs=(pl.BlockSpec((tm, d), lambda i: (i, 0)), pl.BlockSpec((tm, d), lambda i: (i, 0))),
        compiler_params=_params("parallel"),
        name="out_proj",
    )(o, w_o, x, mod, g_norm)


def _ffn_kernel(h_ref, x_ref, mod_ref, g_ref, wg_ref, wu_ref, wd_ref, out_ref, *, d):
    j = pl.program_id(1)

    @pl.when(j == 0)
    def _():
        out_ref[...] = jnp.zeros_like(out_ref)

    h = h_ref[...]
    a = jnp.dot(h, wg_ref[...], preferred_element_type=F32)
    b = jnp.dot(h, wu_ref[...], preferred_element_type=F32)
    u = ((a * jax.nn.sigmoid(a)) * b).astype(BF16)
    out_ref[...] += jnp.dot(u, wd_ref[...], preferred_element_type=F32)

    @pl.when(j == pl.num_programs(1) - 1)
    def _():
        gate = mod_ref[...][:, 5 * d:6 * d]
        out_ref[...] = x_ref[...] + gate * (_rms(out_ref[...]) * g_ref[3:4, :])


def _chunk_columns(w, fc):
    n_layers, d, d_ff = w.shape
    return w.reshape(n_layers, d, d_ff // fc, fc).transpose(0, 2, 1, 3)


def _ffn(h, x, mod, g_norm, w_gate, w_up, w_down, layer, row_of_tile, *, tm):
    t, d = x.shape
    n_chunks, fc = w_gate.shape[1], w_gate.shape[3]
    return pl.pallas_call(
        functools.partial(_ffn_kernel, d=d),
        out_shape=jax.ShapeDtypeStruct((t, d), F32),
        grid=(t // tm, n_chunks),
        in_specs=[
            pl.BlockSpec((tm, d), lambda i, j: (i, 0)),
            pl.BlockSpec((tm, d), lambda i, j: (i, 0)),
            pl.BlockSpec((None, None, 1, mod.shape[-1]), lambda i, j: (layer, row_of_tile(i), 0, 0)),
            pl.BlockSpec((None, 4, d), lambda i, j: (layer, 0, 0)),
            pl.BlockSpec((None, None, d, fc), lambda i, j: (layer, j, 0, 0)),
            pl.BlockSpec((None, None, d, fc), lambda i, j: (layer, j, 0, 0)),
            pl.BlockSpec((None, fc, d), lambda i, j: (layer, j, 0)),
        ],
        out_specs=pl.BlockSpec((tm, d), lambda i, j: (i, 0)),
        compiler_params=_params("parallel", "arbitrary"),
        name="ffn",
    )(h, x, mod, g_norm, w_gate, w_up, w_down)


def _rope_tables(n_tokens):
    nf = HEAD_DIM // 4
    t = jnp.arange(n_tokens, dtype=jnp.int32)
    rows = (t // GRID_W).astype(F32)
    cols = (t % GRID_W).astype(F32)
    inv = 1.0 / (ROPE_THETA ** (jnp.arange(nf, dtype=F32) / nf))
    ar = rows[:, None] * inv
    ac = cols[:, None] * inv
    cos = jnp.concatenate([jnp.cos(ar), jnp.cos(ar), jnp.cos(ac), jnp.cos(ac)], axis=-1)
    sin = jnp.concatenate([-jnp.sin(ar), jnp.sin(ar), -jnp.sin(ac), jnp.sin(ac)], axis=-1)
    return cos, sin


def _diff_lambda_init(layer):
    return 0.8 - 0.6 * math.exp(-0.3 * layer)


def _largest_tile(n, cap):
    t = cap
    while n % t:
        t //= 2
    return t


def _run_path(x3, first_row, rows_per_batch, caches, mod, p, depth):
    batch, seq, d = x3.shape
    latent = caches is not None
    x = x3.reshape(batch * seq, d)
    tm = _largest_tile(seq if latent else batch * seq, 512)
    tiles_per_batch = seq // tm if latent else None

    if latent:
        def row_of_tile(i):
            return first_row + (i // tiles_per_batch) * rows_per_batch
    else:
        def row_of_tile(i):
            return first_row

    rope_tables = _rope_tables(seq) if latent else None
    kv_dtype = BF16 if latent else F32
    kept = ([], [], [], [], [], [])
    for layer in range(depth):
        kind, slot = layer % N_MIXERS, layer // N_MIXERS
        cache = None if not latent else (caches[2 * kind], caches[2 * kind + 1])
        common = dict(tm=tm, kv_dtype=kv_dtype, rope_tables=rope_tables)
        attn_shape = dict(kv_heads=1 if latent else KV_HEADS)
        if kind == 0:
            q, k, v = _qkv(x, mod, p['g_norm'], p['w_qkv_full'], layer, slot, row_of_tile,
                           nq=N_HEADS * HEAD_DIM, nk=KV_HEADS * HEAD_DIM,
                           gains=(p['g_q_full'], p['g_k_full']), **common)
            o = _gqa_attention(q, k, v, cache, slot, None, batch=batch, seq=seq, band=False,
                               tq=_largest_tile(seq, 512), block_rows=ATTN_BLOCK_ROWS, **attn_shape)
            w_o = p['w_o_full']
        elif kind == 1:
            q, k, v = _qkv(x, mod, p['g_norm'], p['w_qkv_win'], layer, slot, row_of_tile,
                           nq=N_HEADS * HEAD_DIM, nk=KV_HEADS * HEAD_DIM, **common)
            o = _gqa_attention(q, k, v, cache, slot, p['sink_win'][slot], batch=batch, seq=seq, band=latent,
                               tq=_largest_tile(seq, 256),
                               block_rows=2 * ATTN_BLOCK_ROWS if latent else ATTN_BLOCK_ROWS, **attn_shape)
            w_o = p['w_o_win']
        else:
            q, k, v = _qkv(x, mod, p['g_norm'], p['w_qkv_diff'], layer, slot, row_of_tile,
                           nq=DIFF_HEADS * 2 * DIFF_DIM, nk=DIFF_KV_HEADS * 2 * DIFF_DIM, **common)
            o = _diff_attention(q, k, v, cache, slot,
                                (p['lam_q1'], p['lam_k1'], p['lam_q2'], p['lam_k2']), p['g_subln_diff'],
                                batch=batch, seq=seq, tq=_largest_tile(seq, 256),
                                kv_heads=1,
                                lam_init=_diff_lambda_init(layer))
            w_o = p['w_o_diff']
        if not latent:
            kept[2 * kind].append(k)
            kept[2 * kind + 1].append(v)
        x, h = _oproj(o, w_o, x, mod, p['g_norm'], layer, slot, row_of_tile, tm=tm)
        x = _ffn(h, x, mod, p['g_norm'], p['w_gate'], p['w_up'], p['w_down'], layer, row_of_tile, tm=tm)
    return x.reshape(batch, seq, d), kept


def kernel(x_prompt, x_sample, c, cache_k_full, cache_v_full, cache_k_win, cache_v_win, cache_k_diff,
           cache_v_diff, c_ctx, w_ada, b_ada, g_norm, w_qkv_full, w_o_full, g_q_full, g_k_full, w_qkv_win,
           w_o_win, sink_win, w_qkv_diff, w_o_diff, lam_q1, lam_k1, lam_q2, lam_k2, g_subln_diff, w_gate,
           w_up, w_down):
    depth, d = w_ada.shape[0], w_ada.shape[1]
    batch, seq, _ = x_prompt.shape
    dec_batch, dec_seq, _ = x_sample.shape

    p = {
        'g_norm': g_norm,
        'w_qkv_full': w_qkv_full.astype(BF16), 'w_o_full': w_o_full.astype(BF16),
        'g_q_full': g_q_full, 'g_k_full': g_k_full,
        'w_qkv_win': w_qkv_win.astype(BF16), 'w_o_win': w_o_win.astype(BF16), 'sink_win': sink_win,
        'w_qkv_diff': w_qkv_diff.astype(BF16), 'w_o_diff': w_o_diff.astype(BF16),
        'lam_q1': lam_q1, 'lam_k1': lam_k1, 'lam_q2': lam_q2, 'lam_k2': lam_k2,
        'g_subln_diff': g_subln_diff,
        'w_gate': _chunk_columns(w_gate.astype(BF16), FFN_CHUNK),
        'w_up': _chunk_columns(w_up.astype(BF16), FFN_CHUNK), 'w_down': w_down.astype(BF16),
    }

    n_rows = -(-(1 + dec_batch) // SUBLANES) * SUBLANES
    cond = jnp.zeros((n_rows, d), F32).at[0].set(c_ctx).at[1:1 + dec_batch].set(c)
    mod = _modulation(cond, w_ada, b_ada)
    mod = mod.reshape(depth, n_rows, 1, 6 * d)

    y_prompt, kept = _run_path(x_prompt, 0, 0, None, mod, p, depth)

    def flat_cache(a):
        return a.reshape(a.shape[0], a.shape[1], a.shape[2], -1)

    caches = tuple(flat_cache(a) for a in (cache_k_full, cache_v_full, cache_k_win, cache_v_win,
                                           cache_k_diff, cache_v_diff))
    y_sample, _ = _run_path(x_sample, 1, 1, caches, mod, p, depth)

    def stack(parts, tail):
        return jnp.stack([a.reshape((batch, seq) + tail) for a in parts], axis=1)

    return (y_prompt, y_sample,
            stack(kept[0], (KV_HEADS, HEAD_DIM)), stack(kept[1], (KV_HEADS, HEAD_DIM)),
            stack(kept[2], (KV_HEADS, HEAD_DIM)), stack(kept[3], (KV_HEADS, HEAD_DIM)),
            stack(kept[4], (DIFF_KV_HEADS, 2, DIFF_DIM)), stack(kept[5], (DIFF_KV_HEADS, 2 * DIFF_DIM)))
```

```python
import functools
import math

import jax
import jax.numpy as jnp
from jax import lax
from jax.experimental import pallas as pl
from jax.experimental.pallas import tpu as pltpu

F32 = jnp.float32
BF16 = jnp.bfloat16

N_MIXERS = 3
N_HEADS = 16
KV_HEADS = 4
HEAD_DIM = 128
GROUP = N_HEADS // KV_HEADS
WINDOW = 128
GRID_W = 64
DIFF_HEADS = 8
DIFF_KV_HEADS = 2
DIFF_GROUP = DIFF_HEADS // DIFF_KV_HEADS
DIFF_DIM = 128
ROPE_THETA = 10000.0
EPS = 1e-6
NEG_INF = -1e30

LANES = 128
SUBLANES = 8
V7X_VMEM_BYTES = 64 * 1024 * 1024
VMEM_LIMIT = V7X_VMEM_BYTES - 8 * 1024 * 1024

NT_DIMS = (((1,), (1,)), ((), ()))
LOG2E = math.log2(math.e)
ROW_CHUNK = 2 * SUBLANES
ATTN_BLOCK_ROWS = 256
FFN_CHUNK = 512
FFN_EPILOGUE_BLOCKS = 2


def _params(*semantics):
    return pltpu.CompilerParams(dimension_semantics=semantics, vmem_limit_bytes=VMEM_LIMIT)


def _rms(x):
    return x * lax.rsqrt(jnp.mean(x * x, axis=-1, keepdims=True) + EPS)


def _mod_kernel(c_ref, w_ref, b_ref, o_ref):
    c = c_ref[...]
    a = (c * jax.nn.sigmoid(c)).astype(BF16)
    o_ref[...] = jnp.dot(a, w_ref[...].astype(BF16), preferred_element_type=F32) + b_ref[...]


def _modulation(cond, w_ada, b_ada):
    n_layers, d, n = w_ada.shape
    rows = cond.shape[0]
    tn = 1024
    return pl.pallas_call(
        _mod_kernel,
        out_shape=jax.ShapeDtypeStruct((n_layers, rows, n), F32),
        grid=(n_layers, n // tn),
        in_specs=[
            pl.BlockSpec((rows, d), lambda l, j: (0, 0)),
            pl.BlockSpec((None, d, tn), lambda l, j: (l, 0, j)),
            pl.BlockSpec((None, 1, tn), lambda l, j: (l, 0, j)),
        ],
        out_specs=pl.BlockSpec((None, rows, tn), lambda l, j: (l, 0, j)),
        compiler_params=_params("parallel", "parallel"),
        name="adaln_mod",
    )(cond, w_ada, b_ada.reshape(n_layers, 1, n))


def _qkv_kernel(*refs, d, nq, nk, qk_norm, rope, q_scale):
    it = iter(refs)
    x_ref, mod_ref, g_ref, w_ref = next(it), next(it), next(it), next(it)
    gq_ref = gk_ref = cos_ref = sin_ref = None
    if qk_norm:
        gq_ref, gk_ref = next(it), next(it)
    if rope:
        cos_ref, sin_ref = next(it), next(it)
    q_ref, k_ref, v_ref = next(it), next(it), next(it)

    mod = mod_ref[...]
    h = (_rms(x_ref[...]) * (g_ref[0:1, :] * (1.0 + mod[:, d:2 * d])) + mod[:, 0:d]).astype(BF16)

    def partner(a):
        lane = lax.broadcasted_iota(jnp.int32, a.shape, 1)
        return jnp.where((lane & 32) != 0, pltpu.roll(a, 32, 1), pltpu.roll(a, LANES - 32, 1))

    def lane_factors(gain_ref, scale):
        gain = None if gain_ref is None else jnp.broadcast_to(gain_ref[...] * scale, (SUBLANES, LANES))
        if not rope:
            return (scale if gain is None else gain[0:1, :]), None
        if gain is None:
            return cos_ref[...] * scale, sin_ref[...] * scale
        return cos_ref[...] * gain[0:1, :], sin_ref[...] * partner(gain)[0:1, :]

    def finish(z, normed, factors):
        c, s = factors
        if s is not None:
            t = z * c + partner(z) * s
        else:
            t = z if isinstance(c, float) and c == 1.0 else z * c
        if normed:
            t = t * lax.rsqrt(jnp.mean(z * z, axis=-1, keepdims=True) + EPS)
        return t

    q_factors = lane_factors(gq_ref, q_scale)
    k_factors = lane_factors(gk_ref, 1.0)
    chunk = 512
    for c0 in range(0, nq + nk, chunk):
        acc = jnp.dot(h, w_ref[:, c0:c0 + chunk], preferred_element_type=F32)
        for j in range(chunk // LANES):
            col = c0 + j * LANES
            z = acc[:, j * LANES:(j + 1) * LANES]
            if col < nq:
                q_ref[:, col:col + LANES] = finish(z, qk_norm, q_factors).astype(q_ref.dtype)
            else:
                k_ref[:, col - nq:col - nq + LANES] = finish(z, qk_norm, k_factors).astype(k_ref.dtype)
    v_ref[...] = jnp.dot(h, w_ref[:, nq + nk:], preferred_element_type=F32).astype(v_ref.dtype)


def _qkv(x, mod, g_norm, w_qkv, layer, slot, row_of_tile, *, tm, nq, nk, kv_dtype,
         gains=None, rope_tables=None):
    t, d = x.shape
    n_all = w_qkv.shape[-1]
    nv = n_all - nq - nk
    operands = [x, mod, g_norm, w_qkv]
    in_specs = [
        pl.BlockSpec((tm, d), lambda i: (i, 0)),
        pl.BlockSpec((None, None, 1, mod.shape[-1]), lambda i: (layer, row_of_tile(i), 0, 0)),
        pl.BlockSpec((None, 4, d), lambda i: (layer, 0, 0)),
        pl.BlockSpec((None, d, n_all), lambda i: (slot, 0, 0), pipeline_mode=pl.Buffered(1)),
    ]
    if gains is not None:
        for g in gains:
            operands.append(g.reshape(g.shape[0], 1, LANES))
            in_specs.append(pl.BlockSpec((None, 1, LANES), lambda i: (slot, 0, 0)))
    if rope_tables is not None:
        tiles_per_seq = rope_tables[0].shape[0] // tm
        for tab in rope_tables:
            operands.append(tab)
            in_specs.append(pl.BlockSpec((tm, LANES), lambda i: (i % tiles_per_seq, 0)))
    kern = functools.partial(_qkv_kernel, d=d, nq=nq, nk=nk, qk_norm=gains is not None,
                             rope=rope_tables is not None, q_scale=HEAD_DIM ** -0.5 * LOG2E)
    return pl.pallas_call(
        kern,
        out_shape=(jax.ShapeDtypeStruct((t, nq), BF16),
                   jax.ShapeDtypeStruct((t, nk), kv_dtype),
                   jax.ShapeDtypeStruct((t, nv), kv_dtype)),
        grid=(t // tm,),
        in_specs=in_specs,
        out_specs=(pl.BlockSpec((tm, nq), lambda i: (i, 0)),
                   pl.BlockSpec((tm, nk), lambda i: (i, 0)),
                   pl.BlockSpec((tm, nv), lambda i: (i, 0))),
        compiler_params=_params("parallel"),
        name="qkv_proj",
    )(*operands)


def _gqa_attn_kernel(*refs, tq, kv_heads, block_rows, n_new, n_ctx, band, has_sink):
    it = iter(refs)
    sink_ref = next(it) if has_sink else None
    q_ref, k_ref, v_ref = next(it), next(it), next(it)
    kc_ref = vc_ref = None
    if n_ctx:
        kc_ref, vc_ref = next(it), next(it)
    o_ref = next(it)
    s_scr, p_scr = next(it), next(it)
    bias_scr = next(it) if band else None

    band_width = tq + 2 * WINDOW
    q0 = pl.program_id(2) * tq
    if band and n_new > band_width:
        start = pl.multiple_of(jnp.clip(q0 - WINDOW, 0, n_new - band_width), LANES)
        k = k_ref[pl.ds(start, band_width), :]
        v = v_ref[pl.ds(start, band_width), :]
    else:
        start = 0
        k = k_ref[...]
        v = v_ref[...]
    k = k.astype(BF16)
    v = v.astype(BF16)
    nk = k.shape[0]
    if n_ctx:
        k = jnp.concatenate([k, kc_ref[...].astype(BF16)], axis=0)
        v = jnp.concatenate([v, vc_ref[...].astype(BF16)], axis=0)
    if band:
        qpos = q0 + lax.broadcasted_iota(jnp.int32, (tq, nk), 0)
        kpos = start + lax.broadcasted_iota(jnp.int32, (tq, nk), 1)
        bias_scr[...] = jnp.where(jnp.abs(qpos - kpos) <= WINDOW, 0.0, NEG_INF).astype(F32)
    ones = jnp.ones((k.shape[0], HEAD_DIM), BF16)

    if block_rows <= tq:
        blocks = [((h,), r0, block_rows) for h in range(GROUP) for r0 in range(0, tq, block_rows)]
    else:
        stacked = block_rows // tq
        blocks = [(tuple(range(h0, h0 + stacked)), 0, tq) for h0 in range(0, GROUP, stacked)]

    base = 0
    for kvh in range(kv_heads):
        k_h = k[:, kvh * HEAD_DIM:(kvh + 1) * HEAD_DIM]
        v_ext = jnp.concatenate([v[:, kvh * HEAD_DIM:(kvh + 1) * HEAD_DIM], ones], axis=1)
        for heads, row0, n in blocks:
            rows = len(heads) * n
            cols = [(kvh * GROUP + h) * HEAD_DIM for h in heads]
            qb = jnp.concatenate([q_ref[row0:row0 + n, c:c + HEAD_DIM] for c in cols], axis=0)
            s_scr[base:base + rows, :] = lax.dot_general(qb, k_h, NT_DIMS, preferred_element_type=F32)
            sink_terms = []
            for r in range(0, rows, ROW_CHUNK):
                rs = slice(base + r, base + r + ROW_CHUNK)
                if band:
                    qr = row0 + r % n
                    parts = [s_scr[rs, 0:nk] + bias_scr[qr:qr + ROW_CHUNK, :]]
                    if n_ctx:
                        parts.append(s_scr[rs, nk:])
                else:
                    parts = [s_scr[rs, :]]
                m = functools.reduce(jnp.maximum, [jnp.max(a, axis=-1, keepdims=True) for a in parts])
                if has_sink:
                    head = (pl.program_id(1) * kv_heads + kvh) * GROUP + heads[r // n]
                    sk = sink_ref[head] * LOG2E
                    m = jnp.maximum(m, sk)
                    sink_terms.append(jnp.exp2(sk - m))
                col = 0
                for a in parts:
                    p_scr[rs, col:col + a.shape[1]] = jnp.exp2(a - m).astype(BF16)
                    col += a.shape[1]
            acc = jnp.dot(p_scr[base:base + rows, :], v_ext, preferred_element_type=F32)
            denom = acc[:, HEAD_DIM:]
            if has_sink:
                denom = denom + jnp.concatenate(sink_terms, axis=0)
            o = acc[:, :HEAD_DIM] / denom
            for j, c in enumerate(cols):
                o_ref[row0:row0 + n, c:c + HEAD_DIM] = o[j * n:(j + 1) * n, :].astype(o_ref.dtype)
            base += rows


def _gqa_attention(q, k, v, cache, slot, sink, *, batch, seq, tq, kv_heads, block_rows, band):
    nq_tiles = seq // tq
    n_ctx = 0 if cache is None else cache[0].shape[2]
    qw, kw = kv_heads * GROUP * HEAD_DIM, kv_heads * HEAD_DIM
    operands, in_specs = [], []
    if sink is not None:
        operands.append(sink)
        in_specs.append(pl.BlockSpec(memory_space=pltpu.SMEM))
    operands += [q, k, v]
    in_specs += [
        pl.BlockSpec((tq, qw), lambda b, h, i: (b * nq_tiles + i, h)),
        pl.BlockSpec((seq, kw), lambda b, h, i: (b, h)),
        pl.BlockSpec((seq, kw), lambda b, h, i: (b, h)),
    ]
    if cache is not None:
        operands += list(cache)
        in_specs += [pl.BlockSpec((None, None, n_ctx, kw), lambda b, h, i: (b, slot, 0, h))] * 2
    kern = functools.partial(_gqa_attn_kernel, tq=tq, kv_heads=kv_heads, block_rows=block_rows, n_new=seq,
                             n_ctx=n_ctx, band=band, has_sink=sink is not None)
    n_new_keys = min(seq, tq + 2 * WINDOW) if band else seq
    n_keys = n_new_keys + n_ctx
    scratch = [pltpu.VMEM((kv_heads * GROUP * tq, n_keys), F32), pltpu.VMEM((kv_heads * GROUP * tq, n_keys), BF16)]
    if band:
        scratch.append(pltpu.VMEM((tq, n_new_keys), F32))
    return pl.pallas_call(
        kern,
        out_shape=jax.ShapeDtypeStruct(q.shape, BF16),
        grid=(batch, KV_HEADS // kv_heads, nq_tiles),
        in_specs=in_specs,
        out_specs=pl.BlockSpec((tq, qw), lambda b, h, i: (b * nq_tiles + i, h)),
        scratch_shapes=scratch,
        compiler_params=_params("parallel", "parallel", "parallel"),
        name="gqa_attn",
    )(*operands)


def _diff_attn_kernel(*refs, tq, kv_heads, block_heads, n_ctx, lam_init):
    it = iter(refs)
    lq1_ref, lk1_ref, lq2_ref, lk2_ref, gs_ref = next(it), next(it), next(it), next(it), next(it)
    q_ref, k_ref, v_ref = next(it), next(it), next(it)
    kc_ref = vc_ref = None
    if n_ctx:
        kc_ref, vc_ref = next(it), next(it)
    o_ref = next(it)
    s_scrs = (next(it), next(it))
    p_scr = next(it)

    lam = (jnp.exp(jnp.sum(lq1_ref[...] * lk1_ref[...], axis=-1, keepdims=True))
           - jnp.exp(jnp.sum(lq2_ref[...] * lk2_ref[...], axis=-1, keepdims=True)) + lam_init)
    k = k_ref[...].astype(BF16)
    v = v_ref[...].astype(BF16)
    if n_ctx:
        k = jnp.concatenate([k, kc_ref[...].astype(BF16)], axis=0)
        v = jnp.concatenate([v, vc_ref[...].astype(BF16)], axis=0)

    width = 2 * DIFF_DIM
    rows = block_heads * tq
    base = 0
    for kvh in range(kv_heads):
        k_h = k[:, kvh * width:(kvh + 1) * width]
        v_h = v[:, kvh * width:(kvh + 1) * width]
        for h0 in range(0, DIFF_GROUP, block_heads):
            heads = [kvh * DIFF_GROUP + g for g in range(h0, h0 + block_heads)]
            for half in range(2):
                qb = jnp.concatenate(
                    [q_ref[:, (2 * g + half) * DIFF_DIM:(2 * g + half + 1) * DIFF_DIM] for g in heads], axis=0)
                s_scrs[half][base:base + rows, :] = lax.dot_general(
                    qb, k_h[:, half * DIFF_DIM:(half + 1) * DIFF_DIM], NT_DIMS, preferred_element_type=F32)
            for r in range(0, rows, ROW_CHUNK):
                rs = slice(base + r, base + r + ROW_CHUNK)
                s0 = s_scrs[0][rs, :]
                e0 = jnp.exp2(s0 - jnp.max(s0, axis=-1, keepdims=True))
                r0 = 1.0 / jnp.sum(e0, axis=-1, keepdims=True)
                s1 = s_scrs[1][rs, :]
                e1 = jnp.exp2(s1 - jnp.max(s1, axis=-1, keepdims=True))
                r1 = lam / jnp.sum(e1, axis=-1, keepdims=True)
                p_scr[rs, :] = (e0 * r0 - e1 * r1).astype(BF16)
            acc = jnp.dot(p_scr[base:base + rows, :], v_h, preferred_element_type=F32)
            y = (_rms(acc) * gs_ref[...]) * (1.0 - lam_init)
            for j, g in enumerate(heads):
                o_ref[:, g * width:(g + 1) * width] = y[j * tq:(j + 1) * tq, :].astype(o_ref.dtype)
            base += rows


def _diff_attention(q, k, v, cache, slot, lam_vecs, g_subln, *, batch, seq, tq, kv_heads, lam_init):
    nq_tiles = seq // tq
    n_ctx = 0 if cache is None else cache[0].shape[2]
    qw = kv_heads * DIFF_GROUP * 2 * DIFF_DIM
    kw = kv_heads * 2 * DIFF_DIM
    operands = [a.reshape(a.shape[0], 1, DIFF_DIM) for a in lam_vecs]
    in_specs = [pl.BlockSpec((None, 1, DIFF_DIM), lambda b, h, i: (slot, 0, 0))] * 4
    operands.append(g_subln.reshape(g_subln.shape[0], 1, 2 * DIFF_DIM))
    in_specs.append(pl.BlockSpec((None, 1, 2 * DIFF_DIM), lambda b, h, i: (slot, 0, 0)))
    operands += [q, k, v]
    in_specs += [
        pl.BlockSpec((tq, qw), lambda b, h, i: (b * nq_tiles + i, h)),
        pl.BlockSpec((seq, kw), lambda b, h, i: (b, h)),
        pl.BlockSpec((seq, kw), lambda b, h, i: (b, h)),
    ]
    if cache is not None:
        operands += list(cache)
        in_specs += [pl.BlockSpec((None, None, n_ctx, kw), lambda b, h, i: (b, slot, 0, h))] * 2
    kern = functools.partial(_diff_attn_kernel, tq=tq, kv_heads=kv_heads, block_heads=2, n_ctx=n_ctx,
                             lam_init=lam_init)
    score_shape = (kv_heads * DIFF_GROUP * tq, seq + n_ctx)
    return pl.pallas_call(
        kern,
        out_shape=jax.ShapeDtypeStruct(q.shape, BF16),
        grid=(batch, DIFF_KV_HEADS // kv_heads, nq_tiles),
        in_specs=in_specs,
        out_specs=pl.BlockSpec((tq, qw), lambda b, h, i: (b * nq_tiles + i, h)),
        scratch_shapes=[pltpu.VMEM(score_shape, F32), pltpu.VMEM(score_shape, F32),
                        pltpu.VMEM(score_shape, BF16)],
        compiler_params=_params("parallel", "parallel", "parallel"),
        name="diff_attn",
    )(*operands)


def _oproj_kernel(o_ref, w_ref, x_ref, mod_ref, g_ref, out_ref, h_ref, *, d, row_blocks):
    mod = mod_ref[...]
    gain1 = mod[:, 2 * d:3 * d] * g_ref[1:2, :]
    gain2 = g_ref[2:3, :] * (1.0 + mod[:, 4 * d:5 * d])
    shift2 = mod[:, 3 * d:4 * d]
    rows = o_ref.shape[0] // row_blocks
    for r in range(row_blocks):
        rs = slice(r * rows, (r + 1) * rows)
        m = jnp.dot(o_ref[rs, :], w_ref[...], preferred_element_type=F32)
        x_new = x_ref[rs, :] + _rms(m) * gain1
        out_ref[rs, :] = x_new
        h_ref[rs, :] = (_rms(x_new) * gain2 + shift2).astype(BF16)


def _oproj(o, w_o, x, mod, g_norm, layer, slot, row_of_tile, *, tm):
    t, d = x.shape
    nin = o.shape[1]
    return pl.pallas_call(
        functools.partial(_oproj_kernel, d=d, row_blocks=4 if tm % (4 * ROW_CHUNK) == 0 else 1),
        out_shape=(jax.ShapeDtypeStruct((t, d), F32), jax.ShapeDtypeStruct((t, d), BF16)),
        grid=(t // tm,),
        in_specs=[
            pl.BlockSpec((tm, nin), lambda i: (i, 0)),
            pl.BlockSpec((None, nin, d), lambda i: (slot, 0, 0), pipeline_mode=pl.Buffered(1)),
            pl.BlockSpec((tm, d), lambda i: (i, 0)),
            pl.BlockSpec((None, None, 1, mod.shape[-1]), lambda i: (layer, row_of_tile(i), 0, 0)),
            pl.BlockSpec((None, 4, d), lambda i: (layer, 0, 0)),
        ],
        out_specs=(pl.BlockSpec((tm, d), lambda i: (i, 0)), pl.BlockSpec((tm, d), lambda i: (i, 0))),
        compiler_params=_params("parallel"),
        name="out_proj",
    )(o, w_o, x, mod, g_norm)


def _ffn_kernel(h_ref, x_ref, mod_ref, g_ref, wg_ref, wu_ref, wd_ref, out_ref, *, d):
    j = pl.program_id(1)
    last = pl.num_programs(1) - 1

    def partial_sum(rows):
        h = h_ref[rows, :]
        a = jnp.dot(h, wg_ref[...], preferred_element_type=F32)
        b = jnp.dot(h, wu_ref[...], preferred_element_type=F32)
        u = ((a * jax.nn.sigmoid(a)) * b).astype(BF16)
        return jnp.dot(u, wd_ref[...], preferred_element_type=F32)

    @pl.when(j == 0)
    def _():
        out_ref[...] = partial_sum(slice(None))

    @pl.when(jnp.logical_and(j > 0, j < last))
    def _():
        out_ref[...] += partial_sum(slice(None))

    @pl.when(j == last)
    def _():
        gain = mod_ref[...][:, 5 * d:6 * d] * g_ref[3:4, :]
        rows = out_ref.shape[0] // FFN_EPILOGUE_BLOCKS
        for r in range(FFN_EPILOGUE_BLOCKS):
            rs = slice(r * rows, (r + 1) * rows)
            acc = out_ref[rs, :] + partial_sum(rs)
            out_ref[rs, :] = x_ref[rs, :] + _rms(acc) * gain


def _ffn(h, x, mod, g_norm, w_gate, w_up, w_down, layer, row_of_tile, *, tm, fc):
    t, d = x.shape
    d_ff = w_gate.shape[-1]
    return pl.pallas_call(
        functools.partial(_ffn_kernel, d=d),
        out_shape=jax.ShapeDtypeStruct((t, d), F32),
        grid=(t // tm, d_ff // fc),
        in_specs=[
            pl.BlockSpec((tm, d), lambda i, j: (i, 0)),
            pl.BlockSpec((tm, d), lambda i, j: (i, 0)),
            pl.BlockSpec((None, None, 1, mod.shape[-1]), lambda i, j: (layer, row_of_tile(i), 0, 0)),
            pl.BlockSpec((None, 4, d), lambda i, j: (layer, 0, 0)),
            pl.BlockSpec((None, d, fc), lambda i, j: (layer, 0, j)),
            pl.BlockSpec((None, d, fc), lambda i, j: (layer, 0, j)),
            pl.BlockSpec((None, fc, d), lambda i, j: (layer, j, 0)),
        ],
        out_specs=pl.BlockSpec((tm, d), lambda i, j: (i, 0)),
        compiler_params=_params("parallel", "arbitrary"),
        name="ffn",
    )(h, x, mod, g_norm, w_gate, w_up, w_down)


def _rope_tables(n_tokens):
    nf = HEAD_DIM // 4
    t = jnp.arange(n_tokens, dtype=jnp.int32)
    rows = (t // GRID_W).astype(F32)
    cols = (t % GRID_W).astype(F32)
    inv = 1.0 / (ROPE_THETA ** (jnp.arange(nf, dtype=F32) / nf))
    ar = rows[:, None] * inv
    ac = cols[:, None] * inv
    cos = jnp.concatenate([jnp.cos(ar), jnp.cos(ar), jnp.cos(ac), jnp.cos(ac)], axis=-1)
    sin = jnp.concatenate([-jnp.sin(ar), jnp.sin(ar), -jnp.sin(ac), jnp.sin(ac)], axis=-1)
    return cos, sin


def _diff_lambda_init(layer):
    return 0.8 - 0.6 * math.exp(-0.3 * layer)


def _largest_tile(n, cap):
    t = cap
    while n % t:
        t //= 2
    return t


def _run_path(x3, first_row, rows_per_batch, caches, mod, p, depth):
    batch, seq, d = x3.shape
    latent = caches is not None
    x = x3.reshape(batch * seq, d)
    tm = _largest_tile(seq if latent else batch * seq, 512)
    tiles_per_batch = seq // tm if latent else None

    if latent:
        def row_of_tile(i):
            return first_row + (i // tiles_per_batch) * rows_per_batch
    else:
        def row_of_tile(i):
            return first_row

    rope_tables = _rope_tables(seq) if latent else None
    kv_dtype = BF16 if latent else F32
    kept = ([], [], [], [], [], [])
    for layer in range(depth):
        kind, slot = layer % N_MIXERS, layer // N_MIXERS
        cache = None if not latent else (caches[2 * kind], caches[2 * kind + 1])
        common = dict(tm=tm, kv_dtype=kv_dtype, rope_tables=rope_tables)
        attn_shape = dict(kv_heads=1 if latent else KV_HEADS)
        if kind == 0:
            q, k, v = _qkv(x, mod, p['g_norm'], p['w_qkv_full'], layer, slot, row_of_tile,
                           nq=N_HEADS * HEAD_DIM, nk=KV_HEADS * HEAD_DIM,
                           gains=(p['g_q_full'], p['g_k_full']), **common)
            o = _gqa_attention(q, k, v, cache, slot, None, batch=batch, seq=seq, band=False,
                               tq=_largest_tile(seq, 512), block_rows=ATTN_BLOCK_ROWS, **attn_shape)
            w_o = p['w_o_full']
        elif kind == 1:
            q, k, v = _qkv(x, mod, p['g_norm'], p['w_qkv_win'], layer, slot, row_of_tile,
                           nq=N_HEADS * HEAD_DIM, nk=KV_HEADS * HEAD_DIM, **common)
            o = _gqa_attention(q, k, v, cache, slot, p['sink_win'][slot], batch=batch, seq=seq, band=latent,
                               tq=_largest_tile(seq, 256),
                               block_rows=2 * ATTN_BLOCK_ROWS if latent else ATTN_BLOCK_ROWS, **attn_shape)
            w_o = p['w_o_win']
        else:
            q, k, v = _qkv(x, mod, p['g_norm'], p['w_qkv_diff'], layer, slot, row_of_tile,
                           nq=DIFF_HEADS * 2 * DIFF_DIM, nk=DIFF_KV_HEADS * 2 * DIFF_DIM, **common)
            o = _diff_attention(q, k, v, cache, slot,
                                (p['lam_q1'], p['lam_k1'], p['lam_q2'], p['lam_k2']), p['g_subln_diff'],
                                batch=batch, seq=seq, tq=_largest_tile(seq, 256),
                                kv_heads=1,
                                lam_init=_diff_lambda_init(layer))
            w_o = p['w_o_diff']
        if not latent:
            kept[2 * kind].append(k)
            kept[2 * kind + 1].append(v)
        x, h = _oproj(o, w_o, x, mod, p['g_norm'], layer, slot, row_of_tile, tm=tm)
        x = _ffn(h, x, mod, p['g_norm'], p['w_gate'], p['w_up'], p['w_down'], layer, row_of_tile,
                 tm=tm, fc=FFN_CHUNK)
    return x.reshape(batch, seq, d), kept


def kernel(x_prompt, x_sample, c, cache_k_full, cache_v_full, cache_k_win, cache_v_win, cache_k_diff,
           cache_v_diff, c_ctx, w_ada, b_ada, g_norm, w_qkv_full, w_o_full, g_q_full, g_k_full, w_qkv_win,
           w_o_win, sink_win, w_qkv_diff, w_o_diff, lam_q1, lam_k1, lam_q2, lam_k2, g_subln_diff, w_gate,
           w_up, w_down):
    depth, d = w_ada.shape[0], w_ada.shape[1]
    batch, seq, _ = x_prompt.shape
    dec_batch, dec_seq, _ = x_sample.shape

    p = {
        'g_norm': g_norm,
        'w_qkv_full': w_qkv_full.astype(BF16), 'w_o_full': w_o_full.astype(BF16),
        'g_q_full': g_q_full, 'g_k_full': g_k_full,
        'w_qkv_win': w_qkv_win.astype(BF16), 'w_o_win': w_o_win.astype(BF16), 'sink_win': sink_win,
        'w_qkv_diff': w_qkv_diff.astype(BF16), 'w_o_diff': w_o_diff.astype(BF16),
        'lam_q1': lam_q1, 'lam_k1': lam_k1, 'lam_q2': lam_q2, 'lam_k2': lam_k2,
        'g_subln_diff': g_subln_diff,
        'w_gate': w_gate.astype(BF16), 'w_up': w_up.astype(BF16), 'w_down': w_down.astype(BF16),
    }

    n_rows = -(-(1 + dec_batch) // SUBLANES) * SUBLANES
    cond = jnp.zeros((n_rows, d), F32).at[0].set(c_ctx).at[1:1 + dec_batch].set(c)
    mod = _modulation(cond, w_ada, b_ada)
    mod = mod.reshape(depth, n_rows, 1, 6 * d)

    y_prompt, kept = _run_path(x_prompt, 0, 0, None, mod, p, depth)

    def flat_cache(a):
        return a.reshape(a.shape[0], a.shape[1], a.shape[2], -1)

    caches = tuple(flat_cache(a) for a in (cache_k_full, cache_v_full, cache_k_win, cache_v_win,
                                           cache_k_diff, cache_v_diff))
    y_sample, _ = _run_path(x_sample, 1, 1, caches, mod, p, depth)

    def stack(parts, tail):
        return jnp.stack([a.reshape((batch, seq) + tail) for a in parts], axis=1)

    return (y_prompt, y_sample,
            stack(kept[0], (KV_HEADS, HEAD_DIM)), stack(kept[1], (KV_HEADS, HEAD_DIM)),
            stack(kept[2], (KV_HEADS, HEAD_DIM)), stack(kept[3], (KV_HEADS, HEAD_DIM)),
            stack(kept[4], (DIFF_KV_HEADS, 2, DIFF_DIM)), stack(kept[5], (DIFF_KV_HEADS, 2 * DIFF_DIM)))
```

```python
import functools
import math

import jax
import jax.numpy as jnp
from jax import lax
from jax.experimental import pallas as pl
from jax.experimental.pallas import tpu as pltpu

F32 = jnp.float32
BF16 = jnp.bfloat16

N_MIXERS = 3
N_HEADS = 16
KV_HEADS = 4
HEAD_DIM = 128
GROUP = N_HEADS // KV_HEADS
WINDOW = 128
GRID_W = 64
DIFF_HEADS = 8
DIFF_KV_HEADS = 2
DIFF_GROUP = DIFF_HEADS // DIFF_KV_HEADS
DIFF_DIM = 128
ROPE_THETA = 10000.0
EPS = 1e-6
NEG_INF = -1e30

LANES = 128
SUBLANES = 8
V7X_VMEM_BYTES = 64 * 1024 * 1024
VMEM_LIMIT = V7X_VMEM_BYTES - 8 * 1024 * 1024

NT_DIMS = (((1,), (1,)), ((), ()))
LOG2E = math.log2(math.e)
ROW_CHUNK = 2 * SUBLANES
ATTN_BLOCK_ROWS = 256
FFN_CHUNK = 512
FFN_EPILOGUE_BLOCKS = 2


def _params(*semantics):
    return pltpu.CompilerParams(dimension_semantics=semantics, vmem_limit_bytes=VMEM_LIMIT)


def _rms(x):
    return x * lax.rsqrt(jnp.mean(x * x, axis=-1, keepdims=True) + EPS)


def _mod_kernel(c_ref, w_ref, b_ref, o_ref):
    c = c_ref[...]
    a = (c * jax.nn.sigmoid(c)).astype(BF16)
    o_ref[...] = jnp.dot(a, w_ref[...].astype(BF16), preferred_element_type=F32) + b_ref[...]


def _modulation(cond, w_ada, b_ada):
    n_layers, d, n = w_ada.shape
    rows = cond.shape[0]
    tn = 1024
    return pl.pallas_call(
        _mod_kernel,
        out_shape=jax.ShapeDtypeStruct((n_layers, rows, n), F32),
        grid=(n_layers, n // tn),
        in_specs=[
            pl.BlockSpec((rows, d), lambda l, j: (0, 0)),
            pl.BlockSpec((None, d, tn), lambda l, j: (l, 0, j)),
            pl.BlockSpec((None, 1, tn), lambda l, j: (l, 0, j)),
        ],
        out_specs=pl.BlockSpec((None, rows, tn), lambda l, j: (l, 0, j)),
        compiler_params=_params("parallel", "parallel"),
        name="adaln_mod",
    )(cond, w_ada, b_ada.reshape(n_layers, 1, n))


def _qkv_kernel(*refs, d, nq, nk, qk_norm, rope, q_scale, row_of_tile):
    it = iter(refs)
    x_ref, mod_ref, g_ref, w_ref = next(it), next(it), next(it), next(it)
    gq_ref = gk_ref = cos_ref = sin_ref = None
    if qk_norm:
        gq_ref, gk_ref = next(it), next(it)
    if rope:
        cos_ref, sin_ref = next(it), next(it)
    q_ref, k_ref, v_ref = next(it), next(it), next(it)

    mod = mod_ref[pl.ds(row_of_tile(pl.program_id(0)), 1), :]
    h = (_rms(x_ref[...]) * (g_ref[0:1, :] * (1.0 + mod[:, d:2 * d])) + mod[:, 0:d]).astype(BF16)

    def partner(a):
        lane = lax.broadcasted_iota(jnp.int32, a.shape, 1)
        return jnp.where((lane & 32) != 0, pltpu.roll(a, 32, 1), pltpu.roll(a, LANES - 32, 1))

    def lane_factors(gain_ref, scale):
        gain = None if gain_ref is None else jnp.broadcast_to(gain_ref[...] * scale, (SUBLANES, LANES))
        if not rope:
            return (scale if gain is None else gain[0:1, :]), None
        if gain is None:
            return cos_ref[...] * scale, sin_ref[...] * scale
        return cos_ref[...] * gain[0:1, :], sin_ref[...] * partner(gain)[0:1, :]

    def finish(z, normed, factors):
        c, s = factors
        if s is not None:
            t = z * c + partner(z) * s
        else:
            t = z if isinstance(c, float) and c == 1.0 else z * c
        if normed:
            t = t * lax.rsqrt(jnp.mean(z * z, axis=-1, keepdims=True) + EPS)
        return t

    q_factors = lane_factors(gq_ref, q_scale)
    k_factors = lane_factors(gk_ref, 1.0)
    chunk = 512
    for c0 in range(0, nq + nk, chunk):
        acc = jnp.dot(h, w_ref[:, c0:c0 + chunk], preferred_element_type=F32)
        for j in range(chunk // LANES):
            col = c0 + j * LANES
            z = acc[:, j * LANES:(j + 1) * LANES]
            if col < nq:
                q_ref[:, col:col + LANES] = finish(z, qk_norm, q_factors).astype(q_ref.dtype)
            else:
                k_ref[:, col - nq:col - nq + LANES] = finish(z, qk_norm, k_factors).astype(k_ref.dtype)
    v_ref[...] = jnp.dot(h, w_ref[:, nq + nk:], preferred_element_type=F32).astype(v_ref.dtype)


def _qkv(x, mod, g_norm, w_qkv, layer, slot, row_of_tile, *, tm, nq, nk, kv_dtype,
         gains=None, rope_tables=None):
    t, d = x.shape
    n_all = w_qkv.shape[-1]
    nv = n_all - nq - nk
    operands = [x, mod, g_norm, w_qkv]
    in_specs = [
        pl.BlockSpec((tm, d), lambda i: (i, 0)),
        pl.BlockSpec((None,) + mod.shape[1:], lambda i: (layer, 0, 0)),
        pl.BlockSpec((None, 4, d), lambda i: (layer, 0, 0)),
        pl.BlockSpec((None, d, n_all), lambda i: (slot, 0, 0), pipeline_mode=pl.Buffered(1)),
    ]
    if gains is not None:
        for g in gains:
            operands.append(g.reshape(g.shape[0], 1, LANES))
            in_specs.append(pl.BlockSpec((None, 1, LANES), lambda i: (slot, 0, 0)))
    if rope_tables is not None:
        tiles_per_seq = rope_tables[0].shape[0] // tm
        for tab in rope_tables:
            operands.append(tab)
            in_specs.append(pl.BlockSpec((tm, LANES), lambda i: (i % tiles_per_seq, 0)))
    kern = functools.partial(_qkv_kernel, d=d, nq=nq, nk=nk, qk_norm=gains is not None,
                             rope=rope_tables is not None, q_scale=HEAD_DIM ** -0.5 * LOG2E,
                             row_of_tile=row_of_tile)
    return pl.pallas_call(
        kern,
        out_shape=(jax.ShapeDtypeStruct((t, nq), BF16),
                   jax.ShapeDtypeStruct((t, nk), kv_dtype),
                   jax.ShapeDtypeStruct((t, nv), kv_dtype)),
        grid=(t // tm,),
        in_specs=in_specs,
        out_specs=(pl.BlockSpec((tm, nq), lambda i: (i, 0)),
                   pl.BlockSpec((tm, nk), lambda i: (i, 0)),
                   pl.BlockSpec((tm, nv), lambda i: (i, 0))),
        compiler_params=_params("parallel"),
        name="qkv_proj",
    )(*operands)


def _gqa_attn_kernel(*refs, tq, kv_heads, block_rows, n_new, n_ctx, band, has_sink):
    it = iter(refs)
    sink_ref = next(it) if has_sink else None
    q_ref, k_ref, v_ref = next(it), next(it), next(it)
    kc_ref = vc_ref = None
    if n_ctx:
        kc_ref, vc_ref = next(it), next(it)
    o_ref = next(it)
    s_scr, p_scr = next(it), next(it)
    bias_scr = next(it) if band else None
    if n_ctx:
        kc_scr, vc_scr = next(it), next(it)
        for head in range(KV_HEADS):
            @pl.when(jnp.logical_and(pl.program_id(1) == head, pl.program_id(2) == 0))
            def _():
                kc_scr[...] = kc_ref[pl.ds(head, n_ctx, stride=KV_HEADS), :].astype(BF16)
                vc_scr[...] = vc_ref[pl.ds(head, n_ctx, stride=KV_HEADS), :].astype(BF16)

    band_width = tq + 2 * WINDOW
    q0 = pl.program_id(2) * tq
    if band and n_new > band_width:
        start = pl.multiple_of(jnp.clip(q0 - WINDOW, 0, n_new - band_width), LANES)
        k = k_ref[pl.ds(start, band_width), :]
        v = v_ref[pl.ds(start, band_width), :]
    else:
        start = 0
        k = k_ref[...]
        v = v_ref[...]
    k = k.astype(BF16)
    v = v.astype(BF16)
    nk = k.shape[0]
    if n_ctx:
        k = jnp.concatenate([k, kc_scr[...]], axis=0)
        v = jnp.concatenate([v, vc_scr[...]], axis=0)
    if band:
        qpos = q0 + lax.broadcasted_iota(jnp.int32, (tq, nk), 0)
        kpos = start + lax.broadcasted_iota(jnp.int32, (tq, nk), 1)
        bias_scr[...] = jnp.where(jnp.abs(qpos - kpos) <= WINDOW, 0.0, NEG_INF).astype(F32)
    ones = jnp.ones((k.shape[0], HEAD_DIM), BF16)

    if block_rows <= tq:
        blocks = [((h,), r0, block_rows) for h in range(GROUP) for r0 in range(0, tq, block_rows)]
    else:
        stacked = block_rows // tq
        blocks = [(tuple(range(h0, h0 + stacked)), 0, tq) for h0 in range(0, GROUP, stacked)]

    base = 0
    for kvh in range(kv_heads):
        k_h = k[:, kvh * HEAD_DIM:(kvh + 1) * HEAD_DIM]
        v_ext = jnp.concatenate([v[:, kvh * HEAD_DIM:(kvh + 1) * HEAD_DIM], ones], axis=1)
        for heads, row0, n in blocks:
            rows = len(heads) * n
            cols = [(kvh * GROUP + h) * HEAD_DIM for h in heads]
            qb = jnp.concatenate([q_ref[row0:row0 + n, c:c + HEAD_DIM] for c in cols], axis=0)
            s_scr[base:base + rows, :] = lax.dot_general(qb, k_h, NT_DIMS, preferred_element_type=F32)
            sink_terms = []
            for r in range(0, rows, ROW_CHUNK):
                rs = slice(base + r, base + r + ROW_CHUNK)
                if band:
                    qr = row0 + r % n
                    parts = [s_scr[rs, 0:nk] + bias_scr[qr:qr + ROW_CHUNK, :]]
                    if n_ctx:
                        parts.append(s_scr[rs, nk:])
                else:
                    parts = [s_scr[rs, :]]
                m = functools.reduce(jnp.maximum, [jnp.max(a, axis=-1, keepdims=True) for a in parts])
                if has_sink:
                    head = (pl.program_id(1) * kv_heads + kvh) * GROUP + heads[r // n]
                    sk = sink_ref[head] * LOG2E
                    m = jnp.maximum(m, sk)
                    sink_terms.append(jnp.exp2(sk - m))
                col = 0
                for a in parts:
                    p_scr[rs, col:col + a.shape[1]] = jnp.exp2(a - m).astype(BF16)
                    col += a.shape[1]
            acc = jnp.dot(p_scr[base:base + rows, :], v_ext, preferred_element_type=F32)
            denom = acc[:, HEAD_DIM:]
            if has_sink:
                denom = denom + jnp.concatenate(sink_terms, axis=0)
            o = acc[:, :HEAD_DIM] / denom
            for j, c in enumerate(cols):
                o_ref[row0:row0 + n, c:c + HEAD_DIM] = o[j * n:(j + 1) * n, :].astype(o_ref.dtype)
            base += rows


def _gqa_attention(q, k, v, cache, slot, sink, *, batch, seq, tq, kv_heads, block_rows, band):
    nq_tiles = seq // tq
    n_ctx = 0 if cache is None else cache[0].shape[2] // KV_HEADS
    assert cache is None or kv_heads == 1
    qw, kw = kv_heads * GROUP * HEAD_DIM, kv_heads * HEAD_DIM
    operands, in_specs = [], []
    if sink is not None:
        operands.append(sink)
        in_specs.append(pl.BlockSpec(memory_space=pltpu.SMEM))
    operands += [q, k, v]
    in_specs += [
        pl.BlockSpec((tq, qw), lambda b, h, i: (b * nq_tiles + i, h)),
        pl.BlockSpec((seq, kw), lambda b, h, i: (b, h)),
        pl.BlockSpec((seq, kw), lambda b, h, i: (b, h)),
    ]
    if cache is not None:
        operands += list(cache)
        in_specs += [pl.BlockSpec((None, None, n_ctx * KV_HEADS, HEAD_DIM), lambda b, h, i: (b, slot, 0, 0))] * 2
    kern = functools.partial(_gqa_attn_kernel, tq=tq, kv_heads=kv_heads, block_rows=block_rows, n_new=seq,
                             n_ctx=n_ctx, band=band, has_sink=sink is not None)
    n_new_keys = min(seq, tq + 2 * WINDOW) if band else seq
    n_keys = n_new_keys + n_ctx
    scratch = [pltpu.VMEM((kv_heads * GROUP * tq, n_keys), F32), pltpu.VMEM((kv_heads * GROUP * tq, n_keys), BF16)]
    if band:
        scratch.append(pltpu.VMEM((tq, n_new_keys), F32))
    if n_ctx:
        scratch += [pltpu.VMEM((n_ctx, HEAD_DIM), BF16)] * 2
    return pl.pallas_call(
        kern,
        out_shape=jax.ShapeDtypeStruct(q.shape, BF16),
        grid=(batch, KV_HEADS // kv_heads, nq_tiles),
        in_specs=in_specs,
        out_specs=pl.BlockSpec((tq, qw), lambda b, h, i: (b * nq_tiles + i, h)),
        scratch_shapes=scratch,
        compiler_params=_params("parallel", "parallel", "arbitrary"),
        name="gqa_attn",
    )(*operands)


def _diff_attn_kernel(*refs, tq, kv_heads, block_heads, n_ctx, lam_init):
    it = iter(refs)
    lq1_ref, lk1_ref, lq2_ref, lk2_ref, gs_ref = next(it), next(it), next(it), next(it), next(it)
    q_ref, k_ref, v_ref = next(it), next(it), next(it)
    kc_ref = vc_ref = None
    if n_ctx:
        kc_ref, vc_ref = next(it), next(it)
    o_ref = next(it)
    s_scrs = (next(it), next(it))
    p_scr = next(it)
    if n_ctx:
        kc_scr, vc_scr = next(it), next(it)
        for head in range(DIFF_KV_HEADS):
            @pl.when(jnp.logical_and(pl.program_id(1) == head, pl.program_id(2) == 0))
            def _():
                for half in range(2):
                    rows = pl.ds(2 * head + half, n_ctx, stride=2 * DIFF_KV_HEADS)
                    cols = slice(half * DIFF_DIM, (half + 1) * DIFF_DIM)
                    kc_scr[:, cols] = kc_ref[rows, :].astype(BF16)
                    vc_scr[:, cols] = vc_ref[rows, :].astype(BF16)

    lam = (jnp.exp(jnp.sum(lq1_ref[...] * lk1_ref[...], axis=-1, keepdims=True))
           - jnp.exp(jnp.sum(lq2_ref[...] * lk2_ref[...], axis=-1, keepdims=True)) + lam_init)
    k = k_ref[...].astype(BF16)
    v = v_ref[...].astype(BF16)
    if n_ctx:
        k = jnp.concatenate([k, kc_scr[...]], axis=0)
        v = jnp.concatenate([v, vc_scr[...]], axis=0)

    width = 2 * DIFF_DIM
    rows = block_heads * tq
    base = 0
    for kvh in range(kv_heads):
        k_h = k[:, kvh * width:(kvh + 1) * width]
        v_h = v[:, kvh * width:(kvh + 1) * width]
        for h0 in range(0, DIFF_GROUP, block_heads):
            heads = [kvh * DIFF_GROUP + g for g in range(h0, h0 + block_heads)]
            for half in range(2):
                qb = jnp.concatenate(
                    [q_ref[:, (2 * g + half) * DIFF_DIM:(2 * g + half + 1) * DIFF_DIM] for g in heads], axis=0)
                s_scrs[half][base:base + rows, :] = lax.dot_general(
                    qb, k_h[:, half * DIFF_DIM:(half + 1) * DIFF_DIM], NT_DIMS, preferred_element_type=F32)
            for r in range(0, rows, ROW_CHUNK):
                rs = slice(base + r, base + r + ROW_CHUNK)
                s0 = s_scrs[0][rs, :]
                e0 = jnp.exp2(s0 - jnp.max(s0, axis=-1, keepdims=True))
                r0 = 1.0 / jnp.sum(e0, axis=-1, keepdims=True)
                s1 = s_scrs[1][rs, :]
                e1 = jnp.exp2(s1 - jnp.max(s1, axis=-1, keepdims=True))
                r1 = lam / jnp.sum(e1, axis=-1, keepdims=True)
                p_scr[rs, :] = (e0 * r0 - e1 * r1).astype(BF16)
            acc = jnp.dot(p_scr[base:base + rows, :], v_h, preferred_element_type=F32)
            y = (_rms(acc) * gs_ref[...]) * (1.0 - lam_init)
            for j, g in enumerate(heads):
                o_ref[:, g * width:(g + 1) * width] = y[j * tq:(j + 1) * tq, :].astype(o_ref.dtype)
            base += rows


def _diff_attention(q, k, v, cache, slot, lam_vecs, g_subln, *, batch, seq, tq, kv_heads, lam_init):
    nq_tiles = seq // tq
    n_ctx = 0 if cache is None else cache[0].shape[2] // (2 * DIFF_KV_HEADS)
    assert cache is None or kv_heads == 1
    qw = kv_heads * DIFF_GROUP * 2 * DIFF_DIM
    kw = kv_heads * 2 * DIFF_DIM
    operands = [a.reshape(a.shape[0], 1, DIFF_DIM) for a in lam_vecs]
    in_specs = [pl.BlockSpec((None, 1, DIFF_DIM), lambda b, h, i: (slot, 0, 0))] * 4
    operands.append(g_subln.reshape(g_subln.shape[0], 1, 2 * DIFF_DIM))
    in_specs.append(pl.BlockSpec((None, 1, 2 * DIFF_DIM), lambda b, h, i: (slot, 0, 0)))
    operands += [q, k, v]
    in_specs += [
        pl.BlockSpec((tq, qw), lambda b, h, i: (b * nq_tiles + i, h)),
        pl.BlockSpec((seq, kw), lambda b, h, i: (b, h)),
        pl.BlockSpec((seq, kw), lambda b, h, i: (b, h)),
    ]
    if cache is not None:
        operands += list(cache)
        in_specs += [pl.BlockSpec((None, None) + c.shape[2:], lambda b, h, i: (b, slot, 0, 0)) for c in cache]
    kern = functools.partial(_diff_attn_kernel, tq=tq, kv_heads=kv_heads, block_heads=2, n_ctx=n_ctx,
                             lam_init=lam_init)
    score_shape = (kv_heads * DIFF_GROUP * tq, seq + n_ctx)
    ctx_scratch = [pltpu.VMEM((n_ctx, 2 * DIFF_DIM), BF16)] * 2 if n_ctx else []
    return pl.pallas_call(
        kern,
        out_shape=jax.ShapeDtypeStruct(q.shape, BF16),
        grid=(batch, DIFF_KV_HEADS // kv_heads, nq_tiles),
        in_specs=in_specs,
        out_specs=pl.BlockSpec((tq, qw), lambda b, h, i: (b * nq_tiles + i, h)),
        scratch_shapes=[pltpu.VMEM(score_shape, F32), pltpu.VMEM(score_shape, F32),
                        pltpu.VMEM(score_shape, BF16)] + ctx_scratch,
        compiler_params=_params("parallel", "parallel", "arbitrary"),
        name="diff_attn",
    )(*operands)


def _oproj_kernel(o_ref, w_ref, x_ref, mod_ref, g_ref, out_ref, h_ref, *, d, row_blocks, row_of_tile):
    mod = mod_ref[pl.ds(row_of_tile(pl.program_id(0)), 1), :]
    gain1 = mod[:, 2 * d:3 * d] * g_ref[1:2, :]
    gain2 = g_ref[2:3, :] * (1.0 + mod[:, 4 * d:5 * d])
    shift2 = mod[:, 3 * d:4 * d]
    rows = o_ref.shape[0] // row_blocks
    for r in range(row_blocks):
        rs = slice(r * rows, (r + 1) * rows)
        m = jnp.dot(o_ref[rs, :], w_ref[...], preferred_element_type=F32)
        x_new = x_ref[rs, :] + _rms(m) * gain1
        out_ref[rs, :] = x_new
        h_ref[rs, :] = (_rms(x_new) * gain2 + shift2).astype(BF16)


def _oproj(o, w_o, x, mod, g_norm, layer, slot, row_of_tile, *, tm):
    t, d = x.shape
    nin = o.shape[1]
    return pl.pallas_call(
        functools.partial(_oproj_kernel, d=d, row_blocks=4 if tm % (4 * ROW_CHUNK) == 0 else 1,
                          row_of_tile=row_of_tile),
        out_shape=(jax.ShapeDtypeStruct((t, d), F32), jax.ShapeDtypeStruct((t, d), BF16)),
        grid=(t // tm,),
        in_specs=[
            pl.BlockSpec((tm, nin), lambda i: (i, 0)),
            pl.BlockSpec((None, nin, d), lambda i: (slot, 0, 0), pipeline_mode=pl.Buffered(1)),
            pl.BlockSpec((tm, d), lambda i: (i, 0)),
            pl.BlockSpec((None,) + mod.shape[1:], lambda i: (layer, 0, 0)),
            pl.BlockSpec((None, 4, d), lambda i: (layer, 0, 0)),
        ],
        out_specs=(pl.BlockSpec((tm, d), lambda i: (i, 0)), pl.BlockSpec((tm, d), lambda i: (i, 0))),
        compiler_params=_params("parallel"),
        name="out_proj",
    )(o, w_o, x, mod, g_norm)


def _ffn_kernel(h_ref, x_ref, mod_ref, g_ref, wg_ref, wu_ref, wd_ref, out_ref, *, d, row_of_tile):
    j = pl.program_id(1)
    last = pl.num_programs(1) - 1

    def partial_sum(rows):
        h = h_ref[rows, :]
        a = jnp.dot(h, wg_ref[...], preferred_element_type=F32)
        b = jnp.dot(h, wu_ref[...], preferred_element_type=F32)
        u = ((a * jax.nn.sigmoid(a)) * b).astype(BF16)
        return jnp.dot(u, wd_ref[...], preferred_element_type=F32)

    @pl.when(j == 0)
    def _():
        out_ref[...] = partial_sum(slice(None))

    @pl.when(jnp.logical_and(j > 0, j < last))
    def _():
        out_ref[...] += partial_sum(slice(None))

    @pl.when(j == last)
    def _():
        gate = mod_ref[pl.ds(row_of_tile(pl.program_id(0)), 1), 5 * d:6 * d]
        gain = gate * g_ref[3:4, :]
        rows = out_ref.shape[0] // FFN_EPILOGUE_BLOCKS
        for r in range(FFN_EPILOGUE_BLOCKS):
            rs = slice(r * rows, (r + 1) * rows)
            acc = out_ref[rs, :] + partial_sum(rs)
            out_ref[rs, :] = x_ref[rs, :] + _rms(acc) * gain


def _ffn(h, x, mod, g_norm, w_gate, w_up, w_down, layer, row_of_tile, *, tm, fc):
    t, d = x.shape
    d_ff = w_gate.shape[-1]
    return pl.pallas_call(
        functools.partial(_ffn_kernel, d=d, row_of_tile=row_of_tile),
        out_shape=jax.ShapeDtypeStruct((t, d), F32),
        grid=(t // tm, d_ff // fc),
        in_specs=[
            pl.BlockSpec((tm, d), lambda i, j: (i, 0)),
            pl.BlockSpec((tm, d), lambda i, j: (i, 0)),
            pl.BlockSpec((None,) + mod.shape[1:], lambda i, j: (layer, 0, 0)),
            pl.BlockSpec((None, 4, d), lambda i, j: (layer, 0, 0)),
            pl.BlockSpec((None, d, fc), lambda i, j: (layer, 0, j)),
            pl.BlockSpec((None, d, fc), lambda i, j: (layer, 0, j)),
            pl.BlockSpec((None, fc, d), lambda i, j: (layer, j, 0)),
        ],
        out_specs=pl.BlockSpec((tm, d), lambda i, j: (i, 0)),
        compiler_params=_params("parallel", "arbitrary"),
        name="ffn",
    )(h, x, mod, g_norm, w_gate, w_up, w_down)


def _rope_tables(n_tokens):
    nf = HEAD_DIM // 4
    t = jnp.arange(n_tokens, dtype=jnp.int32)
    rows = (t // GRID_W).astype(F32)
    cols = (t % GRID_W).astype(F32)
    inv = 1.0 / (ROPE_THETA ** (jnp.arange(nf, dtype=F32) / nf))
    ar = rows[:, None] * inv
    ac = cols[:, None] * inv
    cos = jnp.concatenate([jnp.cos(ar), jnp.cos(ar), jnp.cos(ac), jnp.cos(ac)], axis=-1)
    sin = jnp.concatenate([-jnp.sin(ar), jnp.sin(ar), -jnp.sin(ac), jnp.sin(ac)], axis=-1)
    return cos, sin


def _diff_lambda_init(layer):
    return 0.8 - 0.6 * math.exp(-0.3 * layer)


def _largest_tile(n, cap):
    t = cap
    while n % t:
        t //= 2
    return t


def _run_path(x3, first_row, rows_per_batch, caches, mod, p, depth):
    batch, seq, d = x3.shape
    latent = caches is not None
    x = x3.reshape(batch * seq, d)
    tm = _largest_tile(seq if latent else batch * seq, 512)
    tiles_per_batch = seq // tm if latent else None

    if latent:
        def row_of_tile(i):
            return first_row + (i // tiles_per_batch) * rows_per_batch
    else:
        def row_of_tile(i):
            return first_row

    rope_tables = _rope_tables(seq) if latent else None
    kv_dtype = BF16 if latent else F32
    kept = ([], [], [], [], [], [])
    for layer in range(depth):
        kind, slot = layer % N_MIXERS, layer // N_MIXERS
        cache = None if not latent else (caches[2 * kind], caches[2 * kind + 1])
        common = dict(tm=tm, kv_dtype=kv_dtype, rope_tables=rope_tables)
        attn_shape = dict(kv_heads=1 if latent else KV_HEADS)
        if kind == 0:
            q, k, v = _qkv(x, mod, p['g_norm'], p['w_qkv_full'], layer, slot, row_of_tile,
                           nq=N_HEADS * HEAD_DIM, nk=KV_HEADS * HEAD_DIM,
                           gains=(p['g_q_full'], p['g_k_full']), **common)
            o = _gqa_attention(q, k, v, cache, slot, None, batch=batch, seq=seq, band=False,
                               tq=_largest_tile(seq, 512), block_rows=ATTN_BLOCK_ROWS, **attn_shape)
            w_o = p['w_o_full']
        elif kind == 1:
            q, k, v = _qkv(x, mod, p['g_norm'], p['w_qkv_win'], layer, slot, row_of_tile,
                           nq=N_HEADS * HEAD_DIM, nk=KV_HEADS * HEAD_DIM, **common)
            o = _gqa_attention(q, k, v, cache, slot, p['sink_win'][slot], batch=batch, seq=seq, band=latent,
                               tq=_largest_tile(seq, 256),
                               block_rows=2 * ATTN_BLOCK_ROWS if latent else ATTN_BLOCK_ROWS, **attn_shape)
            w_o = p['w_o_win']
        else:
            q, k, v = _qkv(x, mod, p['g_norm'], p['w_qkv_diff'], layer, slot, row_of_tile,
                           nq=DIFF_HEADS * 2 * DIFF_DIM, nk=DIFF_KV_HEADS * 2 * DIFF_DIM, **common)
            o = _diff_attention(q, k, v, cache, slot,
                                (p['lam_q1'], p['lam_k1'], p['lam_q2'], p['lam_k2']), p['g_subln_diff'],
                                batch=batch, seq=seq, tq=_largest_tile(seq, 256),
                                kv_heads=1,
                                lam_init=_diff_lambda_init(layer))
            w_o = p['w_o_diff']
        if not latent:
            kept[2 * kind].append(k)
            kept[2 * kind + 1].append(v)
        x, h = _oproj(o, w_o, x, mod, p['g_norm'], layer, slot, row_of_tile, tm=tm)
        x = _ffn(h, x, mod, p['g_norm'], p['w_gate'], p['w_up'], p['w_down'], layer, row_of_tile,
                 tm=tm, fc=FFN_CHUNK)
    return x.reshape(batch, seq, d), kept


def kernel(x_prompt, x_sample, c, cache_k_full, cache_v_full, cache_k_win, cache_v_win, cache_k_diff,
           cache_v_diff, c_ctx, w_ada, b_ada, g_norm, w_qkv_full, w_o_full, g_q_full, g_k_full, w_qkv_win,
           w_o_win, sink_win, w_qkv_diff, w_o_diff, lam_q1, lam_k1, lam_q2, lam_k2, g_subln_diff, w_gate,
           w_up, w_down):
    depth, d = w_ada.shape[0], w_ada.shape[1]
    batch, seq, _ = x_prompt.shape
    dec_batch, dec_seq, _ = x_sample.shape

    p = {
        'g_norm': g_norm,
        'w_qkv_full': w_qkv_full.astype(BF16), 'w_o_full': w_o_full.astype(BF16),
        'g_q_full': g_q_full, 'g_k_full': g_k_full,
        'w_qkv_win': w_qkv_win.astype(BF16), 'w_o_win': w_o_win.astype(BF16), 'sink_win': sink_win,
        'w_qkv_diff': w_qkv_diff.astype(BF16), 'w_o_diff': w_o_diff.astype(BF16),
        'lam_q1': lam_q1, 'lam_k1': lam_k1, 'lam_q2': lam_q2, 'lam_k2': lam_k2,
        'g_subln_diff': g_subln_diff,
        'w_gate': w_gate.astype(BF16), 'w_up': w_up.astype(BF16), 'w_down': w_down.astype(BF16),
    }

    n_rows = -(-(1 + dec_batch) // SUBLANES) * SUBLANES
    cond = jnp.zeros((n_rows, d), F32).at[0].set(c_ctx).at[1:1 + dec_batch].set(c)
    mod = _modulation(cond, w_ada, b_ada)

    y_prompt, kept = _run_path(x_prompt, 0, 0, None, mod, p, depth)

    def head_rows_cache(a):
        return a.reshape(a.shape[0], a.shape[1], -1, LANES)

    caches = tuple(head_rows_cache(a) for a in (cache_k_full, cache_v_full, cache_k_win, cache_v_win,
                                                cache_k_diff, cache_v_diff))
    y_sample, _ = _run_path(x_sample, 1, 1, caches, mod, p, depth)

    def stack(parts, tail):
        return jnp.stack([a.reshape((batch, seq) + tail) for a in parts], axis=1)

    return (y_prompt, y_sample,
            stack(kept[0], (KV_HEADS, HEAD_DIM)), stack(kept[1], (KV_HEADS, HEAD_DIM)),
            stack(kept[2], (KV_HEADS, HEAD_DIM)), stack(kept[3], (KV_HEADS, HEAD_DIM)),
            stack(kept[4], (DIFF_KV_HEADS, 2, DIFF_DIM)), stack(kept[5], (DIFF_KV_HEADS, 2 * DIFF_DIM)))
```

```python
import functools
import math

import jax
import jax.numpy as jnp
from jax import lax
from jax.experimental import pallas as pl
from jax.experimental.pallas import tpu as pltpu

F32 = jnp.float32
BF16 = jnp.bfloat16

N_MIXERS = 3
N_HEADS = 16
KV_HEADS = 4
HEAD_DIM = 128
GROUP = N_HEADS // KV_HEADS
WINDOW = 128
GRID_W = 64
DIFF_HEADS = 8
DIFF_KV_HEADS = 2
DIFF_GROUP = DIFF_HEADS // DIFF_KV_HEADS
DIFF_DIM = 128
ROPE_THETA = 10000.0
EPS = 1e-6
NEG_INF = -1e30

LANES = 128
SUBLANES = 8
V7X_VMEM_BYTES = 64 * 1024 * 1024
VMEM_LIMIT = V7X_VMEM_BYTES - 8 * 1024 * 1024

NT_DIMS = (((1,), (1,)), ((), ()))
LOG2E = math.log2(math.e)
ROW_CHUNK = 2 * SUBLANES
ATTN_BLOCK_ROWS = 256
FFN_CHUNK = 512
FFN_EPILOGUE_BLOCKS = 2


def _params(*semantics):
    return pltpu.CompilerParams(dimension_semantics=semantics, vmem_limit_bytes=VMEM_LIMIT)


def _rms(x):
    return x * lax.rsqrt(jnp.mean(x * x, axis=-1, keepdims=True) + EPS)


def _mod_kernel(c_ref, w_ref, b_ref, o_ref):
    c = c_ref[...]
    a = (c * jax.nn.sigmoid(c)).astype(BF16)
    o_ref[...] = jnp.dot(a, w_ref[...].astype(BF16), preferred_element_type=F32) + b_ref[...]


def _modulation(cond, w_ada, b_ada):
    n_layers, d, n = w_ada.shape
    rows = cond.shape[0]
    tn = 1024
    return pl.pallas_call(
        _mod_kernel,
        out_shape=jax.ShapeDtypeStruct((n_layers, rows, n), F32),
        grid=(n_layers, n // tn),
        in_specs=[
            pl.BlockSpec((rows, d), lambda l, j: (0, 0)),
            pl.BlockSpec((None, d, tn), lambda l, j: (l, 0, j)),
            pl.BlockSpec((None, 1, tn), lambda l, j: (l, 0, j)),
        ],
        out_specs=pl.BlockSpec((None, rows, tn), lambda l, j: (l, 0, j)),
        compiler_params=_params("parallel", "parallel"),
        name="adaln_mod",
    )(cond, w_ada, b_ada.reshape(n_layers, 1, n))


def _qkv_kernel(*refs, d, nq, nk, qk_norm, rope, q_scale, row_of_tile, head_rows_seq, n_aliased):
    it = iter(refs)
    x_ref, mod_ref, g_ref, w_ref = next(it), next(it), next(it), next(it)
    gq_ref = gk_ref = cos_ref = sin_ref = None
    if qk_norm:
        gq_ref, gk_ref = next(it), next(it)
    if rope:
        cos_ref, sin_ref = next(it), next(it)
    for _ in range(n_aliased):
        next(it)
    q_ref, k_ref, v_ref = next(it), next(it), next(it)

    def store_chunk(ref, chunk, n_chunks, z):
        if head_rows_seq is None:
            ref[:, chunk * LANES:(chunk + 1) * LANES] = z.astype(ref.dtype)
        else:
            seq = head_rows_seq
            for b in range(z.shape[0] // seq):
                ref[b, pl.ds(chunk, seq, stride=n_chunks), :] = z[b * seq:(b + 1) * seq, :].astype(ref.dtype)

    mod = mod_ref[pl.ds(row_of_tile(pl.program_id(0)), 1), :]
    h = (_rms(x_ref[...]) * (g_ref[0:1, :] * (1.0 + mod[:, d:2 * d])) + mod[:, 0:d]).astype(BF16)

    def partner(a):
        lane = lax.broadcasted_iota(jnp.int32, a.shape, 1)
        return jnp.where((lane & 32) != 0, pltpu.roll(a, 32, 1), pltpu.roll(a, LANES - 32, 1))

    def lane_factors(gain_ref, scale):
        gain = None if gain_ref is None else jnp.broadcast_to(gain_ref[...] * scale, (SUBLANES, LANES))
        if not rope:
            return (scale if gain is None else gain[0:1, :]), None
        if gain is None:
            return cos_ref[...] * scale, sin_ref[...] * scale
        return cos_ref[...] * gain[0:1, :], sin_ref[...] * partner(gain)[0:1, :]

    def finish(z, normed, factors):
        c, s = factors
        if s is not None:
            t = z * c + partner(z) * s
        else:
            t = z if isinstance(c, float) and c == 1.0 else z * c
        if normed:
            t = t * lax.rsqrt(jnp.mean(z * z, axis=-1, keepdims=True) + EPS)
        return t

    q_factors = lane_factors(gq_ref, q_scale)
    k_factors = lane_factors(gk_ref, 1.0)
    chunk = 512
    for c0 in range(0, nq + nk, chunk):
        acc = jnp.dot(h, w_ref[:, c0:c0 + chunk], preferred_element_type=F32)
        for j in range(chunk // LANES):
            col = c0 + j * LANES
            z = acc[:, j * LANES:(j + 1) * LANES]
            if col < nq:
                q_ref[:, col:col + LANES] = finish(z, qk_norm, q_factors).astype(q_ref.dtype)
            else:
                store_chunk(k_ref, (col - nq) // LANES, nk // LANES, finish(z, qk_norm, k_factors))
    v = jnp.dot(h, w_ref[:, nq + nk:], preferred_element_type=F32)
    n_v_chunks = v.shape[1] // LANES
    for c in range(n_v_chunks):
        store_chunk(v_ref, c, n_v_chunks, v[:, c * LANES:(c + 1) * LANES])


def _qkv(x, mod, g_norm, w_qkv, layer, slot, row_of_tile, *, tm, nq, nk, gains=None, rope_tables=None,
         state_seq=None, state_slots=1, state_prev=None):
    t, d = x.shape
    n_all = w_qkv.shape[-1]
    nv = n_all - nq - nk
    operands = [x, mod, g_norm, w_qkv]
    in_specs = [
        pl.BlockSpec((tm, d), lambda i: (i, 0)),
        pl.BlockSpec((None,) + mod.shape[1:], lambda i: (layer, 0, 0)),
        pl.BlockSpec((None, 4, d), lambda i: (layer, 0, 0)),
        pl.BlockSpec((None, d, n_all), lambda i: (slot, 0, 0), pipeline_mode=pl.Buffered(1)),
    ]
    if gains is not None:
        for g in gains:
            operands.append(g.reshape(g.shape[0], 1, LANES))
            in_specs.append(pl.BlockSpec((None, 1, LANES), lambda i: (slot, 0, 0)))
    if rope_tables is not None:
        tiles_per_seq = rope_tables[0].shape[0] // tm
        for tab in rope_tables:
            operands.append(tab)
            in_specs.append(pl.BlockSpec((tm, LANES), lambda i: (i % tiles_per_seq, 0)))
    aliases = {}
    if state_seq is None:
        kv_shapes = [jax.ShapeDtypeStruct((t, n), BF16) for n in (nk, nv)]
        kv_specs = [pl.BlockSpec((tm, n), lambda i: (i, 0)) for n in (nk, nv)]
    else:
        batches = tm // state_seq
        kv_shapes = [jax.ShapeDtypeStruct((t // state_seq, state_slots, state_seq * (n // LANES), LANES), F32)
                     for n in (nk, nv)]
        kv_specs = [pl.BlockSpec((batches, None) + s.shape[2:], lambda i: (i, slot, 0, 0)) for s in kv_shapes]
        if state_prev is not None:
            for j, prev in enumerate(state_prev):
                aliases[len(operands)] = 1 + j
                operands.append(prev)
                in_specs.append(pl.BlockSpec(memory_space=pl.ANY))
    kern = functools.partial(_qkv_kernel, d=d, nq=nq, nk=nk, qk_norm=gains is not None,
                             rope=rope_tables is not None, q_scale=HEAD_DIM ** -0.5 * LOG2E,
                             row_of_tile=row_of_tile, head_rows_seq=state_seq, n_aliased=len(aliases))
    return pl.pallas_call(
        kern,
        out_shape=[jax.ShapeDtypeStruct((t, nq), BF16)] + kv_shapes,
        grid=(t // tm,),
        in_specs=in_specs,
        out_specs=[pl.BlockSpec((tm, nq), lambda i: (i, 0))] + kv_specs,
        input_output_aliases=aliases,
        compiler_params=_params("parallel"),
        name="qkv_proj",
    )(*operands)


def _gqa_attn_kernel(*refs, tq, kv_heads, block_rows, n_new, n_ctx, band, has_sink, head_rows):
    it = iter(refs)
    sink_ref = next(it) if has_sink else None
    q_ref, k_ref, v_ref = next(it), next(it), next(it)
    kc_ref = vc_ref = None
    if n_ctx:
        kc_ref, vc_ref = next(it), next(it)
    o_ref = next(it)
    s_scr, p_scr = next(it), next(it)
    bias_scr = next(it) if band else None
    if n_ctx:
        kc_scr, vc_scr = next(it), next(it)
        for head in range(KV_HEADS):
            @pl.when(jnp.logical_and(pl.program_id(1) == head, pl.program_id(2) == 0))
            def _():
                kc_scr[...] = kc_ref[pl.ds(head, n_ctx, stride=KV_HEADS), :].astype(BF16)
                vc_scr[...] = vc_ref[pl.ds(head, n_ctx, stride=KV_HEADS), :].astype(BF16)

    band_width = tq + 2 * WINDOW
    q0 = pl.program_id(2) * tq
    if head_rows:
        nk = n_new

        def head_kv(kvh):
            rows = pl.ds(kvh, n_new, stride=KV_HEADS)
            return k_ref[rows, :].astype(BF16), v_ref[rows, :].astype(BF16)
    else:
        if band and n_new > band_width:
            start = pl.multiple_of(jnp.clip(q0 - WINDOW, 0, n_new - band_width), LANES)
            k = k_ref[pl.ds(start, band_width), :]
            v = v_ref[pl.ds(start, band_width), :]
        else:
            start = 0
            k = k_ref[...]
            v = v_ref[...]
        nk = k.shape[0]
        if n_ctx:
            k = jnp.concatenate([k, kc_scr[...]], axis=0)
            v = jnp.concatenate([v, vc_scr[...]], axis=0)

        def head_kv(kvh):
            cols = slice(kvh * HEAD_DIM, (kvh + 1) * HEAD_DIM)
            return k[:, cols], v[:, cols]
    if band:
        qpos = q0 + lax.broadcasted_iota(jnp.int32, (tq, nk), 0)
        kpos = start + lax.broadcasted_iota(jnp.int32, (tq, nk), 1)
        bias_scr[...] = jnp.where(jnp.abs(qpos - kpos) <= WINDOW, 0.0, NEG_INF).astype(F32)
    ones = jnp.ones((nk + n_ctx, HEAD_DIM), BF16)

    if block_rows <= tq:
        blocks = [((h,), r0, block_rows) for h in range(GROUP) for r0 in range(0, tq, block_rows)]
    else:
        stacked = block_rows // tq
        blocks = [(tuple(range(h0, h0 + stacked)), 0, tq) for h0 in range(0, GROUP, stacked)]

    base = 0
    for kvh in range(kv_heads):
        k_h, v_h = head_kv(kvh)
        v_ext = jnp.concatenate([v_h, ones], axis=1)
        for heads, row0, n in blocks:
            rows = len(heads) * n
            cols = [(kvh * GROUP + h) * HEAD_DIM for h in heads]
            qb = jnp.concatenate([q_ref[row0:row0 + n, c:c + HEAD_DIM] for c in cols], axis=0)
            s_scr[base:base + rows, :] = lax.dot_general(qb, k_h, NT_DIMS, preferred_element_type=F32)
            sink_terms = []
            for r in range(0, rows, ROW_CHUNK):
                rs = slice(base + r, base + r + ROW_CHUNK)
                if band:
                    qr = row0 + r % n
                    parts = [s_scr[rs, 0:nk] + bias_scr[qr:qr + ROW_CHUNK, :]]
                    if n_ctx:
                        parts.append(s_scr[rs, nk:])
                else:
                    parts = [s_scr[rs, :]]
                m = functools.reduce(jnp.maximum, [jnp.max(a, axis=-1, keepdims=True) for a in parts])
                if has_sink:
                    head = (pl.program_id(1) * kv_heads + kvh) * GROUP + heads[r // n]
                    sk = sink_ref[head] * LOG2E
                    m = jnp.maximum(m, sk)
                    sink_terms.append(jnp.exp2(sk - m))
                col = 0
                for a in parts:
                    p_scr[rs, col:col + a.shape[1]] = jnp.exp2(a - m).astype(BF16)
                    col += a.shape[1]
            acc = jnp.dot(p_scr[base:base + rows, :], v_ext, preferred_element_type=F32)
            denom = acc[:, HEAD_DIM:]
            if has_sink:
                denom = denom + jnp.concatenate(sink_terms, axis=0)
            o = acc[:, :HEAD_DIM] / denom
            for j, c in enumerate(cols):
                o_ref[row0:row0 + n, c:c + HEAD_DIM] = o[j * n:(j + 1) * n, :].astype(o_ref.dtype)
            base += rows


def _gqa_attention(q, k, v, cache, slot, sink, *, batch, seq, tq, kv_heads, block_rows, band):
    nq_tiles = seq // tq
    head_rows = cache is None
    n_ctx = 0 if cache is None else cache[0].shape[2] // KV_HEADS
    assert kv_heads == (KV_HEADS if head_rows else 1)
    qw, kw = kv_heads * GROUP * HEAD_DIM, kv_heads * HEAD_DIM
    operands, in_specs = [], []
    if sink is not None:
        operands.append(sink)
        in_specs.append(pl.BlockSpec(memory_space=pltpu.SMEM))
    operands += [q, k, v]
    in_specs.append(pl.BlockSpec((tq, qw), lambda b, h, i: (b * nq_tiles + i, h)))
    if head_rows:
        in_specs += [pl.BlockSpec((None, None, seq * KV_HEADS, HEAD_DIM), lambda b, h, i: (b, slot, 0, 0))] * 2
    else:
        in_specs += [pl.BlockSpec((seq, kw), lambda b, h, i: (b, h))] * 2
        operands += list(cache)
        in_specs += [pl.BlockSpec((None, None, n_ctx * KV_HEADS, HEAD_DIM), lambda b, h, i: (b, slot, 0, 0))] * 2
    kern = functools.partial(_gqa_attn_kernel, tq=tq, kv_heads=kv_heads, block_rows=block_rows, n_new=seq,
                             n_ctx=n_ctx, band=band, has_sink=sink is not None, head_rows=head_rows)
    n_new_keys = min(seq, tq + 2 * WINDOW) if band else seq
    n_keys = n_new_keys + n_ctx
    scratch = [pltpu.VMEM((kv_heads * GROUP * tq, n_keys), F32), pltpu.VMEM((kv_heads * GROUP * tq, n_keys), BF16)]
    if band:
        scratch.append(pltpu.VMEM((tq, n_new_keys), F32))
    if n_ctx:
        scratch += [pltpu.VMEM((n_ctx, HEAD_DIM), BF16)] * 2
    return pl.pallas_call(
        kern,
        out_shape=jax.ShapeDtypeStruct(q.shape, BF16),
        grid=(batch, KV_HEADS // kv_heads, nq_tiles),
        in_specs=in_specs,
        out_specs=pl.BlockSpec((tq, qw), lambda b, h, i: (b * nq_tiles + i, h)),
        scratch_shapes=scratch,
        compiler_params=_params("parallel", "parallel", "arbitrary"),
        name="gqa_attn",
    )(*operands)


def _diff_attn_kernel(*refs, tq, kv_heads, block_heads, n_new, n_ctx, lam_init, head_rows):
    it = iter(refs)
    lq1_ref, lk1_ref, lq2_ref, lk2_ref, gs_ref = next(it), next(it), next(it), next(it), next(it)
    q_ref, k_ref, v_ref = next(it), next(it), next(it)
    kc_ref = vc_ref = None
    if n_ctx:
        kc_ref, vc_ref = next(it), next(it)
    o_ref = next(it)
    s_scrs = (next(it), next(it))
    p_scr = next(it)

    def pick_head(k_src, v_src, n_keys):
        k_dst, v_dst = next(it), next(it)
        for head in range(DIFF_KV_HEADS):
            @pl.when(jnp.logical_and(pl.program_id(1) == head, pl.program_id(2) == 0))
            def _():
                for half in range(2):
                    rows = pl.ds(2 * head + half, n_keys, stride=2 * DIFF_KV_HEADS)
                    cols = slice(half * DIFF_DIM, (half + 1) * DIFF_DIM)
                    k_dst[:, cols] = k_src[rows, :].astype(BF16)
                    v_dst[:, cols] = v_src[rows, :].astype(BF16)
        return k_dst[...], v_dst[...]

    lam = (jnp.exp(jnp.sum(lq1_ref[...] * lk1_ref[...], axis=-1, keepdims=True))
           - jnp.exp(jnp.sum(lq2_ref[...] * lk2_ref[...], axis=-1, keepdims=True)) + lam_init)
    if head_rows:
        k, v = pick_head(k_ref, v_ref, n_new)
    else:
        k, v = k_ref[...], v_ref[...]
    if n_ctx:
        kc, vc = pick_head(kc_ref, vc_ref, n_ctx)
        k = jnp.concatenate([k, kc], axis=0)
        v = jnp.concatenate([v, vc], axis=0)

    width = 2 * DIFF_DIM
    rows = block_heads * tq
    base = 0
    for kvh in range(kv_heads):
        k_h = k[:, kvh * width:(kvh + 1) * width]
        v_h = v[:, kvh * width:(kvh + 1) * width]
        for h0 in range(0, DIFF_GROUP, block_heads):
            heads = [kvh * DIFF_GROUP + g for g in range(h0, h0 + block_heads)]
            for half in range(2):
                qb = jnp.concatenate(
                    [q_ref[:, (2 * g + half) * DIFF_DIM:(2 * g + half + 1) * DIFF_DIM] for g in heads], axis=0)
                s_scrs[half][base:base + rows, :] = lax.dot_general(
                    qb, k_h[:, half * DIFF_DIM:(half + 1) * DIFF_DIM], NT_DIMS, preferred_element_type=F32)
            for r in range(0, rows, ROW_CHUNK):
                rs = slice(base + r, base + r + ROW_CHUNK)
                s0 = s_scrs[0][rs, :]
                e0 = jnp.exp2(s0 - jnp.max(s0, axis=-1, keepdims=True))
                r0 = 1.0 / jnp.sum(e0, axis=-1, keepdims=True)
                s1 = s_scrs[1][rs, :]
                e1 = jnp.exp2(s1 - jnp.max(s1, axis=-1, keepdims=True))
                r1 = lam / jnp.sum(e1, axis=-1, keepdims=True)
                p_scr[rs, :] = (e0 * r0 - e1 * r1).astype(BF16)
            acc = jnp.dot(p_scr[base:base + rows, :], v_h, preferred_element_type=F32)
            y = (_rms(acc) * gs_ref[...]) * (1.0 - lam_init)
            for j, g in enumerate(heads):
                o_ref[:, g * width:(g + 1) * width] = y[j * tq:(j + 1) * tq, :].astype(o_ref.dtype)
            base += rows


def _diff_attention(q, k, v, cache, slot, lam_vecs, g_subln, *, batch, seq, tq, kv_heads, lam_init):
    nq_tiles = seq // tq
    head_rows = cache is None
    rows_per_key = 2 * DIFF_KV_HEADS
    n_ctx = 0 if cache is None else cache[0].shape[2] // rows_per_key
    assert kv_heads == 1
    qw = DIFF_GROUP * 2 * DIFF_DIM
    kw = 2 * DIFF_DIM
    operands = [a.reshape(a.shape[0], 1, DIFF_DIM) for a in lam_vecs]
    in_specs = [pl.BlockSpec((None, 1, DIFF_DIM), lambda b, h, i: (slot, 0, 0))] * 4
    operands.append(g_subln.reshape(g_subln.shape[0], 1, 2 * DIFF_DIM))
    in_specs.append(pl.BlockSpec((None, 1, 2 * DIFF_DIM), lambda b, h, i: (slot, 0, 0)))
    operands += [q, k, v]
    in_specs.append(pl.BlockSpec((tq, qw), lambda b, h, i: (b * nq_tiles + i, h)))
    picked = []
    if head_rows:
        in_specs += [pl.BlockSpec((None, None, seq * rows_per_key, LANES), lambda b, h, i: (b, slot, 0, 0))] * 2
        picked.append(seq)
    else:
        in_specs += [pl.BlockSpec((seq, kw), lambda b, h, i: (b, h))] * 2
        operands += list(cache)
        in_specs += [pl.BlockSpec((None, None, n_ctx * rows_per_key, LANES), lambda b, h, i: (b, slot, 0, 0))] * 2
        picked.append(n_ctx)
    kern = functools.partial(_diff_attn_kernel, tq=tq, kv_heads=kv_heads, block_heads=2, n_new=seq, n_ctx=n_ctx,
                             lam_init=lam_init, head_rows=head_rows)
    score_shape = (kv_heads * DIFF_GROUP * tq, seq + n_ctx)
    ctx_scratch = [pltpu.VMEM((n, 2 * DIFF_DIM), BF16) for n in picked for _ in range(2)]
    return pl.pallas_call(
        kern,
        out_shape=jax.ShapeDtypeStruct(q.shape, BF16),
        grid=(batch, DIFF_KV_HEADS // kv_heads, nq_tiles),
        in_specs=in_specs,
        out_specs=pl.BlockSpec((tq, qw), lambda b, h, i: (b * nq_tiles + i, h)),
        scratch_shapes=[pltpu.VMEM(score_shape, F32), pltpu.VMEM(score_shape, F32),
                        pltpu.VMEM(score_shape, BF16)] + ctx_scratch,
        compiler_params=_params("parallel", "parallel", "arbitrary"),
        name="diff_attn",
    )(*operands)


def _oproj_kernel(o_ref, w_ref, x_ref, mod_ref, g_ref, out_ref, h_ref, *, d, row_blocks, row_of_tile):
    mod = mod_ref[pl.ds(row_of_tile(pl.program_id(0)), 1), :]
    gain1 = mod[:, 2 * d:3 * d] * g_ref[1:2, :]
    gain2 = g_ref[2:3, :] * (1.0 + mod[:, 4 * d:5 * d])
    shift2 = mod[:, 3 * d:4 * d]
    rows = o_ref.shape[0] // row_blocks
    for r in range(row_blocks):
        rs = slice(r * rows, (r + 1) * rows)
        m = jnp.dot(o_ref[rs, :], w_ref[...], preferred_element_type=F32)
        x_new = x_ref[rs, :] + _rms(m) * gain1
        out_ref[rs, :] = x_new
        h_ref[rs, :] = (_rms(x_new) * gain2 + shift2).astype(BF16)


def _oproj(o, w_o, x, mod, g_norm, layer, slot, row_of_tile, *, tm):
    t, d = x.shape
    nin = o.shape[1]
    return pl.pallas_call(
        functools.partial(_oproj_kernel, d=d, row_blocks=4 if tm % (4 * ROW_CHUNK) == 0 else 1,
                          row_of_tile=row_of_tile),
        out_shape=(jax.ShapeDtypeStruct((t, d), F32), jax.ShapeDtypeStruct((t, d), BF16)),
        grid=(t // tm,),
        in_specs=[
            pl.BlockSpec((tm, nin), lambda i: (i, 0)),
            pl.BlockSpec((None, nin, d), lambda i: (slot, 0, 0), pipeline_mode=pl.Buffered(1)),
            pl.BlockSpec((tm, d), lambda i: (i, 0)),
            pl.BlockSpec((None,) + mod.shape[1:], lambda i: (layer, 0, 0)),
            pl.BlockSpec((None, 4, d), lambda i: (layer, 0, 0)),
        ],
        out_specs=(pl.BlockSpec((tm, d), lambda i: (i, 0)), pl.BlockSpec((tm, d), lambda i: (i, 0))),
        compiler_params=_params("parallel"),
        name="out_proj",
    )(o, w_o, x, mod, g_norm)


def _ffn_kernel(h_ref, x_ref, mod_ref, g_ref, wg_ref, wu_ref, wd_ref, out_ref, *, d, row_of_tile):
    j = pl.program_id(1)
    last = pl.num_programs(1) - 1

    def partial_sum(rows):
        h = h_ref[rows, :]
        a = jnp.dot(h, wg_ref[...], preferred_element_type=F32)
        b = jnp.dot(h, wu_ref[...], preferred_element_type=F32)
        u = ((a * jax.nn.sigmoid(a)) * b).astype(BF16)
        return jnp.dot(u, wd_ref[...], preferred_element_type=F32)

    @pl.when(j == 0)
    def _():
        out_ref[...] = partial_sum(slice(None))

    @pl.when(jnp.logical_and(j > 0, j < last))
    def _():
        out_ref[...] += partial_sum(slice(None))

    @pl.when(j == last)
    def _():
        gate = mod_ref[pl.ds(row_of_tile(pl.program_id(0)), 1), 5 * d:6 * d]
        gain = gate * g_ref[3:4, :]
        rows = out_ref.shape[0] // FFN_EPILOGUE_BLOCKS
        for r in range(FFN_EPILOGUE_BLOCKS):
            rs = slice(r * rows, (r + 1) * rows)
            acc = out_ref[rs, :] + partial_sum(rs)
            out_ref[rs, :] = x_ref[rs, :] + _rms(acc) * gain


def _ffn(h, x, mod, g_norm, w_gate, w_up, w_down, layer, row_of_tile, *, tm, fc):
    t, d = x.shape
    d_ff = w_gate.shape[-1]
    return pl.pallas_call(
        functools.partial(_ffn_kernel, d=d, row_of_tile=row_of_tile),
        out_shape=jax.ShapeDtypeStruct((t, d), F32),
        grid=(t // tm, d_ff // fc),
        in_specs=[
            pl.BlockSpec((tm, d), lambda i, j: (i, 0)),
            pl.BlockSpec((tm, d), lambda i, j: (i, 0)),
            pl.BlockSpec((None,) + mod.shape[1:], lambda i, j: (layer, 0, 0)),
            pl.BlockSpec((None, 4, d), lambda i, j: (layer, 0, 0)),
            pl.BlockSpec((None, d, fc), lambda i, j: (layer, 0, j)),
            pl.BlockSpec((None, d, fc), lambda i, j: (layer, 0, j)),
            pl.BlockSpec((None, fc, d), lambda i, j: (layer, j, 0)),
        ],
        out_specs=pl.BlockSpec((tm, d), lambda i, j: (i, 0)),
        compiler_params=_params("parallel", "arbitrary"),
        name="ffn",
    )(h, x, mod, g_norm, w_gate, w_up, w_down)


def _rope_tables(n_tokens):
    nf = HEAD_DIM // 4
    t = jnp.arange(n_tokens, dtype=jnp.int32)
    rows = (t // GRID_W).astype(F32)
    cols = (t % GRID_W).astype(F32)
    inv = 1.0 / (ROPE_THETA ** (jnp.arange(nf, dtype=F32) / nf))
    ar = rows[:, None] * inv
    ac = cols[:, None] * inv
    cos = jnp.concatenate([jnp.cos(ar), jnp.cos(ar), jnp.cos(ac), jnp.cos(ac)], axis=-1)
    sin = jnp.concatenate([-jnp.sin(ar), jnp.sin(ar), -jnp.sin(ac), jnp.sin(ac)], axis=-1)
    return cos, sin


def _diff_lambda_init(layer):
    return 0.8 - 0.6 * math.exp(-0.3 * layer)


def _largest_tile(n, cap):
    t = cap
    while n % t:
        t //= 2
    return t


def _run_path(x3, first_row, rows_per_batch, caches, mod, p, depth):
    batch, seq, d = x3.shape
    latent = caches is not None
    x = x3.reshape(batch * seq, d)
    tm = _largest_tile(seq if latent else batch * seq, 512)
    tiles_per_batch = seq // tm if latent else None

    if latent:
        def row_of_tile(i):
            return first_row + (i // tiles_per_batch) * rows_per_batch
    else:
        def row_of_tile(i):
            return first_row

    rope_tables = _rope_tables(seq) if latent else None
    new_state = [None] * N_MIXERS
    for layer in range(depth):
        kind, slot = layer % N_MIXERS, layer // N_MIXERS
        cache = None if not latent else (caches[2 * kind], caches[2 * kind + 1])
        common = dict(tm=tm, rope_tables=rope_tables)
        if not latent:
            n_slots = (depth - kind + N_MIXERS - 1) // N_MIXERS
            common.update(state_seq=seq, state_slots=n_slots, state_prev=new_state[kind])
        attn_shape = dict(kv_heads=1 if latent else KV_HEADS)
        if kind == 0:
            q, k, v = _qkv(x, mod, p['g_norm'], p['w_qkv_full'], layer, slot, row_of_tile,
                           nq=N_HEADS * HEAD_DIM, nk=KV_HEADS * HEAD_DIM,
                           gains=(p['g_q_full'], p['g_k_full']), **common)
            o = _gqa_attention(q, k, v, cache, slot, None, batch=batch, seq=seq, band=False,
                               tq=_largest_tile(seq, 512), block_rows=ATTN_BLOCK_ROWS, **attn_shape)
            w_o = p['w_o_full']
        elif kind == 1:
            q, k, v = _qkv(x, mod, p['g_norm'], p['w_qkv_win'], layer, slot, row_of_tile,
                           nq=N_HEADS * HEAD_DIM, nk=KV_HEADS * HEAD_DIM, **common)
            o = _gqa_attention(q, k, v, cache, slot, p['sink_win'][slot], batch=batch, seq=seq, band=latent,
                               tq=_largest_tile(seq, 256),
                               block_rows=2 * ATTN_BLOCK_ROWS if latent else ATTN_BLOCK_ROWS, **attn_shape)
            w_o = p['w_o_win']
        else:
            q, k, v = _qkv(x, mod, p['g_norm'], p['w_qkv_diff'], layer, slot, row_of_tile,
                           nq=DIFF_HEADS * 2 * DIFF_DIM, nk=DIFF_KV_HEADS * 2 * DIFF_DIM, **common)
            o = _diff_attention(q, k, v, cache, slot,
                                (p['lam_q1'], p['lam_k1'], p['lam_q2'], p['lam_k2']), p['g_subln_diff'],
                                batch=batch, seq=seq, tq=_largest_tile(seq, 256),
                                kv_heads=1,
                                lam_init=_diff_lambda_init(layer))
            w_o = p['w_o_diff']
        if not latent:
            new_state[kind] = (k, v)
        x, h = _oproj(o, w_o, x, mod, p['g_norm'], layer, slot, row_of_tile, tm=tm)
        x = _ffn(h, x, mod, p['g_norm'], p['w_gate'], p['w_up'], p['w_down'], layer, row_of_tile,
                 tm=tm, fc=FFN_CHUNK)
    return x.reshape(batch, seq, d), new_state


def kernel(x_prompt, x_sample, c, cache_k_full, cache_v_full, cache_k_win, cache_v_win, cache_k_diff,
           cache_v_diff, c_ctx, w_ada, b_ada, g_norm, w_qkv_full, w_o_full, g_q_full, g_k_full, w_qkv_win,
           w_o_win, sink_win, w_qkv_diff, w_o_diff, lam_q1, lam_k1, lam_q2, lam_k2, g_subln_diff, w_gate,
           w_up, w_down):
    depth, d = w_ada.shape[0], w_ada.shape[1]
    batch, seq, _ = x_prompt.shape
    dec_batch, dec_seq, _ = x_sample.shape

    p = {
        'g_norm': g_norm,
        'w_qkv_full': w_qkv_full.astype(BF16), 'w_o_full': w_o_full.astype(BF16),
        'g_q_full': g_q_full, 'g_k_full': g_k_full,
        'w_qkv_win': w_qkv_win.astype(BF16), 'w_o_win': w_o_win.astype(BF16), 'sink_win': sink_win,
        'w_qkv_diff': w_qkv_diff.astype(BF16), 'w_o_diff': w_o_diff.astype(BF16),
        'lam_q1': lam_q1, 'lam_k1': lam_k1, 'lam_q2': lam_q2, 'lam_k2': lam_k2,
        'g_subln_diff': g_subln_diff,
        'w_gate': w_gate.astype(BF16), 'w_up': w_up.astype(BF16), 'w_down': w_down.astype(BF16),
    }

    n_rows = -(-(1 + dec_batch) // SUBLANES) * SUBLANES
    cond = jnp.zeros((n_rows, d), F32).at[0].set(c_ctx).at[1:1 + dec_batch].set(c)
    mod = _modulation(cond, w_ada, b_ada)

    y_prompt, new_state = _run_path(x_prompt, 0, 0, None, mod, p, depth)

    def head_rows_cache(a):
        return a.reshape(a.shape[0], a.shape[1], -1, LANES)

    caches = tuple(head_rows_cache(a) for a in (cache_k_full, cache_v_full, cache_k_win, cache_v_win,
                                                cache_k_diff, cache_v_diff))
    y_sample, _ = _run_path(x_sample, 1, 1, caches, mod, p, depth)

    def state(a, tail):
        return a.reshape(a.shape[:2] + (seq,) + tail)

    (k_full, v_full), (k_win, v_win), (k_diff, v_diff) = new_state
    return (y_prompt, y_sample,
            state(k_full, (KV_HEADS, HEAD_DIM)), state(v_full, (KV_HEADS, HEAD_DIM)),
            state(k_win, (KV_HEADS, HEAD_DIM)), state(v_win, (KV_HEADS, HEAD_DIM)),
            state(k_diff, (DIFF_KV_HEADS, 2, DIFF_DIM)), state(v_diff, (DIFF_KV_HEADS, 2 * DIFF_DIM)))
```

```python
import functools
import math

import jax
import jax.numpy as jnp
from jax import lax
from jax.experimental import pallas as pl
from jax.experimental.pallas import tpu as pltpu

F32 = jnp.float32
BF16 = jnp.bfloat16

N_MIXERS = 3
N_HEADS = 16
KV_HEADS = 4
HEAD_DIM = 128
GROUP = N_HEADS // KV_HEADS
WINDOW = 128
GRID_W = 64
DIFF_HEADS = 8
DIFF_KV_HEADS = 2
DIFF_GROUP = DIFF_HEADS // DIFF_KV_HEADS
DIFF_DIM = 128
ROPE_THETA = 10000.0
EPS = 1e-6
NEG_INF = -1e30

LANES = 128
SUBLANES = 8
V7X_VMEM_BYTES = 64 * 1024 * 1024
VMEM_LIMIT = V7X_VMEM_BYTES - 8 * 1024 * 1024

NT_DIMS = (((1,), (1,)), ((), ()))
LOG2E = math.log2(math.e)
ROW_CHUNK = 2 * SUBLANES
ATTN_BLOCK_ROWS = 256
FFN_CHUNK = 512
FFN_EPILOGUE_BLOCKS = 2


def _params(*semantics):
    return pltpu.CompilerParams(dimension_semantics=semantics, vmem_limit_bytes=VMEM_LIMIT)


def _rms(x):
    return x * lax.rsqrt(jnp.mean(x * x, axis=-1, keepdims=True) + EPS)


def _mod_kernel(c_ref, w_ref, b_ref, o_ref):
    c = c_ref[...]
    a = (c * jax.nn.sigmoid(c)).astype(BF16)
    o_ref[...] = jnp.dot(a, w_ref[...].astype(BF16), preferred_element_type=F32) + b_ref[...]


def _modulation(cond, w_ada, b_ada):
    n_layers, d, n = w_ada.shape
    rows = cond.shape[0]
    tn = 1024
    return pl.pallas_call(
        _mod_kernel,
        out_shape=jax.ShapeDtypeStruct((n_layers, rows, n), F32),
        grid=(n_layers, n // tn),
        in_specs=[
            pl.BlockSpec((rows, d), lambda l, j: (0, 0)),
            pl.BlockSpec((None, d, tn), lambda l, j: (l, 0, j)),
            pl.BlockSpec((None, 1, tn), lambda l, j: (l, 0, j)),
        ],
        out_specs=pl.BlockSpec((None, rows, tn), lambda l, j: (l, 0, j)),
        compiler_params=_params("parallel", "parallel"),
        name="adaln_mod",
    )(cond, w_ada, b_ada.reshape(n_layers, 1, n))


def _qkv_kernel(*refs, d, nq, nk, qk_norm, rope, q_scale, row_of_tile, head_rows_seq, n_aliased):
    it = iter(refs)
    x_ref, mod_ref, g_ref, w_ref = next(it), next(it), next(it), next(it)
    gq_ref = gk_ref = cos_ref = sin_ref = None
    if qk_norm:
        gq_ref, gk_ref = next(it), next(it)
    if rope:
        cos_ref, sin_ref = next(it), next(it)
    for _ in range(n_aliased):
        next(it)
    q_ref, k_ref, v_ref = next(it), next(it), next(it)

    def store_chunk(ref, chunk, n_chunks, z):
        if head_rows_seq is None:
            ref[:, chunk * LANES:(chunk + 1) * LANES] = z.astype(ref.dtype)
        else:
            seq = head_rows_seq
            for b in range(z.shape[0] // seq):
                ref[b, pl.ds(chunk, seq, stride=n_chunks), :] = z[b * seq:(b + 1) * seq, :].astype(ref.dtype)

    mod = mod_ref[pl.ds(row_of_tile(pl.program_id(0)), 1), :]
    h = (_rms(x_ref[...]) * (g_ref[0:1, :] * (1.0 + mod[:, d:2 * d])) + mod[:, 0:d]).astype(BF16)

    def partner(a):
        lane = lax.broadcasted_iota(jnp.int32, a.shape, 1)
        return jnp.where((lane & 32) != 0, pltpu.roll(a, 32, 1), pltpu.roll(a, LANES - 32, 1))

    def lane_factors(gain_ref, scale):
        gain = None if gain_ref is None else jnp.broadcast_to(gain_ref[...] * scale, (SUBLANES, LANES))
        if not rope:
            return (scale if gain is None else gain[0:1, :]), None
        if gain is None:
            return cos_ref[...] * scale, sin_ref[...] * scale
        return cos_ref[...] * gain[0:1, :], sin_ref[...] * partner(gain)[0:1, :]

    def finish(z, normed, factors):
        c, s = factors
        if s is not None:
            t = z * c + partner(z) * s
        else:
            t = z if isinstance(c, float) and c == 1.0 else z * c
        if normed:
            t = t * lax.rsqrt(jnp.mean(z * z, axis=-1, keepdims=True) + EPS)
        return t

    q_factors = lane_factors(gq_ref, q_scale)
    k_factors = lane_factors(gk_ref, 1.0)
    chunk = 512
    for c0 in range(0, nq + nk, chunk):
        acc = jnp.dot(h, w_ref[:, c0:c0 + chunk], preferred_element_type=F32)
        for j in range(chunk // LANES):
            col = c0 + j * LANES
            z = acc[:, j * LANES:(j + 1) * LANES]
            if col < nq:
                q_ref[:, col:col + LANES] = finish(z, qk_norm, q_factors).astype(q_ref.dtype)
            else:
                store_chunk(k_ref, (col - nq) // LANES, nk // LANES, finish(z, qk_norm, k_factors))
    v = jnp.dot(h, w_ref[:, nq + nk:], preferred_element_type=F32)
    n_v_chunks = v.shape[1] // LANES
    for c in range(n_v_chunks):
        store_chunk(v_ref, c, n_v_chunks, v[:, c * LANES:(c + 1) * LANES])


def _qkv(x, mod, g_norm, w_qkv, layer, slot, row_of_tile, *, tm, nq, nk, gains=None, rope_tables=None,
         state_seq=None, state_slots=1, state_prev=None):
    t, d = x.shape
    n_all = w_qkv.shape[-1]
    nv = n_all - nq - nk
    operands = [x, mod, g_norm, w_qkv]
    in_specs = [
        pl.BlockSpec((tm, d), lambda i: (i, 0)),
        pl.BlockSpec((None,) + mod.shape[1:], lambda i: (layer, 0, 0)),
        pl.BlockSpec((None, 4, d), lambda i: (layer, 0, 0)),
        pl.BlockSpec((None, d, n_all), lambda i: (slot, 0, 0), pipeline_mode=pl.Buffered(1)),
    ]
    if gains is not None:
        for g in gains:
            operands.append(g.reshape(g.shape[0], 1, LANES))
            in_specs.append(pl.BlockSpec((None, 1, LANES), lambda i: (slot, 0, 0)))
    if rope_tables is not None:
        tiles_per_seq = rope_tables[0].shape[0] // tm
        for tab in rope_tables:
            operands.append(tab)
            in_specs.append(pl.BlockSpec((tm, LANES), lambda i: (i % tiles_per_seq, 0)))
    aliases = {}
    if state_seq is None:
        kv_shapes = [jax.ShapeDtypeStruct((t, n), BF16) for n in (nk, nv)]
        kv_specs = [pl.BlockSpec((tm, n), lambda i: (i, 0)) for n in (nk, nv)]
    else:
        batches = tm // state_seq
        kv_shapes = [jax.ShapeDtypeStruct((t // state_seq, state_slots, state_seq * (n // LANES), LANES), F32)
                     for n in (nk, nv)]
        kv_specs = [pl.BlockSpec((batches, None) + s.shape[2:], lambda i: (i, slot, 0, 0)) for s in kv_shapes]
        if state_prev is not None:
            for j, prev in enumerate(state_prev):
                aliases[len(operands)] = 1 + j
                operands.append(prev)
                in_specs.append(pl.BlockSpec(memory_space=pl.ANY))
    kern = functools.partial(_qkv_kernel, d=d, nq=nq, nk=nk, qk_norm=gains is not None,
                             rope=rope_tables is not None, q_scale=HEAD_DIM ** -0.5 * LOG2E,
                             row_of_tile=row_of_tile, head_rows_seq=state_seq, n_aliased=len(aliases))
    return pl.pallas_call(
        kern,
        out_shape=[jax.ShapeDtypeStruct((t, nq), BF16)] + kv_shapes,
        grid=(t // tm,),
        in_specs=in_specs,
        out_specs=[pl.BlockSpec((tm, nq), lambda i: (i, 0))] + kv_specs,
        input_output_aliases=aliases,
        compiler_params=_params("parallel"),
        name="qkv_proj",
    )(*operands)


def _gqa_attn_kernel(*refs, tq, kv_heads, block_rows, n_new, n_ctx, band, has_sink, head_rows):
    it = iter(refs)
    sink_ref = next(it) if has_sink else None
    q_ref, k_ref, v_ref = next(it), next(it), next(it)
    kc_ref = vc_ref = None
    if n_ctx:
        kc_ref, vc_ref = next(it), next(it)
    o_ref = next(it)
    s_scr, p_scr = next(it), next(it)
    bias_scr = next(it) if band else None
    if n_ctx:
        kc_scr, vc_scr = next(it), next(it)
        for head in range(KV_HEADS):
            @pl.when(jnp.logical_and(pl.program_id(1) == head, pl.program_id(2) == 0))
            def _():
                kc_scr[...] = kc_ref[pl.ds(head, n_ctx, stride=KV_HEADS), :].astype(BF16)
                vc_scr[...] = vc_ref[pl.ds(head, n_ctx, stride=KV_HEADS), :].astype(BF16)

    band_width = tq + 2 * WINDOW
    q0 = pl.program_id(2) * tq
    if head_rows:
        nk = n_new

        def head_kv(kvh):
            rows = pl.ds(kvh, n_new, stride=KV_HEADS)
            return k_ref[rows, :].astype(BF16), v_ref[rows, :].astype(BF16)
    else:
        if band and n_new > band_width:
            start = pl.multiple_of(jnp.clip(q0 - WINDOW, 0, n_new - band_width), LANES)
            k = k_ref[pl.ds(start, band_width), :]
            v = v_ref[pl.ds(start, band_width), :]
        else:
            start = 0
            k = k_ref[...]
            v = v_ref[...]
        nk = k.shape[0]
        if n_ctx:
            k = jnp.concatenate([k, kc_scr[...]], axis=0)
            v = jnp.concatenate([v, vc_scr[...]], axis=0)

        def head_kv(kvh):
            cols = slice(kvh * HEAD_DIM, (kvh + 1) * HEAD_DIM)
            return k[:, cols], v[:, cols]
    if band:
        qpos = q0 + lax.broadcasted_iota(jnp.int32, (tq, nk), 0)
        kpos = start + lax.broadcasted_iota(jnp.int32, (tq, nk), 1)
        bias_scr[...] = jnp.where(jnp.abs(qpos - kpos) <= WINDOW, 0.0, NEG_INF).astype(F32)
    ones = jnp.ones((nk + n_ctx, HEAD_DIM), BF16)

    if block_rows <= tq:
        blocks = [((h,), r0, block_rows) for h in range(GROUP) for r0 in range(0, tq, block_rows)]
    else:
        stacked = block_rows // tq
        blocks = [(tuple(range(h0, h0 + stacked)), 0, tq) for h0 in range(0, GROUP, stacked)]

    base = 0
    for kvh in range(kv_heads):
        k_h, v_h = head_kv(kvh)
        v_ext = jnp.concatenate([v_h, ones], axis=1)
        for heads, row0, n in blocks:
            rows = len(heads) * n
            cols = [(kvh * GROUP + h) * HEAD_DIM for h in heads]
            qb = jnp.concatenate([q_ref[row0:row0 + n, c:c + HEAD_DIM] for c in cols], axis=0)
            s_scr[base:base + rows, :] = lax.dot_general(qb, k_h, NT_DIMS, preferred_element_type=F32)
            sink_terms = []
            for r in range(0, rows, ROW_CHUNK):
                rs = slice(base + r, base + r + ROW_CHUNK)
                if band:
                    qr = row0 + r % n
                    parts = [s_scr[rs, 0:nk] + bias_scr[qr:qr + ROW_CHUNK, :]]
                    if n_ctx:
                        parts.append(s_scr[rs, nk:])
                else:
                    parts = [s_scr[rs, :]]
                m = functools.reduce(jnp.maximum, [jnp.max(a, axis=-1, keepdims=True) for a in parts])
                if has_sink:
                    head = (pl.program_id(1) * kv_heads + kvh) * GROUP + heads[r // n]
                    sk = sink_ref[head] * LOG2E
                    m = jnp.maximum(m, sk)
                    sink_terms.append(jnp.exp2(sk - jnp.broadcast_to(m, (ROW_CHUNK, HEAD_DIM))))
                col = 0
                for a in parts:
                    p_scr[rs, col:col + a.shape[1]] = jnp.exp2(a - m).astype(BF16)
                    col += a.shape[1]
            acc = jnp.dot(p_scr[base:base + rows, :], v_ext, preferred_element_type=F32)
            denom = acc[:, HEAD_DIM:]
            if has_sink:
                denom = denom + jnp.concatenate(sink_terms, axis=0)
            o = acc[:, :HEAD_DIM] / denom
            for j, c in enumerate(cols):
                o_ref[row0:row0 + n, c:c + HEAD_DIM] = o[j * n:(j + 1) * n, :].astype(o_ref.dtype)
            base += rows


def _gqa_attention(q, k, v, cache, slot, sink, *, batch, seq, tq, kv_heads, block_rows, band):
    nq_tiles = seq // tq
    head_rows = cache is None
    n_ctx = 0 if cache is None else cache[0].shape[2] // KV_HEADS
    assert kv_heads == (KV_HEADS if head_rows else 1)
    qw, kw = kv_heads * GROUP * HEAD_DIM, kv_heads * HEAD_DIM
    operands, in_specs = [], []
    if sink is not None:
        operands.append(sink)
        in_specs.append(pl.BlockSpec(memory_space=pltpu.SMEM))
    operands += [q, k, v]
    in_specs.append(pl.BlockSpec((tq, qw), lambda b, h, i: (b * nq_tiles + i, h)))
    if head_rows:
        in_specs += [pl.BlockSpec((None, None, seq * KV_HEADS, HEAD_DIM), lambda b, h, i: (b, slot, 0, 0))] * 2
    else:
        in_specs += [pl.BlockSpec((seq, kw), lambda b, h, i: (b, h))] * 2
        operands += list(cache)
        in_specs += [pl.BlockSpec((None, None, n_ctx * KV_HEADS, HEAD_DIM), lambda b, h, i: (b, slot, 0, 0))] * 2
    kern = functools.partial(_gqa_attn_kernel, tq=tq, kv_heads=kv_heads, block_rows=block_rows, n_new=seq,
                             n_ctx=n_ctx, band=band, has_sink=sink is not None, head_rows=head_rows)
    n_new_keys = min(seq, tq + 2 * WINDOW) if band else seq
    n_keys = n_new_keys + n_ctx
    scratch = [pltpu.VMEM((kv_heads * GROUP * tq, n_keys), F32), pltpu.VMEM((kv_heads * GROUP * tq, n_keys), BF16)]
    if band:
        scratch.append(pltpu.VMEM((tq, n_new_keys), F32))
    if n_ctx:
        scratch += [pltpu.VMEM((n_ctx, HEAD_DIM), BF16)] * 2
    return pl.pallas_call(
        kern,
        out_shape=jax.ShapeDtypeStruct(q.shape, BF16),
        grid=(batch, KV_HEADS // kv_heads, nq_tiles),
        in_specs=in_specs,
        out_specs=pl.BlockSpec((tq, qw), lambda b, h, i: (b * nq_tiles + i, h)),
        scratch_shapes=scratch,
        compiler_params=_params("parallel", "parallel", "arbitrary"),
        name="gqa_attn",
    )(*operands)


def _diff_attn_kernel(*refs, tq, kv_heads, block_heads, n_new, n_ctx, lam_init, head_rows):
    it = iter(refs)
    lq1_ref, lk1_ref, lq2_ref, lk2_ref, gs_ref = next(it), next(it), next(it), next(it), next(it)
    q_ref, k_ref, v_ref = next(it), next(it), next(it)
    kc_ref = vc_ref = None
    if n_ctx:
        kc_ref, vc_ref = next(it), next(it)
    o_ref = next(it)
    s_scrs = (next(it), next(it))
    p_scr = next(it)

    def pick_head(k_src, v_src, n_keys):
        k_dst, v_dst = next(it), next(it)
        for head in range(DIFF_KV_HEADS):
            @pl.when(jnp.logical_and(pl.program_id(1) == head, pl.program_id(2) == 0))
            def _():
                for half in range(2):
                    rows = pl.ds(2 * head + half, n_keys, stride=2 * DIFF_KV_HEADS)
                    cols = slice(half * DIFF_DIM, (half + 1) * DIFF_DIM)
                    k_dst[:, cols] = k_src[rows, :].astype(BF16)
                    v_dst[:, cols] = v_src[rows, :].astype(BF16)
        return k_dst[...], v_dst[...]

    lam = (jnp.exp(jnp.sum(lq1_ref[...] * lk1_ref[...], axis=-1, keepdims=True))
           - jnp.exp(jnp.sum(lq2_ref[...] * lk2_ref[...], axis=-1, keepdims=True)) + lam_init)
    if head_rows:
        k, v = pick_head(k_ref, v_ref, n_new)
    else:
        k, v = k_ref[...], v_ref[...]
    if n_ctx:
        kc, vc = pick_head(kc_ref, vc_ref, n_ctx)
        k = jnp.concatenate([k, kc], axis=0)
        v = jnp.concatenate([v, vc], axis=0)

    width = 2 * DIFF_DIM
    rows = block_heads * tq
    base = 0
    for kvh in range(kv_heads):
        k_h = k[:, kvh * width:(kvh + 1) * width]
        v_h = v[:, kvh * width:(kvh + 1) * width]
        for h0 in range(0, DIFF_GROUP, block_heads):
            heads = [kvh * DIFF_GROUP + g for g in range(h0, h0 + block_heads)]
            for half in range(2):
                qb = jnp.concatenate(
                    [q_ref[:, (2 * g + half) * DIFF_DIM:(2 * g + half + 1) * DIFF_DIM] for g in heads], axis=0)
                s_scrs[half][base:base + rows, :] = lax.dot_general(
                    qb, k_h[:, half * DIFF_DIM:(half + 1) * DIFF_DIM], NT_DIMS, preferred_element_type=F32)
            for r in range(0, rows, ROW_CHUNK):
                rs = slice(base + r, base + r + ROW_CHUNK)
                s0 = s_scrs[0][rs, :]
                e0 = jnp.exp2(s0 - jnp.max(s0, axis=-1, keepdims=True))
                r0 = 1.0 / jnp.sum(e0, axis=-1, keepdims=True)
                s1 = s_scrs[1][rs, :]
                e1 = jnp.exp2(s1 - jnp.max(s1, axis=-1, keepdims=True))
                r1 = lam / jnp.sum(e1, axis=-1, keepdims=True)
                p_scr[rs, :] = (e0 * r0 - e1 * r1).astype(BF16)
            acc = jnp.dot(p_scr[base:base + rows, :], v_h, preferred_element_type=F32)
            y = (_rms(acc) * gs_ref[...]) * (1.0 - lam_init)
            for j, g in enumerate(heads):
                o_ref[:, g * width:(g + 1) * width] = y[j * tq:(j + 1) * tq, :].astype(o_ref.dtype)
            base += rows


def _diff_attention(q, k, v, cache, slot, lam_vecs, g_subln, *, batch, seq, tq, kv_heads, lam_init):
    nq_tiles = seq // tq
    head_rows = cache is None
    rows_per_key = 2 * DIFF_KV_HEADS
    n_ctx = 0 if cache is None else cache[0].shape[2] // rows_per_key
    assert kv_heads == 1
    qw = DIFF_GROUP * 2 * DIFF_DIM
    kw = 2 * DIFF_DIM
    operands = [a.reshape(a.shape[0], 1, DIFF_DIM) for a in lam_vecs]
    in_specs = [pl.BlockSpec((None, 1, DIFF_DIM), lambda b, h, i: (slot, 0, 0))] * 4
    operands.append(g_subln.reshape(g_subln.shape[0], 1, 2 * DIFF_DIM))
    in_specs.append(pl.BlockSpec((None, 1, 2 * DIFF_DIM), lambda b, h, i: (slot, 0, 0)))
    operands += [q, k, v]
    in_specs.append(pl.BlockSpec((tq, qw), lambda b, h, i: (b * nq_tiles + i, h)))
    picked = []
    if head_rows:
        in_specs += [pl.BlockSpec((None, None, seq * rows_per_key, LANES), lambda b, h, i: (b, slot, 0, 0))] * 2
        picked.append(seq)
    else:
        in_specs += [pl.BlockSpec((seq, kw), lambda b, h, i: (b, h))] * 2
        operands += list(cache)
        in_specs += [pl.BlockSpec((None, None, n_ctx * rows_per_key, LANES), lambda b, h, i: (b, slot, 0, 0))] * 2
        picked.append(n_ctx)
    kern = functools.partial(_diff_attn_kernel, tq=tq, kv_heads=kv_heads, block_heads=2, n_new=seq, n_ctx=n_ctx,
                             lam_init=lam_init, head_rows=head_rows)
    score_shape = (kv_heads * DIFF_GROUP * tq, seq + n_ctx)
    ctx_scratch = [pltpu.VMEM((n, 2 * DIFF_DIM), BF16) for n in picked for _ in range(2)]
    return pl.pallas_call(
        kern,
        out_shape=jax.ShapeDtypeStruct(q.shape, BF16),
        grid=(batch, DIFF_KV_HEADS // kv_heads, nq_tiles),
        in_specs=in_specs,
        out_specs=pl.BlockSpec((tq, qw), lambda b, h, i: (b * nq_tiles + i, h)),
        scratch_shapes=[pltpu.VMEM(score_shape, F32), pltpu.VMEM(score_shape, F32),
                        pltpu.VMEM(score_shape, BF16)] + ctx_scratch,
        compiler_params=_params("parallel", "parallel", "arbitrary"),
        name="diff_attn",
    )(*operands)


def _oproj_kernel(o_ref, w_ref, x_ref, mod_ref, g_ref, out_ref, h_ref, *, d, row_blocks, row_of_tile):
    mod = mod_ref[pl.ds(row_of_tile(pl.program_id(0)), 1), :]
    gain1 = mod[:, 2 * d:3 * d] * g_ref[1:2, :]
    gain2 = g_ref[2:3, :] * (1.0 + mod[:, 4 * d:5 * d])
    shift2 = mod[:, 3 * d:4 * d]
    rows = o_ref.shape[0] // row_blocks
    for r in range(row_blocks):
        rs = slice(r * rows, (r + 1) * rows)
        m = jnp.dot(o_ref[rs, :], w_ref[...], preferred_element_type=F32)
        x_new = x_ref[rs, :] + _rms(m) * gain1
        out_ref[rs, :] = x_new
        h_ref[rs, :] = (_rms(x_new) * gain2 + shift2).astype(BF16)


def _oproj(o, w_o, x, mod, g_norm, layer, slot, row_of_tile, *, tm):
    t, d = x.shape
    nin = o.shape[1]
    return pl.pallas_call(
        functools.partial(_oproj_kernel, d=d, row_blocks=4 if tm % (4 * ROW_CHUNK) == 0 else 1,
                          row_of_tile=row_of_tile),
        out_shape=(jax.ShapeDtypeStruct((t, d), F32), jax.ShapeDtypeStruct((t, d), BF16)),
        grid=(t // tm,),
        in_specs=[
            pl.BlockSpec((tm, nin), lambda i: (i, 0)),
            pl.BlockSpec((None, nin, d), lambda i: (slot, 0, 0), pipeline_mode=pl.Buffered(1)),
            pl.BlockSpec((tm, d), lambda i: (i, 0)),
            pl.BlockSpec((None,) + mod.shape[1:], lambda i: (layer, 0, 0)),
            pl.BlockSpec((None, 4, d), lambda i: (layer, 0, 0)),
        ],
        out_specs=(pl.BlockSpec((tm, d), lambda i: (i, 0)), pl.BlockSpec((tm, d), lambda i: (i, 0))),
        compiler_params=_params("parallel"),
        name="out_proj",
    )(o, w_o, x, mod, g_norm)


def _ffn_kernel(h_ref, x_ref, mod_ref, g_ref, wg_ref, wu_ref, wd_ref, out_ref, *, d, row_of_tile):
    j = pl.program_id(1)
    last = pl.num_programs(1) - 1

    def partial_sum(rows):
        h = h_ref[rows, :]
        a = jnp.dot(h, wg_ref[...], preferred_element_type=F32)
        b = jnp.dot(h, wu_ref[...], preferred_element_type=F32)
        u = ((a * jax.nn.sigmoid(a)) * b).astype(BF16)
        return jnp.dot(u, wd_ref[...], preferred_element_type=F32)

    @pl.when(j == 0)
    def _():
        out_ref[...] = partial_sum(slice(None))

    @pl.when(jnp.logical_and(j > 0, j < last))
    def _():
        out_ref[...] += partial_sum(slice(None))

    @pl.when(j == last)
    def _():
        gate = mod_ref[pl.ds(row_of_tile(pl.program_id(0)), 1), 5 * d:6 * d]
        gain = gate * g_ref[3:4, :]
        rows = out_ref.shape[0] // FFN_EPILOGUE_BLOCKS
        for r in range(FFN_EPILOGUE_BLOCKS):
            rs = slice(r * rows, (r + 1) * rows)
            acc = out_ref[rs, :] + partial_sum(rs)
            out_ref[rs, :] = x_ref[rs, :] + _rms(acc) * gain


def _ffn(h, x, mod, g_norm, w_gate, w_up, w_down, layer, row_of_tile, *, tm, fc):
    t, d = x.shape
    d_ff = w_gate.shape[-1]
    return pl.pallas_call(
        functools.partial(_ffn_kernel, d=d, row_of_tile=row_of_tile),
        out_shape=jax.ShapeDtypeStruct((t, d), F32),
        grid=(t // tm, d_ff // fc),
        in_specs=[
            pl.BlockSpec((tm, d), lambda i, j: (i, 0)),
            pl.BlockSpec((tm, d), lambda i, j: (i, 0)),
            pl.BlockSpec((None,) + mod.shape[1:], lambda i, j: (layer, 0, 0)),
            pl.BlockSpec((None, 4, d), lambda i, j: (layer, 0, 0)),
            pl.BlockSpec((None, d, fc), lambda i, j: (layer, 0, j)),
            pl.BlockSpec((None, d, fc), lambda i, j: (layer, 0, j)),
            pl.BlockSpec((None, fc, d), lambda i, j: (layer, j, 0)),
        ],
        out_specs=pl.BlockSpec((tm, d), lambda i, j: (i, 0)),
        compiler_params=_params("parallel", "arbitrary"),
        name="ffn",
    )(h, x, mod, g_norm, w_gate, w_up, w_down)


def _rope_tables(n_tokens):
    nf = HEAD_DIM // 4
    t = jnp.arange(n_tokens, dtype=jnp.int32)
    rows = (t // GRID_W).astype(F32)
    cols = (t % GRID_W).astype(F32)
    inv = 1.0 / (ROPE_THETA ** (jnp.arange(nf, dtype=F32) / nf))
    ar = rows[:, None] * inv
    ac = cols[:, None] * inv
    cos = jnp.concatenate([jnp.cos(ar), jnp.cos(ar), jnp.cos(ac), jnp.cos(ac)], axis=-1)
    sin = jnp.concatenate([-jnp.sin(ar), jnp.sin(ar), -jnp.sin(ac), jnp.sin(ac)], axis=-1)
    return cos, sin


def _diff_lambda_init(layer):
    return 0.8 - 0.6 * math.exp(-0.3 * layer)


def _largest_tile(n, cap):
    t = cap
    while n % t:
        t //= 2
    return t


def _run_path(x3, first_row, rows_per_batch, caches, mod, p, depth):
    batch, seq, d = x3.shape
    latent = caches is not None
    x = x3.reshape(batch * seq, d)
    tm = _largest_tile(seq if latent else batch * seq, 512)
    tiles_per_batch = seq // tm if latent else None

    if latent:
        def row_of_tile(i):
            return first_row + (i // tiles_per_batch) * rows_per_batch
    else:
        def row_of_tile(i):
            return first_row

    rope_tables = _rope_tables(seq) if latent else None
    new_state = [None] * N_MIXERS
    for layer in range(depth):
        kind, slot = layer % N_MIXERS, layer // N_MIXERS
        cache = None if not latent else (caches[2 * kind], caches[2 * kind + 1])
        common = dict(tm=tm, rope_tables=rope_tables)
        if not latent:
            n_slots = (depth - kind + N_MIXERS - 1) // N_MIXERS
            common.update(state_seq=seq, state_slots=n_slots, state_prev=new_state[kind])
        attn_shape = dict(kv_heads=1 if latent else KV_HEADS)
        if kind == 0:
            q, k, v = _qkv(x, mod, p['g_norm'], p['w_qkv_full'], layer, slot, row_of_tile,
                           nq=N_HEADS * HEAD_DIM, nk=KV_HEADS * HEAD_DIM,
                           gains=(p['g_q_full'], p['g_k_full']), **common)
            o = _gqa_attention(q, k, v, cache, slot, None, batch=batch, seq=seq, band=False,
                               tq=_largest_tile(seq, 512), block_rows=ATTN_BLOCK_ROWS, **attn_shape)
            w_o = p['w_o_full']
        elif kind == 1:
            q, k, v = _qkv(x, mod, p['g_norm'], p['w_qkv_win'], layer, slot, row_of_tile,
                           nq=N_HEADS * HEAD_DIM, nk=KV_HEADS * HEAD_DIM, **common)
            o = _gqa_attention(q, k, v, cache, slot, p['sink_win'][slot], batch=batch, seq=seq, band=latent,
                               tq=_largest_tile(seq, 512 if latent else 256),
                               block_rows=ATTN_BLOCK_ROWS, **attn_shape)
            w_o = p['w_o_win']
        else:
            q, k, v = _qkv(x, mod, p['g_norm'], p['w_qkv_diff'], layer, slot, row_of_tile,
                           nq=DIFF_HEADS * 2 * DIFF_DIM, nk=DIFF_KV_HEADS * 2 * DIFF_DIM, **common)
            o = _diff_attention(q, k, v, cache, slot,
                                (p['lam_q1'], p['lam_k1'], p['lam_q2'], p['lam_k2']), p['g_subln_diff'],
                                batch=batch, seq=seq, tq=_largest_tile(seq, 256),
                                kv_heads=1,
                                lam_init=_diff_lambda_init(layer))
            w_o = p['w_o_diff']
        if not latent:
            new_state[kind] = (k, v)
        x, h = _oproj(o, w_o, x, mod, p['g_norm'], layer, slot, row_of_tile, tm=tm)
        x = _ffn(h, x, mod, p['g_norm'], p['w_gate'], p['w_up'], p['w_down'], layer, row_of_tile,
                 tm=tm, fc=FFN_CHUNK)
    return x.reshape(batch, seq, d), new_state


def kernel(x_prompt, x_sample, c, cache_k_full, cache_v_full, cache_k_win, cache_v_win, cache_k_diff,
           cache_v_diff, c_ctx, w_ada, b_ada, g_norm, w_qkv_full, w_o_full, g_q_full, g_k_full, w_qkv_win,
           w_o_win, sink_win, w_qkv_diff, w_o_diff, lam_q1, lam_k1, lam_q2, lam_k2, g_subln_diff, w_gate,
           w_up, w_down):
    depth, d = w_ada.shape[0], w_ada.shape[1]
    batch, seq, _ = x_prompt.shape
    dec_batch, dec_seq, _ = x_sample.shape

    p = {
        'g_norm': g_norm,
        'w_qkv_full': w_qkv_full.astype(BF16), 'w_o_full': w_o_full.astype(BF16),
        'g_q_full': g_q_full, 'g_k_full': g_k_full,
        'w_qkv_win': w_qkv_win.astype(BF16), 'w_o_win': w_o_win.astype(BF16), 'sink_win': sink_win,
        'w_qkv_diff': w_qkv_diff.astype(BF16), 'w_o_diff': w_o_diff.astype(BF16),
        'lam_q1': lam_q1, 'lam_k1': lam_k1, 'lam_q2': lam_q2, 'lam_k2': lam_k2,
        'g_subln_diff': g_subln_diff,
        'w_gate': w_gate.astype(BF16), 'w_up': w_up.astype(BF16), 'w_down': w_down.astype(BF16),
    }

    n_rows = -(-(1 + dec_batch) // SUBLANES) * SUBLANES
    cond = jnp.zeros((n_rows, d), F32).at[0].set(c_ctx).at[1:1 + dec_batch].set(c)
    mod = _modulation(cond, w_ada, b_ada)

    y_prompt, new_state = _run_path(x_prompt, 0, 0, None, mod, p, depth)

    def head_rows_cache(a):
        return a.reshape(a.shape[0], a.shape[1], -1, LANES)

    caches = tuple(head_rows_cache(a) for a in (cache_k_full, cache_v_full, cache_k_win, cache_v_win,
                                                cache_k_diff, cache_v_diff))
    y_sample, _ = _run_path(x_sample, 1, 1, caches, mod, p, depth)

    def state(a, tail):
        return a.reshape(a.shape[:2] + (seq,) + tail)

    (k_full, v_full), (k_win, v_win), (k_diff, v_diff) = new_state
    return (y_prompt, y_sample,
            state(k_full, (KV_HEADS, HEAD_DIM)), state(v_full, (KV_HEADS, HEAD_DIM)),
            state(k_win, (KV_HEADS, HEAD_DIM)), state(v_win, (KV_HEADS, HEAD_DIM)),
            state(k_diff, (DIFF_KV_HEADS, 2, DIFF_DIM)), state(v_diff, (DIFF_KV_HEADS, 2 * DIFF_DIM)))
```

```python
import functools
import math

import jax
import jax.numpy as jnp
from jax import lax
from jax.experimental import pallas as pl
from jax.experimental.pallas import tpu as pltpu

F32 = jnp.float32
BF16 = jnp.bfloat16

N_MIXERS = 3
N_HEADS = 16
KV_HEADS = 4
HEAD_DIM = 128
GROUP = N_HEADS // KV_HEADS
WINDOW = 128
GRID_W = 64
DIFF_HEADS = 8
DIFF_KV_HEADS = 2
DIFF_GROUP = DIFF_HEADS // DIFF_KV_HEADS
DIFF_DIM = 128
ROPE_THETA = 10000.0
EPS = 1e-6
NEG_INF = -1e30

LANES = 128
SUBLANES = 8
V7X_VMEM_BYTES = 64 * 1024 * 1024
VMEM_LIMIT = V7X_VMEM_BYTES - 8 * 1024 * 1024

NT_DIMS = (((1,), (1,)), ((), ()))
LOG2E = math.log2(math.e)
ROW_CHUNK = 2 * SUBLANES
ATTN_BLOCK_ROWS = 256
FFN_CHUNK = 512
FFN_EPILOGUE_BLOCKS = 2
FFN_DOWN_COLS = 512
FFN_TILE = 1024


def _params(*semantics):
    return pltpu.CompilerParams(dimension_semantics=semantics, vmem_limit_bytes=VMEM_LIMIT)


def _rms(x):
    return x * lax.rsqrt(jnp.mean(x * x, axis=-1, keepdims=True) + EPS)


def _mod_kernel(c_ref, w_ref, b_ref, o_ref):
    c = c_ref[...]
    a = (c * jax.nn.sigmoid(c)).astype(BF16)
    o_ref[...] = jnp.dot(a, w_ref[...].astype(BF16), preferred_element_type=F32) + b_ref[...]


def _modulation(cond, w_ada, b_ada):
    n_layers, d, n = w_ada.shape
    rows = cond.shape[0]
    tn = 1024
    return pl.pallas_call(
        _mod_kernel,
        out_shape=jax.ShapeDtypeStruct((n_layers, rows, n), F32),
        grid=(n_layers, n // tn),
        in_specs=[
            pl.BlockSpec((rows, d), lambda l, j: (0, 0)),
            pl.BlockSpec((None, d, tn), lambda l, j: (l, 0, j)),
            pl.BlockSpec((None, 1, tn), lambda l, j: (l, 0, j)),
        ],
        out_specs=pl.BlockSpec((None, rows, tn), lambda l, j: (l, 0, j)),
        compiler_params=_params("parallel", "parallel"),
        name="adaln_mod",
    )(cond, w_ada, b_ada.reshape(n_layers, 1, n))


def _qkv_kernel(*refs, d, nq, nk, qk_norm, rope, q_scale, row_of_tile, head_rows_seq, n_aliased):
    it = iter(refs)
    x_ref, mod_ref, g_ref, w_ref = next(it), next(it), next(it), next(it)
    gq_ref = gk_ref = cos_ref = sin_ref = None
    if qk_norm:
        gq_ref, gk_ref = next(it), next(it)
    if rope:
        cos_ref, sin_ref = next(it), next(it)
    for _ in range(n_aliased):
        next(it)
    q_ref, k_ref, v_ref = next(it), next(it), next(it)

    def store_chunk(ref, chunk, n_chunks, z):
        if head_rows_seq is None:
            ref[:, chunk * LANES:(chunk + 1) * LANES] = z.astype(ref.dtype)
        else:
            seq = head_rows_seq
            for b in range(z.shape[0] // seq):
                ref[b, pl.ds(chunk, seq, stride=n_chunks), :] = z[b * seq:(b + 1) * seq, :].astype(ref.dtype)

    mod = mod_ref[pl.ds(row_of_tile(pl.program_id(0)), 1), :]
    h = (_rms(x_ref[...]) * (g_ref[0:1, :] * (1.0 + mod[:, d:2 * d])) + mod[:, 0:d]).astype(BF16)

    def partner(a):
        lane = lax.broadcasted_iota(jnp.int32, a.shape, 1)
        return jnp.where((lane & 32) != 0, pltpu.roll(a, 32, 1), pltpu.roll(a, LANES - 32, 1))

    def lane_factors(gain_ref, scale):
        gain = None if gain_ref is None else jnp.broadcast_to(gain_ref[...] * scale, (SUBLANES, LANES))
        if not rope:
            return (scale if gain is None else gain[0:1, :]), None
        if gain is None:
            return cos_ref[...] * scale, sin_ref[...] * scale
        return cos_ref[...] * gain[0:1, :], sin_ref[...] * partner(gain)[0:1, :]

    def finish(z, normed, factors):
        c, s = factors
        if s is not None:
            t = z * c + partner(z) * s
        else:
            t = z if isinstance(c, float) and c == 1.0 else z * c
        if normed:
            t = t * lax.rsqrt(jnp.mean(z * z, axis=-1, keepdims=True) + EPS)
        return t

    q_factors = lane_factors(gq_ref, q_scale)
    k_factors = lane_factors(gk_ref, 1.0)
    chunk = 512
    for c0 in range(0, nq + nk, chunk):
        acc = jnp.dot(h, w_ref[:, c0:c0 + chunk], preferred_element_type=F32)
        for j in range(chunk // LANES):
            col = c0 + j * LANES
            z = acc[:, j * LANES:(j + 1) * LANES]
            if col < nq:
                q_ref[:, col:col + LANES] = finish(z, qk_norm, q_factors).astype(q_ref.dtype)
            else:
                store_chunk(k_ref, (col - nq) // LANES, nk // LANES, finish(z, qk_norm, k_factors))
    v = jnp.dot(h, w_ref[:, nq + nk:], preferred_element_type=F32)
    n_v_chunks = v.shape[1] // LANES
    for c in range(n_v_chunks):
        store_chunk(v_ref, c, n_v_chunks, v[:, c * LANES:(c + 1) * LANES])


def _qkv(x, mod, g_norm, w_qkv, layer, slot, row_of_tile, *, tm, nq, nk, gains=None, rope_tables=None,
         state_seq=None, state_slots=1, state_prev=None):
    t, d = x.shape
    n_all = w_qkv.shape[-1]
    nv = n_all - nq - nk
    operands = [x, mod, g_norm, w_qkv]
    in_specs = [
        pl.BlockSpec((tm, d), lambda i: (i, 0)),
        pl.BlockSpec((None,) + mod.shape[1:], lambda i: (layer, 0, 0)),
        pl.BlockSpec((None, 4, d), lambda i: (layer, 0, 0)),
        pl.BlockSpec((None, d, n_all), lambda i: (slot, 0, 0), pipeline_mode=pl.Buffered(1)),
    ]
    if gains is not None:
        for g in gains:
            operands.append(g.reshape(g.shape[0], 1, LANES))
            in_specs.append(pl.BlockSpec((None, 1, LANES), lambda i: (slot, 0, 0)))
    if rope_tables is not None:
        tiles_per_seq = rope_tables[0].shape[0] // tm
        for tab in rope_tables:
            operands.append(tab)
            in_specs.append(pl.BlockSpec((tm, LANES), lambda i: (i % tiles_per_seq, 0)))
    aliases = {}
    if state_seq is None:
        kv_shapes = [jax.ShapeDtypeStruct((t, n), BF16) for n in (nk, nv)]
        kv_specs = [pl.BlockSpec((tm, n), lambda i: (i, 0)) for n in (nk, nv)]
    else:
        batches = tm // state_seq
        kv_shapes = [jax.ShapeDtypeStruct((t // state_seq, state_slots, state_seq * (n // LANES), LANES), F32)
                     for n in (nk, nv)]
        kv_specs = [pl.BlockSpec((batches, None) + s.shape[2:], lambda i: (i, slot, 0, 0)) for s in kv_shapes]
        if state_prev is not None:
            for j, prev in enumerate(state_prev):
                aliases[len(operands)] = 1 + j
                operands.append(prev)
                in_specs.append(pl.BlockSpec(memory_space=pl.ANY))
    kern = functools.partial(_qkv_kernel, d=d, nq=nq, nk=nk, qk_norm=gains is not None,
                             rope=rope_tables is not None, q_scale=HEAD_DIM ** -0.5 * LOG2E,
                             row_of_tile=row_of_tile, head_rows_seq=state_seq, n_aliased=len(aliases))
    return pl.pallas_call(
        kern,
        out_shape=[jax.ShapeDtypeStruct((t, nq), BF16)] + kv_shapes,
        grid=(t // tm,),
        in_specs=in_specs,
        out_specs=[pl.BlockSpec((tm, nq), lambda i: (i, 0))] + kv_specs,
        input_output_aliases=aliases,
        compiler_params=_params("parallel"),
        name="qkv_proj",
    )(*operands)


def _gqa_attn_kernel(*refs, tq, kv_heads, block_rows, n_new, n_ctx, band, has_sink, head_rows):
    it = iter(refs)
    sink_ref = next(it) if has_sink else None
    q_ref, k_ref, v_ref = next(it), next(it), next(it)
    kc_ref = vc_ref = None
    if n_ctx:
        kc_ref, vc_ref = next(it), next(it)
    o_ref = next(it)
    s_scr, p_scr = next(it), next(it)
    bias_scr = next(it) if band else None
    if n_ctx:
        kc_scr, vc_scr = next(it), next(it)
        for head in range(KV_HEADS):
            @pl.when(jnp.logical_and(pl.program_id(1) == head, pl.program_id(2) == 0))
            def _():
                kc_scr[...] = kc_ref[pl.ds(head, n_ctx, stride=KV_HEADS), :].astype(BF16)
                vc_scr[...] = vc_ref[pl.ds(head, n_ctx, stride=KV_HEADS), :].astype(BF16)

    band_width = tq + 2 * WINDOW
    q0 = pl.program_id(2) * tq
    if head_rows:
        nk = n_new

        def head_kv(kvh):
            rows = pl.ds(kvh, n_new, stride=KV_HEADS)
            return k_ref[rows, :].astype(BF16), v_ref[rows, :].astype(BF16)
    else:
        if band and n_new > band_width:
            start = pl.multiple_of(jnp.clip(q0 - WINDOW, 0, n_new - band_width), LANES)
            k = k_ref[pl.ds(start, band_width), :]
            v = v_ref[pl.ds(start, band_width), :]
        else:
            start = 0
            k = k_ref[...]
            v = v_ref[...]
        nk = k.shape[0]
        if n_ctx:
            k = jnp.concatenate([k, kc_scr[...]], axis=0)
            v = jnp.concatenate([v, vc_scr[...]], axis=0)

        def head_kv(kvh):
            cols = slice(kvh * HEAD_DIM, (kvh + 1) * HEAD_DIM)
            return k[:, cols], v[:, cols]
    if band:
        qpos = q0 + lax.broadcasted_iota(jnp.int32, (tq, nk), 0)
        kpos = start + lax.broadcasted_iota(jnp.int32, (tq, nk), 1)
        bias_scr[...] = jnp.where(jnp.abs(qpos - kpos) <= WINDOW, 0.0, NEG_INF).astype(F32)
    ones = jnp.ones((nk + n_ctx, HEAD_DIM), BF16)

    if block_rows <= tq:
        blocks = [((h,), r0, block_rows) for h in range(GROUP) for r0 in range(0, tq, block_rows)]
    else:
        stacked = block_rows // tq
        blocks = [(tuple(range(h0, h0 + stacked)), 0, tq) for h0 in range(0, GROUP, stacked)]

    base = 0
    for kvh in range(kv_heads):
        k_h, v_h = head_kv(kvh)
        v_ext = jnp.concatenate([v_h, ones], axis=1)
        for heads, row0, n in blocks:
            rows = len(heads) * n
            cols = [(kvh * GROUP + h) * HEAD_DIM for h in heads]
            qb = jnp.concatenate([q_ref[row0:row0 + n, c:c + HEAD_DIM] for c in cols], axis=0)
            s_scr[base:base + rows, :] = lax.dot_general(qb, k_h, NT_DIMS, preferred_element_type=F32)
            sink_terms = []
            for r in range(0, rows, ROW_CHUNK):
                rs = slice(base + r, base + r + ROW_CHUNK)
                if band:
                    qr = row0 + r % n
                    parts = [s_scr[rs, 0:nk] + bias_scr[qr:qr + ROW_CHUNK, :]]
                    if n_ctx:
                        parts.append(s_scr[rs, nk:])
                else:
                    parts = [s_scr[rs, :]]
                m = functools.reduce(jnp.maximum, [jnp.max(a, axis=-1, keepdims=True) for a in parts])
                if has_sink:
                    head = (pl.program_id(1) * kv_heads + kvh) * GROUP + heads[r // n]
                    sk = sink_ref[head] * LOG2E
                    m = jnp.maximum(m, sk)
                    sink_terms.append(jnp.exp2(sk - jnp.broadcast_to(m, (ROW_CHUNK, HEAD_DIM))))
                col = 0
                for a in parts:
                    p_scr[rs, col:col + a.shape[1]] = jnp.exp2(a - m).astype(BF16)
                    col += a.shape[1]
            acc = jnp.dot(p_scr[base:base + rows, :], v_ext, preferred_element_type=F32)
            denom = acc[:, HEAD_DIM:]
            if has_sink:
                denom = denom + jnp.concatenate(sink_terms, axis=0)
            o = acc[:, :HEAD_DIM] / denom
            for j, c in enumerate(cols):
                o_ref[row0:row0 + n, c:c + HEAD_DIM] = o[j * n:(j + 1) * n, :].astype(o_ref.dtype)
            base += rows


def _gqa_attention(q, k, v, cache, slot, sink, *, batch, seq, tq, kv_heads, block_rows, band):
    nq_tiles = seq // tq
    head_rows = cache is None
    n_ctx = 0 if cache is None else cache[0].shape[2] // KV_HEADS
    assert kv_heads == (KV_HEADS if head_rows else 1)
    qw, kw = kv_heads * GROUP * HEAD_DIM, kv_heads * HEAD_DIM
    operands, in_specs = [], []
    if sink is not None:
        operands.append(sink)
        in_specs.append(pl.BlockSpec(memory_space=pltpu.SMEM))
    operands += [q, k, v]
    in_specs.append(pl.BlockSpec((tq, qw), lambda b, h, i: (b * nq_tiles + i, h)))
    if head_rows:
        in_specs += [pl.BlockSpec((None, None, seq * KV_HEADS, HEAD_DIM), lambda b, h, i: (b, slot, 0, 0))] * 2
    else:
        in_specs += [pl.BlockSpec((seq, kw), lambda b, h, i: (b, h))] * 2
        operands += list(cache)
        in_specs += [pl.BlockSpec((None, None, n_ctx * KV_HEADS, HEAD_DIM), lambda b, h, i: (b, slot, 0, 0))] * 2
    kern = functools.partial(_gqa_attn_kernel, tq=tq, kv_heads=kv_heads, block_rows=block_rows, n_new=seq,
                             n_ctx=n_ctx, band=band, has_sink=sink is not None, head_rows=head_rows)
    n_new_keys = min(seq, tq + 2 * WINDOW) if band else seq
    n_keys = n_new_keys + n_ctx
    scratch = [pltpu.VMEM((kv_heads * GROUP * tq, n_keys), F32), pltpu.VMEM((kv_heads * GROUP * tq, n_keys), BF16)]
    if band:
        scratch.append(pltpu.VMEM((tq, n_new_keys), F32))
    if n_ctx:
        scratch += [pltpu.VMEM((n_ctx, HEAD_DIM), BF16)] * 2
    return pl.pallas_call(
        kern,
        out_shape=jax.ShapeDtypeStruct(q.shape, BF16),
        grid=(batch, KV_HEADS // kv_heads, nq_tiles),
        in_specs=in_specs,
        out_specs=pl.BlockSpec((tq, qw), lambda b, h, i: (b * nq_tiles + i, h)),
        scratch_shapes=scratch,
        compiler_params=_params("parallel", "parallel", "arbitrary"),
        name="gqa_attn",
    )(*operands)


def _diff_attn_kernel(*refs, tq, kv_heads, block_heads, n_new, n_ctx, lam_init, head_rows):
    it = iter(refs)
    lq1_ref, lk1_ref, lq2_ref, lk2_ref, gs_ref = next(it), next(it), next(it), next(it), next(it)
    q_ref, k_ref, v_ref = next(it), next(it), next(it)
    kc_ref = vc_ref = None
    if n_ctx:
        kc_ref, vc_ref = next(it), next(it)
    o_ref = next(it)
    s_scrs = (next(it), next(it))
    p_scr = next(it)

    def pick_head(k_src, v_src, n_keys):
        k_dst, v_dst = next(it), next(it)
        for head in range(DIFF_KV_HEADS):
            @pl.when(jnp.logical_and(pl.program_id(1) == head, pl.program_id(2) == 0))
            def _():
                for half in range(2):
                    rows = pl.ds(2 * head + half, n_keys, stride=2 * DIFF_KV_HEADS)
                    cols = slice(half * DIFF_DIM, (half + 1) * DIFF_DIM)
                    k_dst[:, cols] = k_src[rows, :].astype(BF16)
                    v_dst[:, cols] = v_src[rows, :].astype(BF16)
        return k_dst[...], v_dst[...]

    lam = (jnp.exp(jnp.sum(lq1_ref[...] * lk1_ref[...], axis=-1, keepdims=True))
           - jnp.exp(jnp.sum(lq2_ref[...] * lk2_ref[...], axis=-1, keepdims=True)) + lam_init)
    if head_rows:
        k, v = pick_head(k_ref, v_ref, n_new)
    else:
        k, v = k_ref[...], v_ref[...]
    if n_ctx:
        kc, vc = pick_head(kc_ref, vc_ref, n_ctx)
        k = jnp.concatenate([k, kc], axis=0)
        v = jnp.concatenate([v, vc], axis=0)

    width = 2 * DIFF_DIM
    rows = block_heads * tq
    base = 0
    for kvh in range(kv_heads):
        k_h = k[:, kvh * width:(kvh + 1) * width]
        v_h = v[:, kvh * width:(kvh + 1) * width]
        for h0 in range(0, DIFF_GROUP, block_heads):
            heads = [kvh * DIFF_GROUP + g for g in range(h0, h0 + block_heads)]
            for half in range(2):
                qb = jnp.concatenate(
                    [q_ref[:, (2 * g + half) * DIFF_DIM:(2 * g + half + 1) * DIFF_DIM] for g in heads], axis=0)
                s_scrs[half][base:base + rows, :] = lax.dot_general(
                    qb, k_h[:, half * DIFF_DIM:(half + 1) * DIFF_DIM], NT_DIMS, preferred_element_type=F32)
            for r in range(0, rows, ROW_CHUNK):
                rs = slice(base + r, base + r + ROW_CHUNK)
                s0 = s_scrs[0][rs, :]
                e0 = jnp.exp2(s0 - jnp.max(s0, axis=-1, keepdims=True))
                r0 = 1.0 / jnp.sum(e0, axis=-1, keepdims=True)
                s1 = s_scrs[1][rs, :]
                e1 = jnp.exp2(s1 - jnp.max(s1, axis=-1, keepdims=True))
                r1 = lam / jnp.sum(e1, axis=-1, keepdims=True)
                p_scr[rs, :] = (e0 * r0 - e1 * r1).astype(BF16)
            acc = jnp.dot(p_scr[base:base + rows, :], v_h, preferred_element_type=F32)
            y = (_rms(acc) * gs_ref[...]) * (1.0 - lam_init)
            for j, g in enumerate(heads):
                o_ref[:, g * width:(g + 1) * width] = y[j * tq:(j + 1) * tq, :].astype(o_ref.dtype)
            base += rows


def _diff_attention(q, k, v, cache, slot, lam_vecs, g_subln, *, batch, seq, tq, kv_heads, lam_init):
    nq_tiles = seq // tq
    head_rows = cache is None
    rows_per_key = 2 * DIFF_KV_HEADS
    n_ctx = 0 if cache is None else cache[0].shape[2] // rows_per_key
    assert kv_heads == 1
    qw = DIFF_GROUP * 2 * DIFF_DIM
    kw = 2 * DIFF_DIM
    operands = [a.reshape(a.shape[0], 1, DIFF_DIM) for a in lam_vecs]
    in_specs = [pl.BlockSpec((None, 1, DIFF_DIM), lambda b, h, i: (slot, 0, 0))] * 4
    operands.append(g_subln.reshape(g_subln.shape[0], 1, 2 * DIFF_DIM))
    in_specs.append(pl.BlockSpec((None, 1, 2 * DIFF_DIM), lambda b, h, i: (slot, 0, 0)))
    operands += [q, k, v]
    in_specs.append(pl.BlockSpec((tq, qw), lambda b, h, i: (b * nq_tiles + i, h)))
    picked = []
    if head_rows:
        in_specs += [pl.BlockSpec((None, None, seq * rows_per_key, LANES), lambda b, h, i: (b, slot, 0, 0))] * 2
        picked.append(seq)
    else:
        in_specs += [pl.BlockSpec((seq, kw), lambda b, h, i: (b, h))] * 2
        operands += list(cache)
        in_specs += [pl.BlockSpec((None, None, n_ctx * rows_per_key, LANES), lambda b, h, i: (b, slot, 0, 0))] * 2
        picked.append(n_ctx)
    kern = functools.partial(_diff_attn_kernel, tq=tq, kv_heads=kv_heads, block_heads=2, n_new=seq, n_ctx=n_ctx,
                             lam_init=lam_init, head_rows=head_rows)
    score_shape = (kv_heads * DIFF_GROUP * tq, seq + n_ctx)
    ctx_scratch = [pltpu.VMEM((n, 2 * DIFF_DIM), BF16) for n in picked for _ in range(2)]
    return pl.pallas_call(
        kern,
        out_shape=jax.ShapeDtypeStruct(q.shape, BF16),
        grid=(batch, DIFF_KV_HEADS // kv_heads, nq_tiles),
        in_specs=in_specs,
        out_specs=pl.BlockSpec((tq, qw), lambda b, h, i: (b * nq_tiles + i, h)),
        scratch_shapes=[pltpu.VMEM(score_shape, F32), pltpu.VMEM(score_shape, F32),
                        pltpu.VMEM(score_shape, BF16)] + ctx_scratch,
        compiler_params=_params("parallel", "parallel", "arbitrary"),
        name="diff_attn",
    )(*operands)


def _oproj_kernel(o_ref, w_ref, x_ref, mod_ref, g_ref, out_ref, h_ref, *, d, row_blocks, row_of_tile):
    mod = mod_ref[pl.ds(row_of_tile(pl.program_id(0)), 1), :]
    gain1 = mod[:, 2 * d:3 * d] * g_ref[1:2, :]
    gain2 = g_ref[2:3, :] * (1.0 + mod[:, 4 * d:5 * d])
    shift2 = mod[:, 3 * d:4 * d]
    rows = o_ref.shape[0] // row_blocks
    for r in range(row_blocks):
        rs = slice(r * rows, (r + 1) * rows)
        m = jnp.dot(o_ref[rs, :], w_ref[...], preferred_element_type=F32)
        x_new = x_ref[rs, :] + _rms(m) * gain1
        out_ref[rs, :] = x_new
        h_ref[rs, :] = (_rms(x_new) * gain2 + shift2).astype(BF16)


def _oproj(o, w_o, x, mod, g_norm, layer, slot, row_of_tile, *, tm):
    t, d = x.shape
    nin = o.shape[1]
    return pl.pallas_call(
        functools.partial(_oproj_kernel, d=d, row_blocks=4 if tm % (4 * ROW_CHUNK) == 0 else 1,
                          row_of_tile=row_of_tile),
        out_shape=(jax.ShapeDtypeStruct((t, d), F32), jax.ShapeDtypeStruct((t, d), BF16)),
        grid=(t // tm,),
        in_specs=[
            pl.BlockSpec((tm, nin), lambda i: (i, 0)),
            pl.BlockSpec((None, nin, d), lambda i: (slot, 0, 0), pipeline_mode=pl.Buffered(1)),
            pl.BlockSpec((tm, d), lambda i: (i, 0)),
            pl.BlockSpec((None,) + mod.shape[1:], lambda i: (layer, 0, 0)),
            pl.BlockSpec((None, 4, d), lambda i: (layer, 0, 0)),
        ],
        out_specs=(pl.BlockSpec((tm, d), lambda i: (i, 0)), pl.BlockSpec((tm, d), lambda i: (i, 0))),
        compiler_params=_params("parallel"),
        name="out_proj",
    )(o, w_o, x, mod, g_norm)


def _ffn_kernel(h_ref, x_ref, mod_ref, g_ref, wg_ref, wu_ref, wd_ref, out_ref, *, d, row_of_tile):
    j = pl.program_id(1)
    last = pl.num_programs(1) - 1

    def accumulate(rows, first):
        h = h_ref[rows, :]
        a = jnp.dot(h, wg_ref[...], preferred_element_type=F32)
        b = jnp.dot(h, wu_ref[...], preferred_element_type=F32)
        u = ((a * jax.nn.sigmoid(a)) * b).astype(BF16)
        for c0 in range(0, d, FFN_DOWN_COLS):
            cols = slice(c0, c0 + FFN_DOWN_COLS)
            part = jnp.dot(u, wd_ref[:, cols], preferred_element_type=F32)
            if first:
                out_ref[rows, cols] = part
            else:
                out_ref[rows, cols] += part

    @pl.when(j == 0)
    def _():
        accumulate(slice(None), True)

    @pl.when(jnp.logical_and(j > 0, j < last))
    def _():
        accumulate(slice(None), False)

    @pl.when(j == last)
    def _():
        gate = mod_ref[pl.ds(row_of_tile(pl.program_id(0)), 1), 5 * d:6 * d]
        gain = gate * g_ref[3:4, :]
        rows = out_ref.shape[0] // FFN_EPILOGUE_BLOCKS
        for r in range(FFN_EPILOGUE_BLOCKS):
            rs = slice(r * rows, (r + 1) * rows)
            accumulate(rs, False)
            out_ref[rs, :] = x_ref[rs, :] + _rms(out_ref[rs, :]) * gain


def _ffn(h, x, mod, g_norm, w_gate, w_up, w_down, layer, row_of_tile, *, tm, fc):
    t, d = x.shape
    d_ff = w_gate.shape[-1]
    return pl.pallas_call(
        functools.partial(_ffn_kernel, d=d, row_of_tile=row_of_tile),
        out_shape=jax.ShapeDtypeStruct((t, d), F32),
        grid=(t // tm, d_ff // fc),
        in_specs=[
            pl.BlockSpec((tm, d), lambda i, j: (i, 0)),
            pl.BlockSpec((tm, d), lambda i, j: (i, 0), pipeline_mode=pl.Buffered(1)),
            pl.BlockSpec((None,) + mod.shape[1:], lambda i, j: (layer, 0, 0)),
            pl.BlockSpec((None, 4, d), lambda i, j: (layer, 0, 0)),
            pl.BlockSpec((None, d, fc), lambda i, j: (layer, 0, j)),
            pl.BlockSpec((None, d, fc), lambda i, j: (layer, 0, j)),
            pl.BlockSpec((None, fc, d), lambda i, j: (layer, j, 0)),
        ],
        out_specs=pl.BlockSpec((tm, d), lambda i, j: (i, 0)),
        compiler_params=_params("parallel", "arbitrary"),
        name="ffn",
    )(h, x, mod, g_norm, w_gate, w_up, w_down)


def _rope_tables(n_tokens):
    nf = HEAD_DIM // 4
    t = jnp.arange(n_tokens, dtype=jnp.int32)
    rows = (t // GRID_W).astype(F32)
    cols = (t % GRID_W).astype(F32)
    inv = 1.0 / (ROPE_THETA ** (jnp.arange(nf, dtype=F32) / nf))
    ar = rows[:, None] * inv
    ac = cols[:, None] * inv
    cos = jnp.concatenate([jnp.cos(ar), jnp.cos(ar), jnp.cos(ac), jnp.cos(ac)], axis=-1)
    sin = jnp.concatenate([-jnp.sin(ar), jnp.sin(ar), -jnp.sin(ac), jnp.sin(ac)], axis=-1)
    return cos, sin


def _diff_lambda_init(layer):
    return 0.8 - 0.6 * math.exp(-0.3 * layer)


def _largest_tile(n, cap):
    t = cap
    while n % t:
        t //= 2
    return t


def _run_path(x3, first_row, rows_per_batch, caches, mod, p, depth):
    batch, seq, d = x3.shape
    latent = caches is not None
    x = x3.reshape(batch * seq, d)
    tm = _largest_tile(seq if latent else batch * seq, 512)
    tm_ffn = _largest_tile(seq if latent else batch * seq, FFN_TILE)

    def conditioning_row(tile_rows):
        if not latent:
            return lambda i: first_row
        tiles_per_batch = seq // tile_rows
        return lambda i: first_row + (i // tiles_per_batch) * rows_per_batch

    row_of_tile = conditioning_row(tm)

    rope_tables = _rope_tables(seq) if latent else None
    new_state = [None] * N_MIXERS
    for layer in range(depth):
        kind, slot = layer % N_MIXERS, layer // N_MIXERS
        cache = None if not latent else (caches[2 * kind], caches[2 * kind + 1])
        common = dict(tm=tm, rope_tables=rope_tables)
        if not latent:
            n_slots = (depth - kind + N_MIXERS - 1) // N_MIXERS
            common.update(state_seq=seq, state_slots=n_slots, state_prev=new_state[kind])
        attn_shape = dict(kv_heads=1 if latent else KV_HEADS)
        if kind == 0:
            q, k, v = _qkv(x, mod, p['g_norm'], p['w_qkv_full'], layer, slot, row_of_tile,
                           nq=N_HEADS * HEAD_DIM, nk=KV_HEADS * HEAD_DIM,
                           gains=(p['g_q_full'], p['g_k_full']), **common)
            o = _gqa_attention(q, k, v, cache, slot, None, batch=batch, seq=seq, band=False,
                               tq=_largest_tile(seq, 512), block_rows=ATTN_BLOCK_ROWS, **attn_shape)
            w_o = p['w_o_full']
        elif kind == 1:
            q, k, v = _qkv(x, mod, p['g_norm'], p['w_qkv_win'], layer, slot, row_of_tile,
                           nq=N_HEADS * HEAD_DIM, nk=KV_HEADS * HEAD_DIM, **common)
            o = _gqa_attention(q, k, v, cache, slot, p['sink_win'][slot], batch=batch, seq=seq, band=latent,
                               tq=_largest_tile(seq, 512 if latent else 256),
                               block_rows=ATTN_BLOCK_ROWS, **attn_shape)
            w_o = p['w_o_win']
        else:
            q, k, v = _qkv(x, mod, p['g_norm'], p['w_qkv_diff'], layer, slot, row_of_tile,
                           nq=DIFF_HEADS * 2 * DIFF_DIM, nk=DIFF_KV_HEADS * 2 * DIFF_DIM, **common)
            o = _diff_attention(q, k, v, cache, slot,
                                (p['lam_q1'], p['lam_k1'], p['lam_q2'], p['lam_k2']), p['g_subln_diff'],
                                batch=batch, seq=seq, tq=_largest_tile(seq, 256),
                                kv_heads=1,
                                lam_init=_diff_lambda_init(layer))
            w_o = p['w_o_diff']
        if not latent:
            new_state[kind] = (k, v)
        x, h = _oproj(o, w_o, x, mod, p['g_norm'], layer, slot, row_of_tile, tm=tm)
        x = _ffn(h, x, mod, p['g_norm'], p['w_gate'], p['w_up'], p['w_down'], layer,
                 conditioning_row(tm_ffn), tm=tm_ffn, fc=FFN_CHUNK)
    return x.reshape(batch, seq, d), new_state


def kernel(x_prompt, x_sample, c, cache_k_full, cache_v_full, cache_k_win, cache_v_win, cache_k_diff,
           cache_v_diff, c_ctx, w_ada, b_ada, g_norm, w_qkv_full, w_o_full, g_q_full, g_k_full, w_qkv_win,
           w_o_win, sink_win, w_qkv_diff, w_o_diff, lam_q1, lam_k1, lam_q2, lam_k2, g_subln_diff, w_gate,
           w_up, w_down):
    depth, d = w_ada.shape[0], w_ada.shape[1]
    batch, seq, _ = x_prompt.shape
    dec_batch, dec_seq, _ = x_sample.shape

    p = {
        'g_norm': g_norm,
        'w_qkv_full': w_qkv_full.astype(BF16), 'w_o_full': w_o_full.astype(BF16),
        'g_q_full': g_q_full, 'g_k_full': g_k_full,
        'w_qkv_win': w_qkv_win.astype(BF16), 'w_o_win': w_o_win.astype(BF16), 'sink_win': sink_win,
        'w_qkv_diff': w_qkv_diff.astype(BF16), 'w_o_diff': w_o_diff.astype(BF16),
        'lam_q1': lam_q1, 'lam_k1': lam_k1, 'lam_q2': lam_q2, 'lam_k2': lam_k2,
        'g_subln_diff': g_subln_diff,
        'w_gate': w_gate.astype(BF16), 'w_up': w_up.astype(BF16), 'w_down': w_down.astype(BF16),
    }

    n_rows = -(-(1 + dec_batch) // SUBLANES) * SUBLANES
    cond = jnp.zeros((n_rows, d), F32).at[0].set(c_ctx).at[1:1 + dec_batch].set(c)
    mod = _modulation(cond, w_ada, b_ada)

    y_prompt, new_state = _run_path(x_prompt, 0, 0, None, mod, p, depth)

    def head_rows_cache(a):
        return a.reshape(a.shape[0], a.shape[1], -1, LANES)

    caches = tuple(head_rows_cache(a) for a in (cache_k_full, cache_v_full, cache_k_win, cache_v_win,
                                                cache_k_diff, cache_v_diff))
    y_sample, _ = _run_path(x_sample, 1, 1, caches, mod, p, depth)

    def state(a, tail):
        return a.reshape(a.shape[:2] + (seq,) + tail)

    (k_full, v_full), (k_win, v_win), (k_diff, v_diff) = new_state
    return (y_prompt, y_sample,
            state(k_full, (KV_HEADS, HEAD_DIM)), state(v_full, (KV_HEADS, HEAD_DIM)),
            state(k_win, (KV_HEADS, HEAD_DIM)), state(v_win, (KV_HEADS, HEAD_DIM)),
            state(k_diff, (DIFF_KV_HEADS, 2, DIFF_DIM)), state(v_diff, (DIFF_KV_HEADS, 2 * DIFF_DIM)))
```

```python
import functools
import math

import jax
import jax.numpy as jnp
from jax import lax
from jax.experimental import pallas as pl
from jax.experimental.pallas import tpu as pltpu

F32 = jnp.float32
BF16 = jnp.bfloat16

N_MIXERS = 3
N_HEADS = 16
KV_HEADS = 4
HEAD_DIM = 128
GROUP = N_HEADS // KV_HEADS
WINDOW = 128
GRID_W = 64
DIFF_HEADS = 8
DIFF_KV_HEADS = 2
DIFF_GROUP = DIFF_HEADS // DIFF_KV_HEADS
DIFF_DIM = 128
ROPE_THETA = 10000.0
EPS = 1e-6
NEG_INF = -1e30

LANES = 128
SUBLANES = 8
V7X_VMEM_BYTES = 64 * 1024 * 1024
VMEM_LIMIT = V7X_VMEM_BYTES - 8 * 1024 * 1024

NT_DIMS = (((1,), (1,)), ((), ()))
LOG2E = math.log2(math.e)
ROW_CHUNK = 2 * SUBLANES
ATTN_BLOCK_ROWS = 256
DENSE_BLOCK_ROWS = 128
FFN_CHUNK = 512
FFN_EPILOGUE_BLOCKS = 2


def _params(*semantics):
    return pltpu.CompilerParams(dimension_semantics=semantics, vmem_limit_bytes=VMEM_LIMIT)


def _rms(x):
    return x * lax.rsqrt(jnp.mean(x * x, axis=-1, keepdims=True) + EPS)


def _mod_kernel(c_ref, w_ref, b_ref, o_ref):
    c = c_ref[...]
    a = (c * jax.nn.sigmoid(c)).astype(BF16)
    o_ref[...] = jnp.dot(a, w_ref[...].astype(BF16), preferred_element_type=F32) + b_ref[...]


def _modulation(cond, w_ada, b_ada):
    n_layers, d, n = w_ada.shape
    rows = cond.shape[0]
    tn = 1024
    return pl.pallas_call(
        _mod_kernel,
        out_shape=jax.ShapeDtypeStruct((n_layers, rows, n), F32),
        grid=(n_layers, n // tn),
        in_specs=[
            pl.BlockSpec((rows, d), lambda l, j: (0, 0)),
            pl.BlockSpec((None, d, tn), lambda l, j: (l, 0, j)),
            pl.BlockSpec((None, 1, tn), lambda l, j: (l, 0, j)),
        ],
        out_specs=pl.BlockSpec((None, rows, tn), lambda l, j: (l, 0, j)),
        compiler_params=_params("parallel", "parallel"),
        name="adaln_mod",
    )(cond, w_ada, b_ada.reshape(n_layers, 1, n))


def _qkv_kernel(*refs, d, nq, nk, qk_norm, rope, q_scale, row_of_tile, head_rows_seq, n_aliased, owned_slot):
    it = iter(refs)
    x_ref, mod_ref, g_ref, w_ref = next(it), next(it), next(it), next(it)
    gq_ref = gk_ref = cos_ref = sin_ref = None
    if qk_norm:
        gq_ref, gk_ref = next(it), next(it)
    if rope:
        cos_ref, sin_ref = next(it), next(it)
    for _ in range(n_aliased):
        next(it)
    q_ref, k_ref, v_ref = next(it), next(it), next(it)

    def store_chunk(ref, chunk, n_chunks, z):
        if head_rows_seq is None:
            ref[:, chunk * LANES:(chunk + 1) * LANES] = z.astype(ref.dtype)
        else:
            seq = head_rows_seq
            for b in range(z.shape[0] // seq):
                dst = ref.at[b] if owned_slot is None else ref.at[b, owned_slot]
                dst[pl.ds(chunk, seq, stride=n_chunks), :] = z[b * seq:(b + 1) * seq, :].astype(ref.dtype)

    if owned_slot is not None:
        for ref in (k_ref, v_ref):
            for s in range(ref.shape[1]):
                if s != owned_slot:
                    ref[:, s] = jnp.zeros((ref.shape[0],) + ref.shape[2:], ref.dtype)

    mod = mod_ref[pl.ds(row_of_tile(pl.program_id(0)), 1), :]
    h = (_rms(x_ref[...]) * (g_ref[0:1, :] * (1.0 + mod[:, d:2 * d])) + mod[:, 0:d]).astype(BF16)

    def partner(a):
        lane = lax.broadcasted_iota(jnp.int32, a.shape, 1)
        return jnp.where((lane & 32) != 0, pltpu.roll(a, 32, 1), pltpu.roll(a, LANES - 32, 1))

    def lane_factors(gain_ref, scale):
        gain = None if gain_ref is None else jnp.broadcast_to(gain_ref[...] * scale, (SUBLANES, LANES))
        if not rope:
            return (scale if gain is None else gain[0:1, :]), None
        if gain is None:
            return cos_ref[...] * scale, sin_ref[...] * scale
        return cos_ref[...] * gain[0:1, :], sin_ref[...] * partner(gain)[0:1, :]

    def finish(z, normed, factors):
        c, s = factors
        if s is not None:
            t = z * c + partner(z) * s
        else:
            t = z if isinstance(c, float) and c == 1.0 else z * c
        if normed:
            t = t * lax.rsqrt(jnp.mean(z * z, axis=-1, keepdims=True) + EPS)
        return t

    q_factors = lane_factors(gq_ref, q_scale)
    k_factors = lane_factors(gk_ref, 1.0)
    chunk = 512
    for c0 in range(0, nq + nk, chunk):
        acc = jnp.dot(h, w_ref[:, c0:c0 + chunk], preferred_element_type=F32)
        for j in range(chunk // LANES):
            col = c0 + j * LANES
            z = acc[:, j * LANES:(j + 1) * LANES]
            if col < nq:
                q_ref[:, col:col + LANES] = finish(z, qk_norm, q_factors).astype(q_ref.dtype)
            else:
                store_chunk(k_ref, (col - nq) // LANES, nk // LANES, finish(z, qk_norm, k_factors))
    v = jnp.dot(h, w_ref[:, nq + nk:], preferred_element_type=F32)
    n_v_chunks = v.shape[1] // LANES
    for c in range(n_v_chunks):
        store_chunk(v_ref, c, n_v_chunks, v[:, c * LANES:(c + 1) * LANES])


def _qkv(x, mod, g_norm, w_qkv, layer, slot, row_of_tile, *, tm, nq, nk, gains=None, rope_tables=None,
         state_seq=None, state_slots=1, state_prev=None):
    t, d = x.shape
    n_all = w_qkv.shape[-1]
    nv = n_all - nq - nk
    operands = [x, mod, g_norm, w_qkv]
    in_specs = [
        pl.BlockSpec((tm, d), lambda i: (i, 0)),
        pl.BlockSpec((None,) + mod.shape[1:], lambda i: (layer, 0, 0)),
        pl.BlockSpec((None, 4, d), lambda i: (layer, 0, 0)),
        pl.BlockSpec((None, d, n_all), lambda i: (slot, 0, 0), pipeline_mode=pl.Buffered(1)),
    ]
    if gains is not None:
        for g in gains:
            operands.append(g.reshape(g.shape[0], 1, LANES))
            in_specs.append(pl.BlockSpec((None, 1, LANES), lambda i: (slot, 0, 0)))
    if rope_tables is not None:
        tiles_per_seq = rope_tables[0].shape[0] // tm
        for tab in rope_tables:
            operands.append(tab)
            in_specs.append(pl.BlockSpec((tm, LANES), lambda i: (i % tiles_per_seq, 0)))
    aliases = {}
    owned_slot = None
    if state_seq is None:
        kv_shapes = [jax.ShapeDtypeStruct((t, n), BF16) for n in (nk, nv)]
        kv_specs = [pl.BlockSpec((tm, n), lambda i: (i, 0)) for n in (nk, nv)]
    else:
        batches = tm // state_seq
        kv_shapes = [jax.ShapeDtypeStruct((t // state_seq, state_slots, state_seq * (n // LANES), LANES), F32)
                     for n in (nk, nv)]
        if state_prev is None:
            owned_slot = slot
            kv_specs = [pl.BlockSpec((batches,) + s.shape[1:], lambda i: (i, 0, 0, 0)) for s in kv_shapes]
        else:
            kv_specs = [pl.BlockSpec((batches, None) + s.shape[2:], lambda i: (i, slot, 0, 0)) for s in kv_shapes]
        if state_prev is not None:
            for j, prev in enumerate(state_prev):
                aliases[len(operands)] = 1 + j
                operands.append(prev)
                in_specs.append(pl.BlockSpec(memory_space=pl.ANY))
    kern = functools.partial(_qkv_kernel, d=d, nq=nq, nk=nk, qk_norm=gains is not None,
                             rope=rope_tables is not None, q_scale=HEAD_DIM ** -0.5 * LOG2E,
                             row_of_tile=row_of_tile, head_rows_seq=state_seq, n_aliased=len(aliases),
                             owned_slot=owned_slot)
    return pl.pallas_call(
        kern,
        out_shape=[jax.ShapeDtypeStruct((t, nq), BF16)] + kv_shapes,
        grid=(t // tm,),
        in_specs=in_specs,
        out_specs=[pl.BlockSpec((tm, nq), lambda i: (i, 0))] + kv_specs,
        input_output_aliases=aliases,
        compiler_params=_params("parallel"),
        name="qkv_proj",
    )(*operands)


def _gqa_attn_kernel(*refs, tq, kv_heads, block_rows, n_new, n_ctx, band, has_sink, head_rows):
    it = iter(refs)
    sink_ref = next(it) if has_sink else None
    q_ref, k_ref, v_ref = next(it), next(it), next(it)
    kc_ref = vc_ref = None
    if n_ctx:
        kc_ref, vc_ref = next(it), next(it)
    o_ref = next(it)
    s_scr, p_scr = next(it), next(it)
    bias_scr = next(it) if band else None
    if n_ctx:
        kc_scr, vc_scr = next(it), next(it)
        for head in range(KV_HEADS):
            @pl.when(jnp.logical_and(pl.program_id(1) == head, pl.program_id(2) == 0))
            def _():
                kc_scr[...] = kc_ref[pl.ds(head, n_ctx, stride=KV_HEADS), :].astype(BF16)
                vc_scr[...] = vc_ref[pl.ds(head, n_ctx, stride=KV_HEADS), :].astype(BF16)

    band_width = tq + 2 * WINDOW
    q0 = pl.program_id(2) * tq
    if head_rows:
        nk = n_new

        def head_kv(kvh):
            rows = pl.ds(kvh, n_new, stride=KV_HEADS)
            return k_ref[rows, :].astype(BF16), v_ref[rows, :].astype(BF16)
    else:
        if band and n_new > band_width:
            start = pl.multiple_of(jnp.clip(q0 - WINDOW, 0, n_new - band_width), LANES)
            k = k_ref[pl.ds(start, band_width), :]
            v = v_ref[pl.ds(start, band_width), :]
        else:
            start = 0
            k = k_ref[...]
            v = v_ref[...]
        nk = k.shape[0]
        if n_ctx:
            k = jnp.concatenate([k, kc_scr[...]], axis=0)
            v = jnp.concatenate([v, vc_scr[...]], axis=0)

        def head_kv(kvh):
            cols = slice(kvh * HEAD_DIM, (kvh + 1) * HEAD_DIM)
            return k[:, cols], v[:, cols]
    if band:
        qpos = q0 + lax.broadcasted_iota(jnp.int32, (tq, nk), 0)
        kpos = start + lax.broadcasted_iota(jnp.int32, (tq, nk), 1)
        bias_scr[...] = jnp.where(jnp.abs(qpos - kpos) <= WINDOW, 0.0, NEG_INF).astype(F32)
    ones = jnp.ones((nk + n_ctx, HEAD_DIM), BF16)

    if block_rows <= tq:
        blocks = [((h,), r0, block_rows) for h in range(GROUP) for r0 in range(0, tq, block_rows)]
    else:
        stacked = block_rows // tq
        blocks = [(tuple(range(h0, h0 + stacked)), 0, tq) for h0 in range(0, GROUP, stacked)]

    base = 0
    for kvh in range(kv_heads):
        k_h, v_h = head_kv(kvh)
        v_ext = jnp.concatenate([v_h, ones], axis=1)
        for heads, row0, n in blocks:
            rows = len(heads) * n
            cols = [(kvh * GROUP + h) * HEAD_DIM for h in heads]
            qb = jnp.concatenate([q_ref[row0:row0 + n, c:c + HEAD_DIM] for c in cols], axis=0)
            s_scr[base:base + rows, :] = lax.dot_general(qb, k_h, NT_DIMS, preferred_element_type=F32)
            sink_terms = []
            for r in range(0, rows, ROW_CHUNK):
                rs = slice(base + r, base + r + ROW_CHUNK)
                if band:
                    qr = row0 + r % n
                    parts = [s_scr[rs, 0:nk] + bias_scr[qr:qr + ROW_CHUNK, :]]
                    if n_ctx:
                        parts.append(s_scr[rs, nk:])
                else:
                    parts = [s_scr[rs, :]]
                m = functools.reduce(jnp.maximum, [jnp.max(a, axis=-1, keepdims=True) for a in parts])
                if has_sink:
                    head = (pl.program_id(1) * kv_heads + kvh) * GROUP + heads[r // n]
                    sk = sink_ref[head] * LOG2E
                    m = jnp.maximum(m, sk)
                    sink_terms.append(jnp.exp2(sk - jnp.broadcast_to(m, (ROW_CHUNK, HEAD_DIM))))
                col = 0
                for a in parts:
                    p_scr[rs, col:col + a.shape[1]] = jnp.exp2(a - m).astype(BF16)
                    col += a.shape[1]
            acc = jnp.dot(p_scr[base:base + rows, :], v_ext, preferred_element_type=F32)
            denom = acc[:, HEAD_DIM:]
            if has_sink:
                denom = denom + jnp.concatenate(sink_terms, axis=0)
            o = acc[:, :HEAD_DIM] / denom
            for j, c in enumerate(cols):
                o_ref[row0:row0 + n, c:c + HEAD_DIM] = o[j * n:(j + 1) * n, :].astype(o_ref.dtype)
            base += rows


def _gqa_attention(q, k, v, cache, slot, sink, *, batch, seq, tq, kv_heads, block_rows, band):
    nq_tiles = seq // tq
    head_rows = cache is None
    n_ctx = 0 if cache is None else cache[0].shape[2] // KV_HEADS
    assert kv_heads == (KV_HEADS if head_rows else 1)
    qw, kw = kv_heads * GROUP * HEAD_DIM, kv_heads * HEAD_DIM
    operands, in_specs = [], []
    if sink is not None:
        operands.append(sink)
        in_specs.append(pl.BlockSpec(memory_space=pltpu.SMEM))
    operands += [q, k, v]
    in_specs.append(pl.BlockSpec((tq, qw), lambda b, h, i: (b * nq_tiles + i, h)))
    if head_rows:
        in_specs += [pl.BlockSpec((None, None, seq * KV_HEADS, HEAD_DIM), lambda b, h, i: (b, slot, 0, 0))] * 2
    else:
        in_specs += [pl.BlockSpec((seq, kw), lambda b, h, i: (b, h))] * 2
        operands += list(cache)
        in_specs += [pl.BlockSpec((None, None, n_ctx * KV_HEADS, HEAD_DIM), lambda b, h, i: (b, slot, 0, 0))] * 2
    kern = functools.partial(_gqa_attn_kernel, tq=tq, kv_heads=kv_heads, block_rows=block_rows, n_new=seq,
                             n_ctx=n_ctx, band=band, has_sink=sink is not None, head_rows=head_rows)
    n_new_keys = min(seq, tq + 2 * WINDOW) if band else seq
    n_keys = n_new_keys + n_ctx
    scratch = [pltpu.VMEM((kv_heads * GROUP * tq, n_keys), F32), pltpu.VMEM((kv_heads * GROUP * tq, n_keys), BF16)]
    if band:
        scratch.append(pltpu.VMEM((tq, n_new_keys), F32))
    if n_ctx:
        scratch += [pltpu.VMEM((n_ctx, HEAD_DIM), BF16)] * 2
    return pl.pallas_call(
        kern,
        out_shape=jax.ShapeDtypeStruct(q.shape, BF16),
        grid=(batch, KV_HEADS // kv_heads, nq_tiles),
        in_specs=in_specs,
        out_specs=pl.BlockSpec((tq, qw), lambda b, h, i: (b * nq_tiles + i, h)),
        scratch_shapes=scratch,
        compiler_params=_params("parallel", "parallel", "arbitrary"),
        name="gqa_attn",
    )(*operands)


def _diff_attn_kernel(*refs, tq, kv_heads, block_heads, n_new, n_ctx, lam_init, head_rows):
    it = iter(refs)
    lq1_ref, lk1_ref, lq2_ref, lk2_ref, gs_ref = next(it), next(it), next(it), next(it), next(it)
    q_ref, k_ref, v_ref = next(it), next(it), next(it)
    kc_ref = vc_ref = None
    if n_ctx:
        kc_ref, vc_ref = next(it), next(it)
    o_ref = next(it)
    s_scrs = (next(it), next(it))
    p_scr = next(it)

    def pick_head(k_src, v_src, n_keys):
        k_dst, v_dst = next(it), next(it)
        for head in range(DIFF_KV_HEADS):
            @pl.when(jnp.logical_and(pl.program_id(1) == head, pl.program_id(2) == 0))
            def _():
                for half in range(2):
                    rows = pl.ds(2 * head + half, n_keys, stride=2 * DIFF_KV_HEADS)
                    cols = slice(half * DIFF_DIM, (half + 1) * DIFF_DIM)
                    k_dst[:, cols] = k_src[rows, :].astype(BF16)
                    v_dst[:, cols] = v_src[rows, :].astype(BF16)
        return k_dst[...], v_dst[...]

    lam = (jnp.exp(jnp.sum(lq1_ref[...] * lk1_ref[...], axis=-1, keepdims=True))
           - jnp.exp(jnp.sum(lq2_ref[...] * lk2_ref[...], axis=-1, keepdims=True)) + lam_init)
    if head_rows:
        k, v = pick_head(k_ref, v_ref, n_new)
    else:
        k, v = k_ref[...], v_ref[...]
    if n_ctx:
        kc, vc = pick_head(kc_ref, vc_ref, n_ctx)
        k = jnp.concatenate([k, kc], axis=0)
        v = jnp.concatenate([v, vc], axis=0)

    width = 2 * DIFF_DIM
    rows = block_heads * tq
    base = 0
    for kvh in range(kv_heads):
        k_h = k[:, kvh * width:(kvh + 1) * width]
        v_h = v[:, kvh * width:(kvh + 1) * width]
        for h0 in range(0, DIFF_GROUP, block_heads):
            heads = [kvh * DIFF_GROUP + g for g in range(h0, h0 + block_heads)]
            for half in range(2):
                qb = jnp.concatenate(
                    [q_ref[:, (2 * g + half) * DIFF_DIM:(2 * g + half + 1) * DIFF_DIM] for g in heads], axis=0)
                s_scrs[half][base:base + rows, :] = lax.dot_general(
                    qb, k_h[:, half * DIFF_DIM:(half + 1) * DIFF_DIM], NT_DIMS, preferred_element_type=F32)
            for r in range(0, rows, ROW_CHUNK):
                rs = slice(base + r, base + r + ROW_CHUNK)
                s0 = s_scrs[0][rs, :]
                e0 = jnp.exp2(s0 - jnp.max(s0, axis=-1, keepdims=True))
                r0 = 1.0 / jnp.sum(e0, axis=-1, keepdims=True)
                s1 = s_scrs[1][rs, :]
                e1 = jnp.exp2(s1 - jnp.max(s1, axis=-1, keepdims=True))
                r1 = lam / jnp.sum(e1, axis=-1, keepdims=True)
                p_scr[rs, :] = (e0 * r0 - e1 * r1).astype(BF16)
            acc = jnp.dot(p_scr[base:base + rows, :], v_h, preferred_element_type=F32)
            y = (_rms(acc) * gs_ref[...]) * (1.0 - lam_init)
            for j, g in enumerate(heads):
                o_ref[:, g * width:(g + 1) * width] = y[j * tq:(j + 1) * tq, :].astype(o_ref.dtype)
            base += rows


def _diff_attention(q, k, v, cache, slot, lam_vecs, g_subln, *, batch, seq, tq, kv_heads, lam_init):
    nq_tiles = seq // tq
    head_rows = cache is None
    rows_per_key = 2 * DIFF_KV_HEADS
    n_ctx = 0 if cache is None else cache[0].shape[2] // rows_per_key
    assert kv_heads == 1
    qw = DIFF_GROUP * 2 * DIFF_DIM
    kw = 2 * DIFF_DIM
    operands = [a.reshape(a.shape[0], 1, DIFF_DIM) for a in lam_vecs]
    in_specs = [pl.BlockSpec((None, 1, DIFF_DIM), lambda b, h, i: (slot, 0, 0))] * 4
    operands.append(g_subln.reshape(g_subln.shape[0], 1, 2 * DIFF_DIM))
    in_specs.append(pl.BlockSpec((None, 1, 2 * DIFF_DIM), lambda b, h, i: (slot, 0, 0)))
    operands += [q, k, v]
    in_specs.append(pl.BlockSpec((tq, qw), lambda b, h, i: (b * nq_tiles + i, h)))
    picked = []
    if head_rows:
        in_specs += [pl.BlockSpec((None, None, seq * rows_per_key, LANES), lambda b, h, i: (b, slot, 0, 0))] * 2
        picked.append(seq)
    else:
        in_specs += [pl.BlockSpec((seq, kw), lambda b, h, i: (b, h))] * 2
        operands += list(cache)
        in_specs += [pl.BlockSpec((None, None, n_ctx * rows_per_key, LANES), lambda b, h, i: (b, slot, 0, 0))] * 2
        picked.append(n_ctx)
    kern = functools.partial(_diff_attn_kernel, tq=tq, kv_heads=kv_heads, block_heads=2, n_new=seq, n_ctx=n_ctx,
                             lam_init=lam_init, head_rows=head_rows)
    score_shape = (kv_heads * DIFF_GROUP * tq, seq + n_ctx)
    ctx_scratch = [pltpu.VMEM((n, 2 * DIFF_DIM), BF16) for n in picked for _ in range(2)]
    return pl.pallas_call(
        kern,
        out_shape=jax.ShapeDtypeStruct(q.shape, BF16),
        grid=(batch, DIFF_KV_HEADS // kv_heads, nq_tiles),
        in_specs=in_specs,
        out_specs=pl.BlockSpec((tq, qw), lambda b, h, i: (b * nq_tiles + i, h)),
        scratch_shapes=[pltpu.VMEM(score_shape, F32), pltpu.VMEM(score_shape, F32),
                        pltpu.VMEM(score_shape, BF16)] + ctx_scratch,
        compiler_params=_params("parallel", "parallel", "arbitrary"),
        name="diff_attn",
    )(*operands)


def _oproj_kernel(o_ref, w_ref, x_ref, mod_ref, g_ref, out_ref, h_ref, *, d, row_blocks, row_of_tile):
    mod = mod_ref[pl.ds(row_of_tile(pl.program_id(0)), 1), :]
    gain1 = mod[:, 2 * d:3 * d] * g_ref[1:2, :]
    gain2 = g_ref[2:3, :] * (1.0 + mod[:, 4 * d:5 * d])
    shift2 = mod[:, 3 * d:4 * d]
    rows = o_ref.shape[0] // row_blocks
    for r in range(row_blocks):
        rs = slice(r * rows, (r + 1) * rows)
        m = jnp.dot(o_ref[rs, :], w_ref[...], preferred_element_type=F32)
        x_new = x_ref[rs, :] + _rms(m) * gain1
        out_ref[rs, :] = x_new
        h_ref[rs, :] = (_rms(x_new) * gain2 + shift2).astype(BF16)


def _oproj(o, w_o, x, mod, g_norm, layer, slot, row_of_tile, *, tm):
    t, d = x.shape
    nin = o.shape[1]
    return pl.pallas_call(
        functools.partial(_oproj_kernel, d=d, row_blocks=4 if tm % (4 * ROW_CHUNK) == 0 else 1,
                          row_of_tile=row_of_tile),
        out_shape=(jax.ShapeDtypeStruct((t, d), F32), jax.ShapeDtypeStruct((t, d), BF16)),
        grid=(t // tm,),
        in_specs=[
            pl.BlockSpec((tm, nin), lambda i: (i, 0)),
            pl.BlockSpec((None, nin, d), lambda i: (slot, 0, 0), pipeline_mode=pl.Buffered(1)),
            pl.BlockSpec((tm, d), lambda i: (i, 0)),
            pl.BlockSpec((None,) + mod.shape[1:], lambda i: (layer, 0, 0)),
            pl.BlockSpec((None, 4, d), lambda i: (layer, 0, 0)),
        ],
        out_specs=(pl.BlockSpec((tm, d), lambda i: (i, 0)), pl.BlockSpec((tm, d), lambda i: (i, 0))),
        compiler_params=_params("parallel"),
        name="out_proj",
    )(o, w_o, x, mod, g_norm)


def _ffn_kernel(h_ref, x_ref, mod_ref, g_ref, wg_ref, wu_ref, wd_ref, out_ref, *, d, row_of_tile):
    j = pl.program_id(1)
    last = pl.num_programs(1) - 1

    def partial_sum(rows):
        h = h_ref[rows, :]
        a = jnp.dot(h, wg_ref[...], preferred_element_type=F32)
        b = jnp.dot(h, wu_ref[...], preferred_element_type=F32)
        u = ((a * jax.nn.sigmoid(a)) * b).astype(BF16)
        return jnp.dot(u, wd_ref[...], preferred_element_type=F32)

    @pl.when(j == 0)
    def _():
        out_ref[...] = partial_sum(slice(None))

    @pl.when(jnp.logical_and(j > 0, j < last))
    def _():
        out_ref[...] += partial_sum(slice(None))

    @pl.when(j == last)
    def _():
        gate = mod_ref[pl.ds(row_of_tile(pl.program_id(0)), 1), 5 * d:6 * d]
        gain = gate * g_ref[3:4, :]
        rows = out_ref.shape[0] // FFN_EPILOGUE_BLOCKS
        for r in range(FFN_EPILOGUE_BLOCKS):
            rs = slice(r * rows, (r + 1) * rows)
            acc = out_ref[rs, :] + partial_sum(rs)
            out_ref[rs, :] = x_ref[rs, :] + _rms(acc) * gain


def _ffn(h, x, mod, g_norm, w_gate, w_up, w_down, layer, row_of_tile, *, tm, fc):
    t, d = x.shape
    d_ff = w_gate.shape[-1]
    return pl.pallas_call(
        functools.partial(_ffn_kernel, d=d, row_of_tile=row_of_tile),
        out_shape=jax.ShapeDtypeStruct((t, d), F32),
        grid=(t // tm, d_ff // fc),
        in_specs=[
            pl.BlockSpec((tm, d), lambda i, j: (i, 0)),
            pl.BlockSpec((tm, d), lambda i, j: (i, 0)),
            pl.BlockSpec((None,) + mod.shape[1:], lambda i, j: (layer, 0, 0)),
            pl.BlockSpec((None, 4, d), lambda i, j: (layer, 0, 0)),
            pl.BlockSpec((None, d, fc), lambda i, j: (layer, 0, j)),
            pl.BlockSpec((None, d, fc), lambda i, j: (layer, 0, j)),
            pl.BlockSpec((None, fc, d), lambda i, j: (layer, j, 0)),
        ],
        out_specs=pl.BlockSpec((tm, d), lambda i, j: (i, 0)),
        compiler_params=_params("parallel", "arbitrary"),
        name="ffn",
    )(h, x, mod, g_norm, w_gate, w_up, w_down)


def _rope_tables(n_tokens):
    nf = HEAD_DIM // 4
    t = jnp.arange(n_tokens, dtype=jnp.int32)
    rows = (t // GRID_W).astype(F32)
    cols = (t % GRID_W).astype(F32)
    inv = 1.0 / (ROPE_THETA ** (jnp.arange(nf, dtype=F32) / nf))
    ar = rows[:, None] * inv
    ac = cols[:, None] * inv
    cos = jnp.concatenate([jnp.cos(ar), jnp.cos(ar), jnp.cos(ac), jnp.cos(ac)], axis=-1)
    sin = jnp.concatenate([-jnp.sin(ar), jnp.sin(ar), -jnp.sin(ac), jnp.sin(ac)], axis=-1)
    return cos, sin


def _diff_lambda_init(layer):
    return 0.8 - 0.6 * math.exp(-0.3 * layer)


def _largest_tile(n, cap):
    t = cap
    while n % t:
        t //= 2
    return t


def _run_path(x3, first_row, rows_per_batch, caches, mod, p, depth):
    batch, seq, d = x3.shape
    latent = caches is not None
    x = x3.reshape(batch * seq, d)
    tm = _largest_tile(seq if latent else batch * seq, 512)

    def conditioning_row(tile_rows):
        if not latent:
            return lambda i: first_row
        tiles_per_batch = seq // tile_rows
        return lambda i: first_row + (i // tiles_per_batch) * rows_per_batch

    row_of_tile = conditioning_row(tm)

    rope_tables = _rope_tables(seq) if latent else None
    new_state = [None] * N_MIXERS
    for layer in range(depth):
        kind, slot = layer % N_MIXERS, layer // N_MIXERS
        cache = None if not latent else (caches[2 * kind], caches[2 * kind + 1])
        common = dict(tm=tm, rope_tables=rope_tables)
        if not latent:
            n_slots = (depth - kind + N_MIXERS - 1) // N_MIXERS
            common.update(state_seq=seq, state_slots=n_slots, state_prev=new_state[kind])
        attn_shape = dict(kv_heads=1 if latent else KV_HEADS)
        if kind == 0:
            q, k, v = _qkv(x, mod, p['g_norm'], p['w_qkv_full'], layer, slot, row_of_tile,
                           nq=N_HEADS * HEAD_DIM, nk=KV_HEADS * HEAD_DIM,
                           gains=(p['g_q_full'], p['g_k_full']), **common)
            o = _gqa_attention(q, k, v, cache, slot, None, batch=batch, seq=seq, band=False,
                               tq=_largest_tile(seq, 512),
                               block_rows=DENSE_BLOCK_ROWS if latent else ATTN_BLOCK_ROWS, **attn_shape)
            w_o = p['w_o_full']
        elif kind == 1:
            q, k, v = _qkv(x, mod, p['g_norm'], p['w_qkv_win'], layer, slot, row_of_tile,
                           nq=N_HEADS * HEAD_DIM, nk=KV_HEADS * HEAD_DIM, **common)
            o = _gqa_attention(q, k, v, cache, slot, p['sink_win'][slot], batch=batch, seq=seq, band=latent,
                               tq=_largest_tile(seq, 512 if latent else 256),
                               block_rows=ATTN_BLOCK_ROWS, **attn_shape)
            w_o = p['w_o_win']
        else:
            q, k, v = _qkv(x, mod, p['g_norm'], p['w_qkv_diff'], layer, slot, row_of_tile,
                           nq=DIFF_HEADS * 2 * DIFF_DIM, nk=DIFF_KV_HEADS * 2 * DIFF_DIM, **common)
            o = _diff_attention(q, k, v, cache, slot,
                                (p['lam_q1'], p['lam_k1'], p['lam_q2'], p['lam_k2']), p['g_subln_diff'],
                                batch=batch, seq=seq, tq=_largest_tile(seq, 256),
                                kv_heads=1,
                                lam_init=_diff_lambda_init(layer))
            w_o = p['w_o_diff']
        if not latent:
            new_state[kind] = (k, v)
        x, h = _oproj(o, w_o, x, mod, p['g_norm'], layer, slot, row_of_tile, tm=tm)
        x = _ffn(h, x, mod, p['g_norm'], p['w_gate'], p['w_up'], p['w_down'], layer, row_of_tile,
                 tm=tm, fc=FFN_CHUNK)
    return x.reshape(batch, seq, d), new_state


def kernel(x_prompt, x_sample, c, cache_k_full, cache_v_full, cache_k_win, cache_v_win, cache_k_diff,
           cache_v_diff, c_ctx, w_ada, b_ada, g_norm, w_qkv_full, w_o_full, g_q_full, g_k_full, w_qkv_win,
           w_o_win, sink_win, w_qkv_diff, w_o_diff, lam_q1, lam_k1, lam_q2, lam_k2, g_subln_diff, w_gate,
           w_up, w_down):
    depth, d = w_ada.shape[0], w_ada.shape[1]
    batch, seq, _ = x_prompt.shape
    dec_batch, dec_seq, _ = x_sample.shape

    p = {
        'g_norm': g_norm,
        'w_qkv_full': w_qkv_full.astype(BF16), 'w_o_full': w_o_full.astype(BF16),
        'g_q_full': g_q_full, 'g_k_full': g_k_full,
        'w_qkv_win': w_qkv_win.astype(BF16), 'w_o_win': w_o_win.astype(BF16), 'sink_win': sink_win,
        'w_qkv_diff': w_qkv_diff.astype(BF16), 'w_o_diff': w_o_diff.astype(BF16),
        'lam_q1': lam_q1, 'lam_k1': lam_k1, 'lam_q2': lam_q2, 'lam_k2': lam_k2,
        'g_subln_diff': g_subln_diff,
        'w_gate': w_gate.astype(BF16), 'w_up': w_up.astype(BF16), 'w_down': w_down.astype(BF16),
    }

    n_rows = -(-(1 + dec_batch) // SUBLANES) * SUBLANES
    cond = jnp.zeros((n_rows, d), F32).at[0].set(c_ctx).at[1:1 + dec_batch].set(c)
    mod = _modulation(cond, w_ada, b_ada)

    y_prompt, new_state = _run_path(x_prompt, 0, 0, None, mod, p, depth)

    def head_rows_cache(a):
        return a.reshape(a.shape[0], a.shape[1], -1, LANES)

    caches = tuple(head_rows_cache(a) for a in (cache_k_full, cache_v_full, cache_k_win, cache_v_win,
                                                cache_k_diff, cache_v_diff))
    y_sample, _ = _run_path(x_sample, 1, 1, caches, mod, p, depth)

    def state(a, tail):
        return a.reshape(a.shape[:2] + (seq,) + tail)

    (k_full, v_full), (k_win, v_win), (k_diff, v_diff) = new_state
    return (y_prompt, y_sample,
            state(k_full, (KV_HEADS, HEAD_DIM)), state(v_full, (KV_HEADS, HEAD_DIM)),
            state(k_win, (KV_HEADS, HEAD_DIM)), state(v_win, (KV_HEADS, HEAD_DIM)),
            state(k_diff, (DIFF_KV_HEADS, 2, DIFF_DIM)), state(v_diff, (DIFF_KV_HEADS, 2 * DIFF_DIM)))
```

```python
import functools
import math

import jax
import jax.numpy as jnp
from jax import lax
from jax.experimental import pallas as pl
from jax.experimental.pallas import tpu as pltpu

F32 = jnp.float32
BF16 = jnp.bfloat16

N_MIXERS = 3
N_HEADS = 16
KV_HEADS = 4
HEAD_DIM = 128
GROUP = N_HEADS // KV_HEADS
WINDOW = 128
GRID_W = 64
DIFF_HEADS = 8
DIFF_KV_HEADS = 2
DIFF_GROUP = DIFF_HEADS // DIFF_KV_HEADS
DIFF_DIM = 128
ROPE_THETA = 10000.0
EPS = 1e-6
NEG_INF = -1e30

LANES = 128
SUBLANES = 8
V7X_VMEM_BYTES = 64 * 1024 * 1024
VMEM_LIMIT = V7X_VMEM_BYTES - 8 * 1024 * 1024

NT_DIMS = (((1,), (1,)), ((), ()))
LOG2E = math.log2(math.e)
ROPE_PAIR_LANES = HEAD_DIM // 4

ROW_CHUNK = 2 * SUBLANES
TOKEN_TILE = 512
MOD_COLS = 1024
QKV_COLS = 512
OPROJ_ROW_BLOCKS = 4
ATTN_TQ = 256
LATENT_GQA_TQ = 512
ATTN_BLOCK_ROWS = 256
DENSE_BLOCK_ROWS = 128
FFN_CHUNK = 512
FFN_EPILOGUE_BLOCKS = 2


def _params(*semantics):
    return pltpu.CompilerParams(dimension_semantics=semantics, vmem_limit_bytes=VMEM_LIMIT)


def _rms(x):
    return x * lax.rsqrt(jnp.mean(x * x, axis=-1, keepdims=True) + EPS)


def _mod_kernel(c_ref, w_ref, b_ref, o_ref):
    c = c_ref[...]
    a = (c * jax.nn.sigmoid(c)).astype(BF16)
    o_ref[...] = jnp.dot(a, w_ref[...].astype(BF16), preferred_element_type=F32) + b_ref[...]


def _modulation(cond, w_ada, b_ada):
    n_layers, d, n = w_ada.shape
    rows = cond.shape[0]
    tn = MOD_COLS
    return pl.pallas_call(
        _mod_kernel,
        out_shape=jax.ShapeDtypeStruct((n_layers, rows, n), F32),
        grid=(n_layers, n // tn),
        in_specs=[
            pl.BlockSpec((rows, d), lambda l, j: (0, 0)),
            pl.BlockSpec((None, d, tn), lambda l, j: (l, 0, j)),
            pl.BlockSpec((None, 1, tn), lambda l, j: (l, 0, j)),
        ],
        out_specs=pl.BlockSpec((None, rows, tn), lambda l, j: (l, 0, j)),
        compiler_params=_params("parallel", "parallel"),
        name="adaln_mod",
    )(cond, w_ada, b_ada.reshape(n_layers, 1, n))


def _qkv_kernel(*refs, d, nq, nk, qk_norm, rope, q_scale, row_of_tile, head_rows_seq, n_aliased, owned_slot):
    it = iter(refs)
    x_ref, mod_ref, g_ref, w_ref = next(it), next(it), next(it), next(it)
    gq_ref = gk_ref = cos_ref = sin_ref = None
    if qk_norm:
        gq_ref, gk_ref = next(it), next(it)
    if rope:
        cos_ref, sin_ref = next(it), next(it)
    for _ in range(n_aliased):
        next(it)
    q_ref, k_ref, v_ref = next(it), next(it), next(it)

    def store_chunk(ref, chunk, n_chunks, z):
        if head_rows_seq is None:
            ref[:, chunk * LANES:(chunk + 1) * LANES] = z.astype(ref.dtype)
        else:
            seq = head_rows_seq
            for b in range(z.shape[0] // seq):
                dst = ref.at[b] if owned_slot is None else ref.at[b, owned_slot]
                dst[pl.ds(chunk, seq, stride=n_chunks), :] = z[b * seq:(b + 1) * seq, :].astype(ref.dtype)

    if owned_slot is not None:
        for ref in (k_ref, v_ref):
            for s in range(ref.shape[1]):
                if s != owned_slot:
                    ref[:, s] = jnp.zeros((ref.shape[0],) + ref.shape[2:], ref.dtype)

    mod = mod_ref[pl.ds(row_of_tile(pl.program_id(0)), 1), :]
    h = (_rms(x_ref[...]) * (g_ref[0:1, :] * (1.0 + mod[:, d:2 * d])) + mod[:, 0:d]).astype(BF16)

    def partner(a):
        lane = lax.broadcasted_iota(jnp.int32, a.shape, 1)
        upper = (lane & ROPE_PAIR_LANES) != 0
        return jnp.where(upper, pltpu.roll(a, ROPE_PAIR_LANES, 1), pltpu.roll(a, LANES - ROPE_PAIR_LANES, 1))

    def lane_factors(gain_ref, scale):
        gain = None if gain_ref is None else jnp.broadcast_to(gain_ref[...] * scale, (SUBLANES, LANES))
        if not rope:
            return (scale if gain is None else gain[0:1, :]), None
        if gain is None:
            return cos_ref[...] * scale, sin_ref[...] * scale
        return cos_ref[...] * gain[0:1, :], sin_ref[...] * partner(gain)[0:1, :]

    def finish(z, normed, factors):
        c, s = factors
        if s is not None:
            t = z * c + partner(z) * s
        else:
            t = z if isinstance(c, float) and c == 1.0 else z * c
        if normed:
            t = t * lax.rsqrt(jnp.mean(z * z, axis=-1, keepdims=True) + EPS)
        return t

    q_factors = lane_factors(gq_ref, q_scale)
    k_factors = lane_factors(gk_ref, 1.0)
    chunk = QKV_COLS
    for c0 in range(0, nq + nk, chunk):
        acc = jnp.dot(h, w_ref[:, c0:c0 + chunk], preferred_element_type=F32)
        for j in range(chunk // LANES):
            col = c0 + j * LANES
            z = acc[:, j * LANES:(j + 1) * LANES]
            if col < nq:
                q_ref[:, col:col + LANES] = finish(z, qk_norm, q_factors).astype(q_ref.dtype)
            else:
                store_chunk(k_ref, (col - nq) // LANES, nk // LANES, finish(z, qk_norm, k_factors))
    v = jnp.dot(h, w_ref[:, nq + nk:], preferred_element_type=F32)
    n_v_chunks = v.shape[1] // LANES
    for c in range(n_v_chunks):
        store_chunk(v_ref, c, n_v_chunks, v[:, c * LANES:(c + 1) * LANES])


def _qkv(x, mod, g_norm, w_qkv, layer, slot, row_of_tile, *, tm, nq, nk, gains=None, rope_tables=None,
         state_seq=None, state_slots=1, state_prev=None):
    t, d = x.shape
    assert t % tm == 0 and (state_seq is None or tm % state_seq == 0)
    n_all = w_qkv.shape[-1]
    nv = n_all - nq - nk
    operands = [x, mod, g_norm, w_qkv]
    in_specs = [
        pl.BlockSpec((tm, d), lambda i: (i, 0)),
        pl.BlockSpec((None,) + mod.shape[1:], lambda i: (layer, 0, 0)),
        pl.BlockSpec((None, 4, d), lambda i: (layer, 0, 0)),
        pl.BlockSpec((None, d, n_all), lambda i: (slot, 0, 0), pipeline_mode=pl.Buffered(1)),
    ]
    if gains is not None:
        for g in gains:
            operands.append(g.reshape(g.shape[0], 1, LANES))
            in_specs.append(pl.BlockSpec((None, 1, LANES), lambda i: (slot, 0, 0)))
    if rope_tables is not None:
        tiles_per_seq = rope_tables[0].shape[0] // tm
        for tab in rope_tables:
            operands.append(tab)
            in_specs.append(pl.BlockSpec((tm, LANES), lambda i: (i % tiles_per_seq, 0)))
    aliases = {}
    owned_slot = None
    if state_seq is None:
        kv_shapes = [jax.ShapeDtypeStruct((t, n), BF16) for n in (nk, nv)]
        kv_specs = [pl.BlockSpec((tm, n), lambda i: (i, 0)) for n in (nk, nv)]
    else:
        batches = tm // state_seq
        kv_shapes = [jax.ShapeDtypeStruct((t // state_seq, state_slots, state_seq * (n // LANES), LANES), F32)
                     for n in (nk, nv)]
        if state_prev is None:
            owned_slot = slot
            kv_specs = [pl.BlockSpec((batches,) + s.shape[1:], lambda i: (i, 0, 0, 0)) for s in kv_shapes]
        else:
            kv_specs = [pl.BlockSpec((batches, None) + s.shape[2:], lambda i: (i, slot, 0, 0)) for s in kv_shapes]
            for j, prev in enumerate(state_prev):
                aliases[len(operands)] = 1 + j
                operands.append(prev)
                in_specs.append(pl.BlockSpec(memory_space=pl.ANY))
    kern = functools.partial(_qkv_kernel, d=d, nq=nq, nk=nk, qk_norm=gains is not None,
                             rope=rope_tables is not None, q_scale=HEAD_DIM ** -0.5 * LOG2E,
                             row_of_tile=row_of_tile, head_rows_seq=state_seq, n_aliased=len(aliases),
                             owned_slot=owned_slot)
    return pl.pallas_call(
        kern,
        out_shape=[jax.ShapeDtypeStruct((t, nq), BF16)] + kv_shapes,
        grid=(t // tm,),
        in_specs=in_specs,
        out_specs=[pl.BlockSpec((tm, nq), lambda i: (i, 0))] + kv_specs,
        input_output_aliases=aliases,
        compiler_params=_params("parallel"),
        name="qkv_proj",
    )(*operands)


def _gqa_attn_kernel(*refs, tq, kv_heads, block_rows, n_new, n_ctx, band, has_sink, head_rows):
    it = iter(refs)
    sink_ref = next(it) if has_sink else None
    q_ref, k_ref, v_ref = next(it), next(it), next(it)
    kc_ref = vc_ref = None
    if n_ctx:
        kc_ref, vc_ref = next(it), next(it)
    o_ref = next(it)
    s_scr, p_scr = next(it), next(it)
    bias_scr = next(it) if band else None
    if n_ctx:
        kc_scr, vc_scr = next(it), next(it)
        for head in range(KV_HEADS):
            @pl.when(jnp.logical_and(pl.program_id(1) == head, pl.program_id(2) == 0))
            def _():
                kc_scr[...] = kc_ref[pl.ds(head, n_ctx, stride=KV_HEADS), :].astype(BF16)
                vc_scr[...] = vc_ref[pl.ds(head, n_ctx, stride=KV_HEADS), :].astype(BF16)

    band_width = tq + 2 * WINDOW
    q0 = pl.program_id(2) * tq
    if head_rows:
        nk = n_new

        def head_kv(kvh):
            rows = pl.ds(kvh, n_new, stride=KV_HEADS)
            return k_ref[rows, :].astype(BF16), v_ref[rows, :].astype(BF16)
    else:
        if band and n_new > band_width:
            start = pl.multiple_of(jnp.clip(q0 - WINDOW, 0, n_new - band_width), LANES)
            k = k_ref[pl.ds(start, band_width), :]
            v = v_ref[pl.ds(start, band_width), :]
        else:
            start = 0
            k = k_ref[...]
            v = v_ref[...]
        nk = k.shape[0]
        if n_ctx:
            k = jnp.concatenate([k, kc_scr[...]], axis=0)
            v = jnp.concatenate([v, vc_scr[...]], axis=0)

        def head_kv(kvh):
            cols = slice(kvh * HEAD_DIM, (kvh + 1) * HEAD_DIM)
            return k[:, cols], v[:, cols]
    if band:
        qpos = q0 + lax.broadcasted_iota(jnp.int32, (tq, nk), 0)
        kpos = start + lax.broadcasted_iota(jnp.int32, (tq, nk), 1)
        bias_scr[...] = jnp.where(jnp.abs(qpos - kpos) <= WINDOW, 0.0, NEG_INF).astype(F32)
    ones = jnp.ones((nk + n_ctx, HEAD_DIM), BF16)

    if block_rows <= tq:
        blocks = [((h,), r0, block_rows) for h in range(GROUP) for r0 in range(0, tq, block_rows)]
    else:
        stacked = block_rows // tq
        blocks = [(tuple(range(h0, h0 + stacked)), 0, tq) for h0 in range(0, GROUP, stacked)]

    base = 0
    for kvh in range(kv_heads):
        k_h, v_h = head_kv(kvh)
        v_ext = jnp.concatenate([v_h, ones], axis=1)
        for heads, row0, n in blocks:
            rows = len(heads) * n
            cols = [(kvh * GROUP + h) * HEAD_DIM for h in heads]
            qb = jnp.concatenate([q_ref[row0:row0 + n, c:c + HEAD_DIM] for c in cols], axis=0)
            s_scr[base:base + rows, :] = lax.dot_general(qb, k_h, NT_DIMS, preferred_element_type=F32)
            sink_terms = []
            for r in range(0, rows, ROW_CHUNK):
                rs = slice(base + r, base + r + ROW_CHUNK)
                if band:
                    qr = row0 + r % n
                    parts = [s_scr[rs, 0:nk] + bias_scr[qr:qr + ROW_CHUNK, :]]
                    if n_ctx:
                        parts.append(s_scr[rs, nk:])
                else:
                    parts = [s_scr[rs, :]]
                m = functools.reduce(jnp.maximum, [jnp.max(a, axis=-1, keepdims=True) for a in parts])
                if has_sink:
                    head = (pl.program_id(1) * kv_heads + kvh) * GROUP + heads[r // n]
                    sk = sink_ref[head] * LOG2E
                    m = jnp.maximum(m, sk)
                    sink_terms.append(jnp.exp2(sk - jnp.broadcast_to(m, (ROW_CHUNK, HEAD_DIM))))
                col = 0
                for a in parts:
                    p_scr[rs, col:col + a.shape[1]] = jnp.exp2(a - m).astype(BF16)
                    col += a.shape[1]
            acc = jnp.dot(p_scr[base:base + rows, :], v_ext, preferred_element_type=F32)
            denom = acc[:, HEAD_DIM:]
            if has_sink:
                denom = denom + jnp.concatenate(sink_terms, axis=0)
            o = acc[:, :HEAD_DIM] / denom
            for j, c in enumerate(cols):
                o_ref[row0:row0 + n, c:c + HEAD_DIM] = o[j * n:(j + 1) * n, :].astype(o_ref.dtype)
            base += rows


def _gqa_attention(q, k, v, cache, slot, sink, *, batch, seq, tq, kv_heads, block_rows, band):
    nq_tiles = seq // tq
    head_rows = cache is None
    n_ctx = 0 if cache is None else cache[0].shape[2] // KV_HEADS
    assert kv_heads == (KV_HEADS if head_rows else 1) and not (head_rows and band)
    qw, kw = kv_heads * GROUP * HEAD_DIM, kv_heads * HEAD_DIM
    operands, in_specs = [], []
    if sink is not None:
        operands.append(sink)
        in_specs.append(pl.BlockSpec(memory_space=pltpu.SMEM))
    operands += [q, k, v]
    in_specs.append(pl.BlockSpec((tq, qw), lambda b, h, i: (b * nq_tiles + i, h)))
    if head_rows:
        in_specs += [pl.BlockSpec((None, None, seq * KV_HEADS, HEAD_DIM), lambda b, h, i: (b, slot, 0, 0))] * 2
    else:
        in_specs += [pl.BlockSpec((seq, kw), lambda b, h, i: (b, h))] * 2
        operands += list(cache)
        in_specs += [pl.BlockSpec((None, None, n_ctx * KV_HEADS, HEAD_DIM), lambda b, h, i: (b, slot, 0, 0))] * 2
    kern = functools.partial(_gqa_attn_kernel, tq=tq, kv_heads=kv_heads, block_rows=block_rows, n_new=seq,
                             n_ctx=n_ctx, band=band, has_sink=sink is not None, head_rows=head_rows)
    n_new_keys = min(seq, tq + 2 * WINDOW) if band else seq
    n_keys = n_new_keys + n_ctx
    scratch = [pltpu.VMEM((kv_heads * GROUP * tq, n_keys), F32), pltpu.VMEM((kv_heads * GROUP * tq, n_keys), BF16)]
    if band:
        scratch.append(pltpu.VMEM((tq, n_new_keys), F32))
    if n_ctx:
        scratch += [pltpu.VMEM((n_ctx, HEAD_DIM), BF16)] * 2
    return pl.pallas_call(
        kern,
        out_shape=jax.ShapeDtypeStruct(q.shape, BF16),
        grid=(batch, KV_HEADS // kv_heads, nq_tiles),
        in_specs=in_specs,
        out_specs=pl.BlockSpec((tq, qw), lambda b, h, i: (b * nq_tiles + i, h)),
        scratch_shapes=scratch,
        compiler_params=_params("parallel", "parallel", "arbitrary"),
        name="gqa_attn",
    )(*operands)


def _diff_attn_kernel(*refs, tq, block_rows, n_new, n_ctx, lam_init, head_rows):
    it = iter(refs)
    lq1_ref, lk1_ref, lq2_ref, lk2_ref, gs_ref = next(it), next(it), next(it), next(it), next(it)
    q_ref, k_ref, v_ref = next(it), next(it), next(it)
    kc_ref = vc_ref = None
    if n_ctx:
        kc_ref, vc_ref = next(it), next(it)
    o_ref = next(it)
    s_scrs = (next(it), next(it))
    p_scr = next(it)

    def pick_head(k_src, v_src, n_keys):
        k_dst, v_dst = next(it), next(it)
        for head in range(DIFF_KV_HEADS):
            @pl.when(jnp.logical_and(pl.program_id(1) == head, pl.program_id(2) == 0))
            def _():
                for half in range(2):
                    rows = pl.ds(2 * head + half, n_keys, stride=2 * DIFF_KV_HEADS)
                    cols = slice(half * DIFF_DIM, (half + 1) * DIFF_DIM)
                    k_dst[:, cols] = k_src[rows, :].astype(BF16)
                    v_dst[:, cols] = v_src[rows, :].astype(BF16)
        return k_dst[...], v_dst[...]

    lam = (jnp.exp(jnp.sum(lq1_ref[...] * lk1_ref[...], axis=-1, keepdims=True))
           - jnp.exp(jnp.sum(lq2_ref[...] * lk2_ref[...], axis=-1, keepdims=True)) + lam_init)
    if head_rows:
        k, v = pick_head(k_ref, v_ref, n_new)
    else:
        k, v = k_ref[...], v_ref[...]
    if n_ctx:
        kc, vc = pick_head(kc_ref, vc_ref, n_ctx)
        k = jnp.concatenate([k, kc], axis=0)
        v = jnp.concatenate([v, vc], axis=0)

    width = 2 * DIFF_DIM
    if block_rows <= tq:
        blocks = [((g,), r0, block_rows) for g in range(DIFF_GROUP) for r0 in range(0, tq, block_rows)]
    else:
        stacked = block_rows // tq
        blocks = [(tuple(range(g0, g0 + stacked)), 0, tq) for g0 in range(0, DIFF_GROUP, stacked)]
    base = 0
    for heads, row0, n in blocks:
        rows = len(heads) * n
        for half in range(2):
            qb = jnp.concatenate(
                [q_ref[row0:row0 + n, (2 * g + half) * DIFF_DIM:(2 * g + half + 1) * DIFF_DIM] for g in heads],
                axis=0)
            s_scrs[half][base:base + rows, :] = lax.dot_general(
                qb, k[:, half * DIFF_DIM:(half + 1) * DIFF_DIM], NT_DIMS, preferred_element_type=F32)
        for r in range(0, rows, ROW_CHUNK):
            rs = slice(base + r, base + r + ROW_CHUNK)
            s0 = s_scrs[0][rs, :]
            e0 = jnp.exp2(s0 - jnp.max(s0, axis=-1, keepdims=True))
            r0 = 1.0 / jnp.sum(e0, axis=-1, keepdims=True)
            s1 = s_scrs[1][rs, :]
            e1 = jnp.exp2(s1 - jnp.max(s1, axis=-1, keepdims=True))
            r1 = lam / jnp.sum(e1, axis=-1, keepdims=True)
            p_scr[rs, :] = (e0 * r0 - e1 * r1).astype(BF16)
        acc = jnp.dot(p_scr[base:base + rows, :], v, preferred_element_type=F32)
        y = (_rms(acc) * gs_ref[...]) * (1.0 - lam_init)
        for j, g in enumerate(heads):
            o_ref[row0:row0 + n, g * width:(g + 1) * width] = y[j * n:(j + 1) * n, :].astype(o_ref.dtype)
        base += rows


def _diff_attention(q, k, v, cache, slot, lam_vecs, g_subln, *, batch, seq, tq, block_rows, lam_init):
    nq_tiles = seq // tq
    head_rows = cache is None
    rows_per_key = 2 * DIFF_KV_HEADS
    n_ctx = 0 if cache is None else cache[0].shape[2] // rows_per_key
    qw = DIFF_GROUP * 2 * DIFF_DIM
    kw = 2 * DIFF_DIM
    operands = [a.reshape(a.shape[0], 1, DIFF_DIM) for a in lam_vecs]
    in_specs = [pl.BlockSpec((None, 1, DIFF_DIM), lambda b, h, i: (slot, 0, 0))] * 4
    operands.append(g_subln.reshape(g_subln.shape[0], 1, 2 * DIFF_DIM))
    in_specs.append(pl.BlockSpec((None, 1, 2 * DIFF_DIM), lambda b, h, i: (slot, 0, 0)))
    operands += [q, k, v]
    in_specs.append(pl.BlockSpec((tq, qw), lambda b, h, i: (b * nq_tiles + i, h)))
    picked = []
    if head_rows:
        in_specs += [pl.BlockSpec((None, None, seq * rows_per_key, LANES), lambda b, h, i: (b, slot, 0, 0))] * 2
        picked.append(seq)
    else:
        in_specs += [pl.BlockSpec((seq, kw), lambda b, h, i: (b, h))] * 2
        operands += list(cache)
        in_specs += [pl.BlockSpec((None, None, n_ctx * rows_per_key, LANES), lambda b, h, i: (b, slot, 0, 0))] * 2
        picked.append(n_ctx)
    kern = functools.partial(_diff_attn_kernel, tq=tq, block_rows=block_rows, n_new=seq, n_ctx=n_ctx,
                             lam_init=lam_init, head_rows=head_rows)
    score_shape = (DIFF_GROUP * tq, seq + n_ctx)
    ctx_scratch = [pltpu.VMEM((n, 2 * DIFF_DIM), BF16) for n in picked for _ in range(2)]
    return pl.pallas_call(
        kern,
        out_shape=jax.ShapeDtypeStruct(q.shape, BF16),
        grid=(batch, DIFF_KV_HEADS, nq_tiles),
        in_specs=in_specs,
        out_specs=pl.BlockSpec((tq, qw), lambda b, h, i: (b * nq_tiles + i, h)),
        scratch_shapes=[pltpu.VMEM(score_shape, F32), pltpu.VMEM(score_shape, F32),
                        pltpu.VMEM(score_shape, BF16)] + ctx_scratch,
        compiler_params=_params("parallel", "parallel", "arbitrary"),
        name="diff_attn",
    )(*operands)


def _oproj_kernel(o_ref, w_ref, x_ref, mod_ref, g_ref, out_ref, h_ref, *, d, row_blocks, row_of_tile):
    mod = mod_ref[pl.ds(row_of_tile(pl.program_id(0)), 1), :]
    gain1 = mod[:, 2 * d:3 * d] * g_ref[1:2, :]
    gain2 = g_ref[2:3, :] * (1.0 + mod[:, 4 * d:5 * d])
    shift2 = mod[:, 3 * d:4 * d]
    rows = o_ref.shape[0] // row_blocks
    for r in range(row_blocks):
        rs = slice(r * rows, (r + 1) * rows)
        m = jnp.dot(o_ref[rs, :], w_ref[...], preferred_element_type=F32)
        x_new = x_ref[rs, :] + _rms(m) * gain1
        out_ref[rs, :] = x_new
        h_ref[rs, :] = (_rms(x_new) * gain2 + shift2).astype(BF16)


def _oproj(o, w_o, x, mod, g_norm, layer, slot, row_of_tile, *, tm):
    t, d = x.shape
    nin = o.shape[1]
    return pl.pallas_call(
        functools.partial(_oproj_kernel, d=d,
                          row_blocks=OPROJ_ROW_BLOCKS if tm % (OPROJ_ROW_BLOCKS * ROW_CHUNK) == 0 else 1,
                          row_of_tile=row_of_tile),
        out_shape=(jax.ShapeDtypeStruct((t, d), F32), jax.ShapeDtypeStruct((t, d), BF16)),
        grid=(t // tm,),
        in_specs=[
            pl.BlockSpec((tm, nin), lambda i: (i, 0)),
            pl.BlockSpec((None, nin, d), lambda i: (slot, 0, 0), pipeline_mode=pl.Buffered(1)),
            pl.BlockSpec((tm, d), lambda i: (i, 0)),
            pl.BlockSpec((None,) + mod.shape[1:], lambda i: (layer, 0, 0)),
            pl.BlockSpec((None, 4, d), lambda i: (layer, 0, 0)),
        ],
        out_specs=(pl.BlockSpec((tm, d), lambda i: (i, 0)), pl.BlockSpec((tm, d), lambda i: (i, 0))),
        compiler_params=_params("parallel"),
        name="out_proj",
    )(o, w_o, x, mod, g_norm)


def _ffn_kernel(h_ref, x_ref, mod_ref, g_ref, wg_ref, wu_ref, wd_ref, out_ref, *, d, row_of_tile):
    j = pl.program_id(1)
    last = pl.num_programs(1) - 1

    def partial_sum(rows):
        h = h_ref[rows, :]
        a = jnp.dot(h, wg_ref[...], preferred_element_type=F32)
        b = jnp.dot(h, wu_ref[...], preferred_element_type=F32)
        u = ((a * jax.nn.sigmoid(a)) * b).astype(BF16)
        return jnp.dot(u, wd_ref[...], preferred_element_type=F32)

    @pl.when(j == 0)
    def _():
        out_ref[...] = partial_sum(slice(None))

    @pl.when(jnp.logical_and(j > 0, j < last))
    def _():
        out_ref[...] += partial_sum(slice(None))

    @pl.when(j == last)
    def _():
        gate = mod_ref[pl.ds(row_of_tile(pl.program_id(0)), 1), 5 * d:6 * d]
        gain = gate * g_ref[3:4, :]
        rows = out_ref.shape[0] // FFN_EPILOGUE_BLOCKS
        for r in range(FFN_EPILOGUE_BLOCKS):
            rs = slice(r * rows, (r + 1) * rows)
            acc = out_ref[rs, :] + partial_sum(rs)
            out_ref[rs, :] = x_ref[rs, :] + _rms(acc) * gain


def _ffn(h, x, mod, g_norm, w_gate, w_up, w_down, layer, row_of_tile, *, tm, fc):
    t, d = x.shape
    d_ff = w_gate.shape[-1]
    return pl.pallas_call(
        functools.partial(_ffn_kernel, d=d, row_of_tile=row_of_tile),
        out_shape=jax.ShapeDtypeStruct((t, d), F32),
        grid=(t // tm, d_ff // fc),
        in_specs=[
            pl.BlockSpec((tm, d), lambda i, j: (i, 0)),
            pl.BlockSpec((tm, d), lambda i, j: (i, 0)),
            pl.BlockSpec((None,) + mod.shape[1:], lambda i, j: (layer, 0, 0)),
            pl.BlockSpec((None, 4, d), lambda i, j: (layer, 0, 0)),
            pl.BlockSpec((None, d, fc), lambda i, j: (layer, 0, j)),
            pl.BlockSpec((None, d, fc), lambda i, j: (layer, 0, j)),
            pl.BlockSpec((None, fc, d), lambda i, j: (layer, j, 0)),
        ],
        out_specs=pl.BlockSpec((tm, d), lambda i, j: (i, 0)),
        compiler_params=_params("parallel", "arbitrary"),
        name="ffn",
    )(h, x, mod, g_norm, w_gate, w_up, w_down)


def _rope_tables(n_tokens):
    nf = ROPE_PAIR_LANES
    t = jnp.arange(n_tokens, dtype=jnp.int32)
    rows = (t // GRID_W).astype(F32)
    cols = (t % GRID_W).astype(F32)
    inv = 1.0 / (ROPE_THETA ** (jnp.arange(nf, dtype=F32) / nf))
    ar = rows[:, None] * inv
    ac = cols[:, None] * inv
    cos = jnp.concatenate([jnp.cos(ar), jnp.cos(ar), jnp.cos(ac), jnp.cos(ac)], axis=-1)
    sin = jnp.concatenate([-jnp.sin(ar), jnp.sin(ar), -jnp.sin(ac), jnp.sin(ac)], axis=-1)
    return cos, sin


def _diff_lambda_init(layer):
    return 0.8 - 0.6 * math.exp(-0.3 * layer)


def _largest_tile(n, cap):
    t = cap
    while n % t:
        t //= 2
    return t


def _run_path(x3, first_row, rows_per_batch, caches, mod, p, depth):
    batch, seq, d = x3.shape
    latent = caches is not None
    x = x3.reshape(batch * seq, d)
    tm = _largest_tile(seq if latent else batch * seq, TOKEN_TILE)
    gqa_tq = _largest_tile(seq, LATENT_GQA_TQ if latent else ATTN_TQ)

    def row_of_tile(i):
        if not latent:
            return first_row
        return first_row + (i // (seq // tm)) * rows_per_batch

    rope_tables = _rope_tables(seq) if latent else None
    new_state = [None] * N_MIXERS
    for layer in range(depth):
        kind, slot = layer % N_MIXERS, layer // N_MIXERS
        cache = None if not latent else (caches[2 * kind], caches[2 * kind + 1])
        common = dict(tm=tm, rope_tables=rope_tables)
        if not latent:
            n_slots = (depth - kind + N_MIXERS - 1) // N_MIXERS
            common.update(state_seq=seq, state_slots=n_slots, state_prev=new_state[kind])
        attn_shape = dict(kv_heads=1 if latent else KV_HEADS)
        if kind == 0:
            q, k, v = _qkv(x, mod, p['g_norm'], p['w_qkv_full'], layer, slot, row_of_tile,
                           nq=N_HEADS * HEAD_DIM, nk=KV_HEADS * HEAD_DIM,
                           gains=(p['g_q_full'], p['g_k_full']), **common)
            o = _gqa_attention(q, k, v, cache, slot, None, batch=batch, seq=seq, band=False, tq=gqa_tq,
                               block_rows=DENSE_BLOCK_ROWS if latent else ATTN_BLOCK_ROWS, **attn_shape)
            w_o = p['w_o_full']
        elif kind == 1:
            q, k, v = _qkv(x, mod, p['g_norm'], p['w_qkv_win'], layer, slot, row_of_tile,
                           nq=N_HEADS * HEAD_DIM, nk=KV_HEADS * HEAD_DIM, **common)
            o = _gqa_attention(q, k, v, cache, slot, p['sink_win'][slot], batch=batch, seq=seq, band=latent,
                               tq=gqa_tq, block_rows=ATTN_BLOCK_ROWS, **attn_shape)
            w_o = p['w_o_win']
        else:
            q, k, v = _qkv(x, mod, p['g_norm'], p['w_qkv_diff'], layer, slot, row_of_tile,
                           nq=DIFF_HEADS * 2 * DIFF_DIM, nk=DIFF_KV_HEADS * 2 * DIFF_DIM, **common)
            o = _diff_attention(q, k, v, cache, slot,
                                (p['lam_q1'], p['lam_k1'], p['lam_q2'], p['lam_k2']), p['g_subln_diff'],
                                batch=batch, seq=seq, tq=_largest_tile(seq, ATTN_TQ),
                                block_rows=2 * ATTN_BLOCK_ROWS,
                                lam_init=_diff_lambda_init(layer))
            w_o = p['w_o_diff']
        if not latent:
            new_state[kind] = (k, v)
        x, h = _oproj(o, w_o, x, mod, p['g_norm'], layer, slot, row_of_tile, tm=tm)
        x = _ffn(h, x, mod, p['g_norm'], p['w_gate'], p['w_up'], p['w_down'], layer, row_of_tile,
                 tm=tm, fc=FFN_CHUNK)
    return x.reshape(batch, seq, d), new_state


def kernel(x_prompt, x_sample, c, cache_k_full, cache_v_full, cache_k_win, cache_v_win, cache_k_diff,
           cache_v_diff, c_ctx, w_ada, b_ada, g_norm, w_qkv_full, w_o_full, g_q_full, g_k_full, w_qkv_win,
           w_o_win, sink_win, w_qkv_diff, w_o_diff, lam_q1, lam_k1, lam_q2, lam_k2, g_subln_diff, w_gate,
           w_up, w_down):
    depth, d = w_ada.shape[0], w_ada.shape[1]
    batch, seq, _ = x_prompt.shape
    dec_batch, dec_seq, _ = x_sample.shape

    p = {
        'g_norm': g_norm,
        'w_qkv_full': w_qkv_full.astype(BF16), 'w_o_full': w_o_full.astype(BF16),
        'g_q_full': g_q_full, 'g_k_full': g_k_full,
        'w_qkv_win': w_qkv_win.astype(BF16), 'w_o_win': w_o_win.astype(BF16), 'sink_win': sink_win,
        'w_qkv_diff': w_qkv_diff.astype(BF16), 'w_o_diff': w_o_diff.astype(BF16),
        'lam_q1': lam_q1, 'lam_k1': lam_k1, 'lam_q2': lam_q2, 'lam_k2': lam_k2,
        'g_subln_diff': g_subln_diff,
        'w_gate': w_gate.astype(BF16), 'w_up': w_up.astype(BF16), 'w_down': w_down.astype(BF16),
    }

    n_rows = -(-(1 + dec_batch) // SUBLANES) * SUBLANES
    cond = jnp.zeros((n_rows, d), F32).at[0].set(c_ctx).at[1:1 + dec_batch].set(c)
    mod = _modulation(cond, w_ada, b_ada)

    y_prompt, new_state = _run_path(x_prompt, 0, 0, None, mod, p, depth)

    def head_rows_cache(a):
        return a.reshape(a.shape[0], a.shape[1], -1, LANES)

    caches = tuple(head_rows_cache(a) for a in (cache_k_full, cache_v_full, cache_k_win, cache_v_win,
                                                cache_k_diff, cache_v_diff))
    y_sample, _ = _run_path(x_sample, 1, 1, caches, mod, p, depth)

    def state(a, tail):
        return a.reshape(a.shape[:2] + (seq,) + tail)

    (k_full, v_full), (k_win, v_win), (k_diff, v_diff) = new_state
    return (y_prompt, y_sample,
            state(k_full, (KV_HEADS, HEAD_DIM)), state(v_full, (KV_HEADS, HEAD_DIM)),
            state(k_win, (KV_HEADS, HEAD_DIM)), state(v_win, (KV_HEADS, HEAD_DIM)),
            state(k_diff, (DIFF_KV_HEADS, 2, DIFF_DIM)), state(v_diff, (DIFF_KV_HEADS, 2 * DIFF_DIM)))
```

```python
import functools
import math

import jax
import jax.numpy as jnp
from jax import lax
from jax.experimental import pallas as pl
from jax.experimental.pallas import tpu as pltpu

F32 = jnp.float32
BF16 = jnp.bfloat16

N_MIXERS = 3
N_HEADS = 16
KV_HEADS = 4
HEAD_DIM = 128
GROUP = N_HEADS // KV_HEADS
WINDOW = 128
GRID_W = 64
DIFF_HEADS = 8
DIFF_KV_HEADS = 2
DIFF_GROUP = DIFF_HEADS // DIFF_KV_HEADS
DIFF_DIM = 128
ROPE_THETA = 10000.0
EPS = 1e-6
NEG_INF = -1e30

LANES = 128
SUBLANES = 8
V7X_VMEM_BYTES = 64 * 1024 * 1024
VMEM_LIMIT = V7X_VMEM_BYTES - 8 * 1024 * 1024

NT_DIMS = (((1,), (1,)), ((), ()))
LOG2E = math.log2(math.e)
ROPE_PAIR_LANES = HEAD_DIM // 4

ROW_CHUNK = 2 * SUBLANES
TOKEN_TILE = 512
MOD_COLS = 1024
QKV_COLS = 512
OPROJ_ROW_BLOCKS = 4
ATTN_TQ = 256
LATENT_GQA_TQ = 512
ATTN_BLOCK_ROWS = 256
DENSE_BLOCK_ROWS = 128
FFN_CHUNK = 512
FFN_EPILOGUE_BLOCKS = 2
WEIGHT_SLOTS = 3


def _params(*semantics):
    return pltpu.CompilerParams(dimension_semantics=semantics, vmem_limit_bytes=VMEM_LIMIT)


def _rms(x):
    return x * lax.rsqrt(jnp.mean(x * x, axis=-1, keepdims=True) + EPS)


def _mod_kernel(c_ref, w_ref, b_ref, o_ref):
    c = c_ref[...]
    a = (c * jax.nn.sigmoid(c)).astype(BF16)
    o_ref[...] = jnp.dot(a, w_ref[...].astype(BF16), preferred_element_type=F32) + b_ref[...]


def _modulation(cond, w_ada, b_ada):
    n_layers, d, n = w_ada.shape
    rows = cond.shape[0]
    tn = MOD_COLS
    return pl.pallas_call(
        _mod_kernel,
        out_shape=jax.ShapeDtypeStruct((n_layers, rows, n), F32),
        grid=(n_layers, n // tn),
        in_specs=[
            pl.BlockSpec((rows, d), lambda l, j: (0, 0)),
            pl.BlockSpec((None, d, tn), lambda l, j: (l, 0, j)),
            pl.BlockSpec((None, 1, tn), lambda l, j: (l, 0, j)),
        ],
        out_specs=pl.BlockSpec((None, rows, tn), lambda l, j: (l, 0, j)),
        compiler_params=_params("parallel", "parallel"),
        name="adaln_mod",
    )(cond, w_ada, b_ada.reshape(n_layers, 1, n))


def _qkv_kernel(*refs, d, nq, nk, qk_norm, rope, q_scale, row_of_tile, head_rows_seq, n_aliased, owned_slot):
    it = iter(refs)
    x_ref, mod_ref, g_ref, w_ref = next(it), next(it), next(it), next(it)
    gq_ref = gk_ref = cos_ref = sin_ref = None
    if qk_norm:
        gq_ref, gk_ref = next(it), next(it)
    if rope:
        cos_ref, sin_ref = next(it), next(it)
    for _ in range(n_aliased):
        next(it)
    q_ref, k_ref, v_ref = next(it), next(it), next(it)

    def store_chunk(ref, chunk, n_chunks, z):
        if head_rows_seq is None:
            ref[:, chunk * LANES:(chunk + 1) * LANES] = z.astype(ref.dtype)
        else:
            seq = head_rows_seq
            for b in range(z.shape[0] // seq):
                dst = ref.at[b] if owned_slot is None else ref.at[b, owned_slot]
                dst[pl.ds(chunk, seq, stride=n_chunks), :] = z[b * seq:(b + 1) * seq, :].astype(ref.dtype)

    if owned_slot is not None:
        for ref in (k_ref, v_ref):
            for s in range(ref.shape[1]):
                if s != owned_slot:
                    ref[:, s] = jnp.zeros((ref.shape[0],) + ref.shape[2:], ref.dtype)

    mod = mod_ref[pl.ds(row_of_tile(pl.program_id(0)), 1), :]
    h = (_rms(x_ref[...]) * (g_ref[0:1, :] * (1.0 + mod[:, d:2 * d])) + mod[:, 0:d]).astype(BF16)

    def partner(a):
        lane = lax.broadcasted_iota(jnp.int32, a.shape, 1)
        upper = (lane & ROPE_PAIR_LANES) != 0
        return jnp.where(upper, pltpu.roll(a, ROPE_PAIR_LANES, 1), pltpu.roll(a, LANES - ROPE_PAIR_LANES, 1))

    def lane_factors(gain_ref, scale):
        gain = None if gain_ref is None else jnp.broadcast_to(gain_ref[...] * scale, (SUBLANES, LANES))
        if not rope:
            return (scale if gain is None else gain[0:1, :]), None
        if gain is None:
            return cos_ref[...] * scale, sin_ref[...] * scale
        return cos_ref[...] * gain[0:1, :], sin_ref[...] * partner(gain)[0:1, :]

    def finish(z, normed, factors):
        c, s = factors
        if s is not None:
            t = z * c + partner(z) * s
        else:
            t = z if isinstance(c, float) and c == 1.0 else z * c
        if normed:
            t = t * lax.rsqrt(jnp.mean(z * z, axis=-1, keepdims=True) + EPS)
        return t

    q_factors = lane_factors(gq_ref, q_scale)
    k_factors = lane_factors(gk_ref, 1.0)
    chunk = QKV_COLS
    for c0 in range(0, nq + nk, chunk):
        acc = jnp.dot(h, w_ref[:, c0:c0 + chunk], preferred_element_type=F32)
        for j in range(chunk // LANES):
            col = c0 + j * LANES
            z = acc[:, j * LANES:(j + 1) * LANES]
            if col < nq:
                q_ref[:, col:col + LANES] = finish(z, qk_norm, q_factors).astype(q_ref.dtype)
            else:
                store_chunk(k_ref, (col - nq) // LANES, nk // LANES, finish(z, qk_norm, k_factors))
    v = jnp.dot(h, w_ref[:, nq + nk:], preferred_element_type=F32)
    n_v_chunks = v.shape[1] // LANES
    for c in range(n_v_chunks):
        store_chunk(v_ref, c, n_v_chunks, v[:, c * LANES:(c + 1) * LANES])


def _qkv(x, mod, g_norm, w_qkv, layer, slot, row_of_tile, *, tm, nq, nk, gains=None, rope_tables=None,
         state_seq=None, state_slots=1, state_prev=None):
    t, d = x.shape
    assert t % tm == 0 and (state_seq is None or tm % state_seq == 0)
    n_all = w_qkv.shape[-1]
    nv = n_all - nq - nk
    operands = [x, mod, g_norm, w_qkv]
    in_specs = [
        pl.BlockSpec((tm, d), lambda i: (i, 0)),
        pl.BlockSpec((None,) + mod.shape[1:], lambda i: (layer, 0, 0)),
        pl.BlockSpec((None, 4, d), lambda i: (layer, 0, 0)),
        pl.BlockSpec((None, d, n_all), lambda i: (slot, 0, 0), pipeline_mode=pl.Buffered(1)),
    ]
    if gains is not None:
        for g in gains:
            operands.append(g.reshape(g.shape[0], 1, LANES))
            in_specs.append(pl.BlockSpec((None, 1, LANES), lambda i: (slot, 0, 0)))
    if rope_tables is not None:
        tiles_per_seq = rope_tables[0].shape[0] // tm
        for tab in rope_tables:
            operands.append(tab)
            in_specs.append(pl.BlockSpec((tm, LANES), lambda i: (i % tiles_per_seq, 0)))
    aliases = {}
    owned_slot = None
    if state_seq is None:
        kv_shapes = [jax.ShapeDtypeStruct((t, n), BF16) for n in (nk, nv)]
        kv_specs = [pl.BlockSpec((tm, n), lambda i: (i, 0)) for n in (nk, nv)]
    else:
        batches = tm // state_seq
        kv_shapes = [jax.ShapeDtypeStruct((t // state_seq, state_slots, state_seq * (n // LANES), LANES), F32)
                     for n in (nk, nv)]
        if state_prev is None:
            owned_slot = slot
            kv_specs = [pl.BlockSpec((batches,) + s.shape[1:], lambda i: (i, 0, 0, 0)) for s in kv_shapes]
        else:
            kv_specs = [pl.BlockSpec((batches, None) + s.shape[2:], lambda i: (i, slot, 0, 0)) for s in kv_shapes]
            for j, prev in enumerate(state_prev):
                aliases[len(operands)] = 1 + j
                operands.append(prev)
                in_specs.append(pl.BlockSpec(memory_space=pl.ANY))
    kern = functools.partial(_qkv_kernel, d=d, nq=nq, nk=nk, qk_norm=gains is not None,
                             rope=rope_tables is not None, q_scale=HEAD_DIM ** -0.5 * LOG2E,
                             row_of_tile=row_of_tile, head_rows_seq=state_seq, n_aliased=len(aliases),
                             owned_slot=owned_slot)
    return pl.pallas_call(
        kern,
        out_shape=[jax.ShapeDtypeStruct((t, nq), BF16)] + kv_shapes,
        grid=(t // tm,),
        in_specs=in_specs,
        out_specs=[pl.BlockSpec((tm, nq), lambda i: (i, 0))] + kv_specs,
        input_output_aliases=aliases,
        compiler_params=_params("parallel"),
        name="qkv_proj",
    )(*operands)


def _gqa_attn_kernel(*refs, tq, kv_heads, block_rows, n_new, n_ctx, band, has_sink, head_rows):
    it = iter(refs)
    sink_ref = next(it) if has_sink else None
    q_ref, k_ref, v_ref = next(it), next(it), next(it)
    kc_ref = vc_ref = None
    if n_ctx:
        kc_ref, vc_ref = next(it), next(it)
    o_ref = next(it)
    s_scr, p_scr = next(it), next(it)
    bias_scr = next(it) if band else None
    if n_ctx:
        kc_scr, vc_scr = next(it), next(it)
        for head in range(KV_HEADS):
            @pl.when(jnp.logical_and(pl.program_id(1) == head, pl.program_id(2) == 0))
            def _():
                kc_scr[...] = kc_ref[pl.ds(head, n_ctx, stride=KV_HEADS), :].astype(BF16)
                vc_scr[...] = vc_ref[pl.ds(head, n_ctx, stride=KV_HEADS), :].astype(BF16)

    band_width = tq + 2 * WINDOW
    q0 = pl.program_id(2) * tq
    if head_rows:
        nk = n_new

        def head_kv(kvh):
            rows = pl.ds(kvh, n_new, stride=KV_HEADS)
            return k_ref[rows, :].astype(BF16), v_ref[rows, :].astype(BF16)
    else:
        if band and n_new > band_width:
            start = pl.multiple_of(jnp.clip(q0 - WINDOW, 0, n_new - band_width), LANES)
            k = k_ref[pl.ds(start, band_width), :]
            v = v_ref[pl.ds(start, band_width), :]
        else:
            start = 0
            k = k_ref[...]
            v = v_ref[...]
        nk = k.shape[0]
        if n_ctx:
            k = jnp.concatenate([k, kc_scr[...]], axis=0)
            v = jnp.concatenate([v, vc_scr[...]], axis=0)

        def head_kv(kvh):
            cols = slice(kvh * HEAD_DIM, (kvh + 1) * HEAD_DIM)
            return k[:, cols], v[:, cols]
    if band:
        qpos = q0 + lax.broadcasted_iota(jnp.int32, (tq, nk), 0)
        kpos = start + lax.broadcasted_iota(jnp.int32, (tq, nk), 1)
        bias_scr[...] = jnp.where(jnp.abs(qpos - kpos) <= WINDOW, 0.0, NEG_INF).astype(F32)
    ones = jnp.ones((nk + n_ctx, HEAD_DIM), BF16)

    if block_rows <= tq:
        blocks = [((h,), r0, block_rows) for h in range(GROUP) for r0 in range(0, tq, block_rows)]
    else:
        stacked = block_rows // tq
        blocks = [(tuple(range(h0, h0 + stacked)), 0, tq) for h0 in range(0, GROUP, stacked)]

    base = 0
    for kvh in range(kv_heads):
        k_h, v_h = head_kv(kvh)
        v_ext = jnp.concatenate([v_h, ones], axis=1)
        for heads, row0, n in blocks:
            rows = len(heads) * n
            cols = [(kvh * GROUP + h) * HEAD_DIM for h in heads]
            qb = jnp.concatenate([q_ref[row0:row0 + n, c:c + HEAD_DIM] for c in cols], axis=0)
            s_scr[base:base + rows, :] = lax.dot_general(qb, k_h, NT_DIMS, preferred_element_type=F32)
            sink_terms = []
            for r in range(0, rows, ROW_CHUNK):
                rs = slice(base + r, base + r + ROW_CHUNK)
                if band:
                    qr = row0 + r % n
                    parts = [s_scr[rs, 0:nk] + bias_scr[qr:qr + ROW_CHUNK, :]]
                    if n_ctx:
                        parts.append(s_scr[rs, nk:])
                else:
                    parts = [s_scr[rs, :]]
                m = functools.reduce(jnp.maximum, [jnp.max(a, axis=-1, keepdims=True) for a in parts])
                if has_sink:
                    head = (pl.program_id(1) * kv_heads + kvh) * GROUP + heads[r // n]
                    sk = sink_ref[head] * LOG2E
                    m = jnp.maximum(m, sk)
                    sink_terms.append(jnp.exp2(sk - jnp.broadcast_to(m, (ROW_CHUNK, HEAD_DIM))))
                col = 0
                for a in parts:
                    p_scr[rs, col:col + a.shape[1]] = jnp.exp2(a - m).astype(BF16)
                    col += a.shape[1]
            acc = jnp.dot(p_scr[base:base + rows, :], v_ext, preferred_element_type=F32)
            denom = acc[:, HEAD_DIM:]
            if has_sink:
                denom = denom + jnp.concatenate(sink_terms, axis=0)
            o = acc[:, :HEAD_DIM] / denom
            for j, c in enumerate(cols):
                o_ref[row0:row0 + n, c:c + HEAD_DIM] = o[j * n:(j + 1) * n, :].astype(o_ref.dtype)
            base += rows


def _gqa_attention(q, k, v, cache, slot, sink, *, batch, seq, tq, kv_heads, block_rows, band):
    nq_tiles = seq // tq
    head_rows = cache is None
    n_ctx = 0 if cache is None else cache[0].shape[2] // KV_HEADS
    assert kv_heads == (KV_HEADS if head_rows else 1) and not (head_rows and band)
    qw, kw = kv_heads * GROUP * HEAD_DIM, kv_heads * HEAD_DIM
    operands, in_specs = [], []
    if sink is not None:
        operands.append(sink)
        in_specs.append(pl.BlockSpec(memory_space=pltpu.SMEM))
    operands += [q, k, v]
    in_specs.append(pl.BlockSpec((tq, qw), lambda b, h, i: (b * nq_tiles + i, h)))
    if head_rows:
        in_specs += [pl.BlockSpec((None, None, seq * KV_HEADS, HEAD_DIM), lambda b, h, i: (b, slot, 0, 0))] * 2
    else:
        in_specs += [pl.BlockSpec((seq, kw), lambda b, h, i: (b, h))] * 2
        operands += list(cache)
        in_specs += [pl.BlockSpec((None, None, n_ctx * KV_HEADS, HEAD_DIM), lambda b, h, i: (b, slot, 0, 0))] * 2
    kern = functools.partial(_gqa_attn_kernel, tq=tq, kv_heads=kv_heads, block_rows=block_rows, n_new=seq,
                             n_ctx=n_ctx, band=band, has_sink=sink is not None, head_rows=head_rows)
    n_new_keys = min(seq, tq + 2 * WINDOW) if band else seq
    n_keys = n_new_keys + n_ctx
    scratch = [pltpu.VMEM((kv_heads * GROUP * tq, n_keys), F32), pltpu.VMEM((kv_heads * GROUP * tq, n_keys), BF16)]
    if band:
        scratch.append(pltpu.VMEM((tq, n_new_keys), F32))
    if n_ctx:
        scratch += [pltpu.VMEM((n_ctx, HEAD_DIM), BF16)] * 2
    return pl.pallas_call(
        kern,
        out_shape=jax.ShapeDtypeStruct(q.shape, BF16),
        grid=(batch, KV_HEADS // kv_heads, nq_tiles),
        in_specs=in_specs,
        out_specs=pl.BlockSpec((tq, qw), lambda b, h, i: (b * nq_tiles + i, h)),
        scratch_shapes=scratch,
        compiler_params=_params("parallel", "parallel", "arbitrary"),
        name="gqa_attn",
    )(*operands)


def _diff_attn_kernel(*refs, tq, block_rows, n_new, n_ctx, lam_init, head_rows):
    it = iter(refs)
    lq1_ref, lk1_ref, lq2_ref, lk2_ref, gs_ref = next(it), next(it), next(it), next(it), next(it)
    q_ref, k_ref, v_ref = next(it), next(it), next(it)
    kc_ref = vc_ref = None
    if n_ctx:
        kc_ref, vc_ref = next(it), next(it)
    o_ref = next(it)
    s_scrs = (next(it), next(it))
    p_scr = next(it)

    def pick_head(k_src, v_src, n_keys):
        k_dst, v_dst = next(it), next(it)
        for head in range(DIFF_KV_HEADS):
            @pl.when(jnp.logical_and(pl.program_id(1) == head, pl.program_id(2) == 0))
            def _():
                for half in range(2):
                    rows = pl.ds(2 * head + half, n_keys, stride=2 * DIFF_KV_HEADS)
                    cols = slice(half * DIFF_DIM, (half + 1) * DIFF_DIM)
                    k_dst[:, cols] = k_src[rows, :].astype(BF16)
                    v_dst[:, cols] = v_src[rows, :].astype(BF16)
        return k_dst[...], v_dst[...]

    lam = (jnp.exp(jnp.sum(lq1_ref[...] * lk1_ref[...], axis=-1, keepdims=True))
           - jnp.exp(jnp.sum(lq2_ref[...] * lk2_ref[...], axis=-1, keepdims=True)) + lam_init)
    if head_rows:
        k, v = pick_head(k_ref, v_ref, n_new)
    else:
        k, v = k_ref[...], v_ref[...]
    if n_ctx:
        kc, vc = pick_head(kc_ref, vc_ref, n_ctx)
        k = jnp.concatenate([k, kc], axis=0)
        v = jnp.concatenate([v, vc], axis=0)

    width = 2 * DIFF_DIM
    if block_rows <= tq:
        blocks = [((g,), r0, block_rows) for g in range(DIFF_GROUP) for r0 in range(0, tq, block_rows)]
    else:
        stacked = block_rows // tq
        blocks = [(tuple(range(g0, g0 + stacked)), 0, tq) for g0 in range(0, DIFF_GROUP, stacked)]
    base = 0
    for heads, row0, n in blocks:
        rows = len(heads) * n
        for half in range(2):
            qb = jnp.concatenate(
                [q_ref[row0:row0 + n, (2 * g + half) * DIFF_DIM:(2 * g + half + 1) * DIFF_DIM] for g in heads],
                axis=0)
            s_scrs[half][base:base + rows, :] = lax.dot_general(
                qb, k[:, half * DIFF_DIM:(half + 1) * DIFF_DIM], NT_DIMS, preferred_element_type=F32)
        for r in range(0, rows, ROW_CHUNK):
            rs = slice(base + r, base + r + ROW_CHUNK)
            s0 = s_scrs[0][rs, :]
            e0 = jnp.exp2(s0 - jnp.max(s0, axis=-1, keepdims=True))
            r0 = 1.0 / jnp.sum(e0, axis=-1, keepdims=True)
            s1 = s_scrs[1][rs, :]
            e1 = jnp.exp2(s1 - jnp.max(s1, axis=-1, keepdims=True))
            r1 = lam / jnp.sum(e1, axis=-1, keepdims=True)
            p_scr[rs, :] = (e0 * r0 - e1 * r1).astype(BF16)
        acc = jnp.dot(p_scr[base:base + rows, :], v, preferred_element_type=F32)
        y = (_rms(acc) * gs_ref[...]) * (1.0 - lam_init)
        for j, g in enumerate(heads):
            o_ref[row0:row0 + n, g * width:(g + 1) * width] = y[j * n:(j + 1) * n, :].astype(o_ref.dtype)
        base += rows


def _diff_attention(q, k, v, cache, slot, lam_vecs, g_subln, *, batch, seq, tq, block_rows, lam_init):
    nq_tiles = seq // tq
    head_rows = cache is None
    rows_per_key = 2 * DIFF_KV_HEADS
    n_ctx = 0 if cache is None else cache[0].shape[2] // rows_per_key
    qw = DIFF_GROUP * 2 * DIFF_DIM
    kw = 2 * DIFF_DIM
    operands = [a.reshape(a.shape[0], 1, DIFF_DIM) for a in lam_vecs]
    in_specs = [pl.BlockSpec((None, 1, DIFF_DIM), lambda b, h, i: (slot, 0, 0))] * 4
    operands.append(g_subln.reshape(g_subln.shape[0], 1, 2 * DIFF_DIM))
    in_specs.append(pl.BlockSpec((None, 1, 2 * DIFF_DIM), lambda b, h, i: (slot, 0, 0)))
    operands += [q, k, v]
    in_specs.append(pl.BlockSpec((tq, qw), lambda b, h, i: (b * nq_tiles + i, h)))
    picked = []
    if head_rows:
        in_specs += [pl.BlockSpec((None, None, seq * rows_per_key, LANES), lambda b, h, i: (b, slot, 0, 0))] * 2
        picked.append(seq)
    else:
        in_specs += [pl.BlockSpec((seq, kw), lambda b, h, i: (b, h))] * 2
        operands += list(cache)
        in_specs += [pl.BlockSpec((None, None, n_ctx * rows_per_key, LANES), lambda b, h, i: (b, slot, 0, 0))] * 2
        picked.append(n_ctx)
    kern = functools.partial(_diff_attn_kernel, tq=tq, block_rows=block_rows, n_new=seq, n_ctx=n_ctx,
                             lam_init=lam_init, head_rows=head_rows)
    score_shape = (DIFF_GROUP * tq, seq + n_ctx)
    ctx_scratch = [pltpu.VMEM((n, 2 * DIFF_DIM), BF16) for n in picked for _ in range(2)]
    return pl.pallas_call(
        kern,
        out_shape=jax.ShapeDtypeStruct(q.shape, BF16),
        grid=(batch, DIFF_KV_HEADS, nq_tiles),
        in_specs=in_specs,
        out_specs=pl.BlockSpec((tq, qw), lambda b, h, i: (b * nq_tiles + i, h)),
        scratch_shapes=[pltpu.VMEM(score_shape, F32), pltpu.VMEM(score_shape, F32),
                        pltpu.VMEM(score_shape, BF16)] + ctx_scratch,
        compiler_params=_params("parallel", "parallel", "arbitrary"),
        name="diff_attn",
    )(*operands)


def _oproj_kernel(o_ref, w_ref, x_ref, mod_ref, g_ref, out_ref, h_ref, *, d, row_blocks, row_of_tile):
    mod = mod_ref[pl.ds(row_of_tile(pl.program_id(0)), 1), :]
    gain1 = mod[:, 2 * d:3 * d] * g_ref[1:2, :]
    gain2 = g_ref[2:3, :] * (1.0 + mod[:, 4 * d:5 * d])
    shift2 = mod[:, 3 * d:4 * d]
    rows = o_ref.shape[0] // row_blocks
    for r in range(row_blocks):
        rs = slice(r * rows, (r + 1) * rows)
        m = jnp.dot(o_ref[rs, :], w_ref[...], preferred_element_type=F32)
        x_new = x_ref[rs, :] + _rms(m) * gain1
        out_ref[rs, :] = x_new
        h_ref[rs, :] = (_rms(x_new) * gain2 + shift2).astype(BF16)


def _oproj(o, w_o, x, mod, g_norm, layer, slot, row_of_tile, *, tm):
    t, d = x.shape
    nin = o.shape[1]
    return pl.pallas_call(
        functools.partial(_oproj_kernel, d=d,
                          row_blocks=OPROJ_ROW_BLOCKS if tm % (OPROJ_ROW_BLOCKS * ROW_CHUNK) == 0 else 1,
                          row_of_tile=row_of_tile),
        out_shape=(jax.ShapeDtypeStruct((t, d), F32), jax.ShapeDtypeStruct((t, d), BF16)),
        grid=(t // tm,),
        in_specs=[
            pl.BlockSpec((tm, nin), lambda i: (i, 0)),
            pl.BlockSpec((None, nin, d), lambda i: (slot, 0, 0), pipeline_mode=pl.Buffered(1)),
            pl.BlockSpec((tm, d), lambda i: (i, 0)),
            pl.BlockSpec((None,) + mod.shape[1:], lambda i: (layer, 0, 0)),
            pl.BlockSpec((None, 4, d), lambda i: (layer, 0, 0)),
        ],
        out_specs=(pl.BlockSpec((tm, d), lambda i: (i, 0)), pl.BlockSpec((tm, d), lambda i: (i, 0))),
        compiler_params=_params("parallel"),
        name="out_proj",
    )(o, w_o, x, mod, g_norm)


def _ffn_kernel(h_ref, x_ref, mod_ref, g_ref, wg_hbm, wu_hbm, wd_hbm, out_ref, wg_buf, wu_buf, wd_buf, sems,
                *, d, fc, layer, row_of_tile):
    j = pl.program_id(1)
    n_chunks = pl.num_programs(1)
    last = n_chunks - 1
    step = pl.program_id(0) * n_chunks + j
    n_steps = pl.num_programs(0) * n_chunks

    def weight_copies(s):
        slot = s % WEIGHT_SLOTS
        cols = pl.ds(pl.multiple_of((s % n_chunks) * fc, fc), fc)
        return (pltpu.make_async_copy(wg_hbm.at[layer, :, cols], wg_buf.at[slot], sems.at[0, slot]),
                pltpu.make_async_copy(wu_hbm.at[layer, :, cols], wu_buf.at[slot], sems.at[1, slot]),
                pltpu.make_async_copy(wd_hbm.at[layer, cols, :], wd_buf.at[slot], sems.at[2, slot]))

    @pl.when(step == 0)
    def _():
        for s in range(WEIGHT_SLOTS - 1):
            @pl.when(s < n_steps)
            def _():
                for c in weight_copies(s):
                    c.start()

    @pl.when(step + WEIGHT_SLOTS - 1 < n_steps)
    def _():
        for c in weight_copies(step + WEIGHT_SLOTS - 1):
            c.start()

    for c in weight_copies(step):
        c.wait()
    slot = step % WEIGHT_SLOTS

    def partial_sum(rows):
        h = h_ref[rows, :]
        a = jnp.dot(h, wg_buf[slot], preferred_element_type=F32)
        b = jnp.dot(h, wu_buf[slot], preferred_element_type=F32)
        u = ((a * jax.nn.sigmoid(a)) * b).astype(BF16)
        return jnp.dot(u, wd_buf[slot], preferred_element_type=F32)

    @pl.when(j == 0)
    def _():
        out_ref[...] = partial_sum(slice(None))

    @pl.when(jnp.logical_and(j > 0, j < last))
    def _():
        out_ref[...] += partial_sum(slice(None))

    @pl.when(j == last)
    def _():
        gate = mod_ref[pl.ds(row_of_tile(pl.program_id(0)), 1), 5 * d:6 * d]
        gain = gate * g_ref[3:4, :]
        rows = out_ref.shape[0] // FFN_EPILOGUE_BLOCKS
        for r in range(FFN_EPILOGUE_BLOCKS):
            rs = slice(r * rows, (r + 1) * rows)
            acc = out_ref[rs, :] + partial_sum(rs)
            out_ref[rs, :] = x_ref[rs, :] + _rms(acc) * gain


def _ffn(h, x, mod, g_norm, w_gate, w_up, w_down, layer, row_of_tile, *, tm, fc):
    t, d = x.shape
    d_ff = w_gate.shape[-1]
    return pl.pallas_call(
        functools.partial(_ffn_kernel, d=d, fc=fc, layer=layer, row_of_tile=row_of_tile),
        out_shape=jax.ShapeDtypeStruct((t, d), F32),
        grid=(t // tm, d_ff // fc),
        in_specs=[
            pl.BlockSpec((tm, d), lambda i, j: (i, 0)),
            pl.BlockSpec((tm, d), lambda i, j: (i, 0)),
            pl.BlockSpec((None,) + mod.shape[1:], lambda i, j: (layer, 0, 0)),
            pl.BlockSpec((None, 4, d), lambda i, j: (layer, 0, 0)),
            pl.BlockSpec(memory_space=pl.ANY),
            pl.BlockSpec(memory_space=pl.ANY),
            pl.BlockSpec(memory_space=pl.ANY),
        ],
        out_specs=pl.BlockSpec((tm, d), lambda i, j: (i, 0)),
        scratch_shapes=[pltpu.VMEM((WEIGHT_SLOTS, d, fc), BF16), pltpu.VMEM((WEIGHT_SLOTS, d, fc), BF16),
                        pltpu.VMEM((WEIGHT_SLOTS, fc, d), BF16), pltpu.SemaphoreType.DMA((3, WEIGHT_SLOTS))],
        compiler_params=_params("arbitrary", "arbitrary"),
        name="ffn",
    )(h, x, mod, g_norm, w_gate, w_up, w_down)


def _rope_tables(n_tokens):
    nf = ROPE_PAIR_LANES
    t = jnp.arange(n_tokens, dtype=jnp.int32)
    rows = (t // GRID_W).astype(F32)
    cols = (t % GRID_W).astype(F32)
    inv = 1.0 / (ROPE_THETA ** (jnp.arange(nf, dtype=F32) / nf))
    ar = rows[:, None] * inv
    ac = cols[:, None] * inv
    cos = jnp.concatenate([jnp.cos(ar), jnp.cos(ar), jnp.cos(ac), jnp.cos(ac)], axis=-1)
    sin = jnp.concatenate([-jnp.sin(ar), jnp.sin(ar), -jnp.sin(ac), jnp.sin(ac)], axis=-1)
    return cos, sin


def _diff_lambda_init(layer):
    return 0.8 - 0.6 * math.exp(-0.3 * layer)


def _largest_tile(n, cap):
    t = cap
    while n % t:
        t //= 2
    return t


def _run_path(x3, first_row, rows_per_batch, caches, mod, p, depth):
    batch, seq, d = x3.shape
    latent = caches is not None
    x = x3.reshape(batch * seq, d)
    tm = _largest_tile(seq if latent else batch * seq, TOKEN_TILE)
    gqa_tq = _largest_tile(seq, LATENT_GQA_TQ if latent else ATTN_TQ)

    def row_of_tile(i):
        if not latent:
            return first_row
        return first_row + (i // (seq // tm)) * rows_per_batch

    rope_tables = _rope_tables(seq) if latent else None
    new_state = [None] * N_MIXERS
    for layer in range(depth):
        kind, slot = layer % N_MIXERS, layer // N_MIXERS
        cache = None if not latent else (caches[2 * kind], caches[2 * kind + 1])
        common = dict(tm=tm, rope_tables=rope_tables)
        if not latent:
            n_slots = (depth - kind + N_MIXERS - 1) // N_MIXERS
            common.update(state_seq=seq, state_slots=n_slots, state_prev=new_state[kind])
        attn_shape = dict(kv_heads=1 if latent else KV_HEADS)
        if kind == 0:
            q, k, v = _qkv(x, mod, p['g_norm'], p['w_qkv_full'], layer, slot, row_of_tile,
                           nq=N_HEADS * HEAD_DIM, nk=KV_HEADS * HEAD_DIM,
                           gains=(p['g_q_full'], p['g_k_full']), **common)
            o = _gqa_attention(q, k, v, cache, slot, None, batch=batch, seq=seq, band=False, tq=gqa_tq,
                               block_rows=DENSE_BLOCK_ROWS if latent else ATTN_BLOCK_ROWS, **attn_shape)
            w_o = p['w_o_full']
        elif kind == 1:
            q, k, v = _qkv(x, mod, p['g_norm'], p['w_qkv_win'], layer, slot, row_of_tile,
                           nq=N_HEADS * HEAD_DIM, nk=KV_HEADS * HEAD_DIM, **common)
            o = _gqa_attention(q, k, v, cache, slot, p['sink_win'][slot], batch=batch, seq=seq, band=latent,
                               tq=gqa_tq, block_rows=ATTN_BLOCK_ROWS, **attn_shape)
            w_o = p['w_o_win']
        else:
            q, k, v = _qkv(x, mod, p['g_norm'], p['w_qkv_diff'], layer, slot, row_of_tile,
                           nq=DIFF_HEADS * 2 * DIFF_DIM, nk=DIFF_KV_HEADS * 2 * DIFF_DIM, **common)
            o = _diff_attention(q, k, v, cache, slot,
                                (p['lam_q1'], p['lam_k1'], p['lam_q2'], p['lam_k2']), p['g_subln_diff'],
                                batch=batch, seq=seq, tq=_largest_tile(seq, ATTN_TQ),
                                block_rows=2 * ATTN_BLOCK_ROWS,
                                lam_init=_diff_lambda_init(layer))
            w_o = p['w_o_diff']
        if not latent:
            new_state[kind] = (k, v)
        x, h = _oproj(o, w_o, x, mod, p['g_norm'], layer, slot, row_of_tile, tm=tm)
        x = _ffn(h, x, mod, p['g_norm'], p['w_gate'], p['w_up'], p['w_down'], layer, row_of_tile,
                 tm=tm, fc=FFN_CHUNK)
    return x.reshape(batch, seq, d), new_state


def kernel(x_prompt, x_sample, c, cache_k_full, cache_v_full, cache_k_win, cache_v_win, cache_k_diff,
           cache_v_diff, c_ctx, w_ada, b_ada, g_norm, w_qkv_full, w_o_full, g_q_full, g_k_full, w_qkv_win,
           w_o_win, sink_win, w_qkv_diff, w_o_diff, lam_q1, lam_k1, lam_q2, lam_k2, g_subln_diff, w_gate,
           w_up, w_down):
    depth, d = w_ada.shape[0], w_ada.shape[1]
    batch, seq, _ = x_prompt.shape
    dec_batch, dec_seq, _ = x_sample.shape

    p = {
        'g_norm': g_norm,
        'w_qkv_full': w_qkv_full.astype(BF16), 'w_o_full': w_o_full.astype(BF16),
        'g_q_full': g_q_full, 'g_k_full': g_k_full,
        'w_qkv_win': w_qkv_win.astype(BF16), 'w_o_win': w_o_win.astype(BF16), 'sink_win': sink_win,
        'w_qkv_diff': w_qkv_diff.astype(BF16), 'w_o_diff': w_o_diff.astype(BF16),
        'lam_q1': lam_q1, 'lam_k1': lam_k1, 'lam_q2': lam_q2, 'lam_k2': lam_k2,
        'g_subln_diff': g_subln_diff,
        'w_gate': w_gate.astype(BF16), 'w_up': w_up.astype(BF16), 'w_down': w_down.astype(BF16),
    }

    n_rows = -(-(1 + dec_batch) // SUBLANES) * SUBLANES
    cond = jnp.zeros((n_rows, d), F32).at[0].set(c_ctx).at[1:1 + dec_batch].set(c)
    mod = _modulation(cond, w_ada, b_ada)

    y_prompt, new_state = _run_path(x_prompt, 0, 0, None, mod, p, depth)

    def head_rows_cache(a):
        return a.reshape(a.shape[0], a.shape[1], -1, LANES)

    caches = tuple(head_rows_cache(a) for a in (cache_k_full, cache_v_full, cache_k_win, cache_v_win,
                                                cache_k_diff, cache_v_diff))
    y_sample, _ = _run_path(x_sample, 1, 1, caches, mod, p, depth)

    def state(a, tail):
        return a.reshape(a.shape[:2] + (seq,) + tail)

    (k_full, v_full), (k_win, v_win), (k_diff, v_diff) = new_state
    return (y_prompt, y_sample,
            state(k_full, (KV_HEADS, HEAD_DIM)), state(v_full, (KV_HEADS, HEAD_DIM)),
            state(k_win, (KV_HEADS, HEAD_DIM)), state(v_win, (KV_HEADS, HEAD_DIM)),
            state(k_diff, (DIFF_KV_HEADS, 2, DIFF_DIM)), state(v_diff, (DIFF_KV_HEADS, 2 * DIFF_DIM)))
```
